```python
import jax
import jax.numpy as jnp
from jax import lax
import numpy as np

D_MODEL = 1024
BATCH = 8
SEQ = 2048
DEPTH = 1
DEC_BATCH = 128
DEC_SEQ = 8
PAST_LEN = 16384
PAGE_SIZE = 128

DN_HEADS = 8
DN_HD = 128
DN_DIM = DN_HEADS * DN_HD
QKV_DIM = 3 * DN_DIM
CONV_W = 4
DN_CHUNK = 64
POOL_WINDOWS = (2, 4, 8, 16)
POOL_GROUPS = 4
POOL_DIM = D_MODEL
POOL_GD = POOL_DIM // POOL_GROUPS
POOL_MAX = 16
MEM_LEN = 256
CA_HEADS = 4
CA_HD = D_MODEL // CA_HEADS
MOE_GROUPS = 8
MOE_EPG = 8
MOE_EXPERTS = MOE_GROUPS * MOE_EPG
MOE_TOPK = 2
MOE_FF = D_MODEL // 4
MOE_BLOCK = 128
N_IN = QKV_DIM + DN_DIM + 2 * DN_HEADS + POOL_DIM + 2 * D_MODEL
LN_EPS = 1e-5
NORM_EPS = 1e-6

kernel_name = "hybrid_deltanet_pool_hmoe_decoder_step"


def layer_norm(x, g, b):
    xf = x.astype(jnp.float32)
    mu = jnp.mean(xf, -1, keepdims=True)
    var = jnp.mean(jnp.square(xf - mu), -1, keepdims=True)
    return ((xf - mu) * lax.rsqrt(var + LN_EPS) * g + b).astype(x.dtype)


def l2norm(x):
    return x * lax.rsqrt(jnp.sum(x * x, -1, keepdims=True) + NORM_EPS)


def causal_short_conv(u, prev, w):
    L = u.shape[1]
    full = jnp.concatenate([prev, u], axis=1)
    y = sum(full[:, i:i + L] * w[i] for i in range(CONV_W))
    return jax.nn.silu(y), full[:, -(CONV_W - 1):]


def gated_delta_rule(q, k, v, g, beta, S0):
    B, L, H, DK = q.shape
    DV = v.shape[-1]
    C = min(DN_CHUNK, L)
    n = -(-L // C)
    pad = n * C - L

    def to_chunks(t):
        t = jnp.pad(t, [(0, 0), (0, pad)] + [(0, 0)] * (t.ndim - 2))
        t = t.reshape((B, n, C) + t.shape[2:])
        return jnp.moveaxis(t, 3, 2)

    qc, kc, vc, gc, bc = (to_chunks(t) for t in (q, k, v, g, beta))
    gcum = jnp.cumsum(gc, axis=-1)
    idx = jnp.arange(C)
    causal = idx[:, None] >= idx[None, :]
    strict = idx[:, None] > idx[None, :]
    decay = jnp.exp(jnp.where(causal, gcum[..., :, None] - gcum[..., None, :], -jnp.inf))
    kb = kc * bc[..., None]
    A = jnp.where(strict, jnp.einsum('bnhid,bnhjd->bnhij', kb, kc) * decay, 0.0)
    rhs = jnp.concatenate([vc * bc[..., None], kb * jnp.exp(gcum)[..., None]], axis=-1)
    sol = lax.linalg.triangular_solve(A + jnp.eye(C, dtype=A.dtype), rhs,
                                      left_side=True, lower=True, unit_diagonal=True)
    u, w = sol[..., :DV], sol[..., DV:]
    qk = jnp.where(causal, jnp.einsum('bnhid,bnhjd->bnhij', qc, kc) * decay, 0.0)
    qg = qc * jnp.exp(gcum)[..., None]
    kg = kc * jnp.exp(gcum[..., -1:] - gcum)[..., None]
    glast = jnp.exp(gcum[..., -1])

    def step(S, inp):
        qg_i, kg_i, u_i, w_i, qk_i, gl_i = inp
        v_new = u_i - jnp.einsum('bhck,bhkv->bhcv', w_i, S)
        o = jnp.einsum('bhck,bhkv->bhcv', qg_i, S) + jnp.einsum('bhij,bhjv->bhiv', qk_i, v_new)
        S = S * gl_i[..., None, None] + jnp.einsum('bhck,bhcv->bhkv', kg_i, v_new)
        return S, o

    xs = tuple(jnp.moveaxis(t, 1, 0) for t in (qg, kg, u, w, qk, glast))
    S, o = lax.scan(step, S0, xs)
    o = jnp.moveaxis(jnp.moveaxis(o, 0, 1), 2, 3).reshape(B, n * C, H, DV)[:, :L]
    return o, S


def multiscale_pool(p, prev, pos0):
    B, L, P = p.shape
    full = jnp.concatenate([prev, p], axis=1)
    cs = jnp.cumsum(full.astype(jnp.float32), axis=1)
    cs = jnp.concatenate([jnp.zeros((B, 1, P), jnp.float32), cs], axis=1)
    off = POOL_MAX
    pos = pos0 + jnp.arange(L)
    outs = []
    for gi, win in enumerate(POOL_WINDOWS):
        sl = slice(gi * POOL_GD, (gi + 1) * POOL_GD)
        s = cs[:, off:off + L, sl] - cs[:, off - win:off - win + L, sl]
        cnt = jnp.minimum(win, pos + 1).astype(jnp.float32)
        outs.append(s / cnt[None, :, None])
    pooled = jnp.concatenate(outs, axis=-1) - p.astype(jnp.float32)
    return pooled, full[:, -(POOL_MAX - 1):]


def memory_kv(mem, w_ck, w_cv):
    B, M, _ = mem.shape
    return ((mem @ w_ck).reshape(B, M, CA_HEADS, CA_HD), (mem @ w_cv).reshape(B, M, CA_HEADS, CA_HD))


def memory_attention(x, mem_k, mem_v, w_cq, w_co):
    B, L, D = x.shape
    q = (x @ w_cq).reshape(B, L, CA_HEADS, CA_HD)
    s = jnp.einsum('blhd,bmhd->bhlm', q, mem_k, preferred_element_type=jnp.float32) * (CA_HD ** -0.5)
    p = jax.nn.softmax(s, axis=-1)
    o = jnp.einsum('bhlm,bmhd->blhd', p.astype(x.dtype), mem_v).reshape(B, L, D)
    return o @ w_co


def hier_moe(x, w_rg, w_re, w_gate, w_up, w_down):
    B, L, D = x.shape
    T = B * L
    xf = x.reshape(T, D)
    g_logits = (xf @ w_rg).astype(jnp.float32)
    g_prob = jax.nn.softmax(g_logits, axis=-1)
    g_sel = jnp.argmax(g_logits, axis=-1).astype(jnp.int32)
    g_w = jnp.take_along_axis(g_prob, g_sel[:, None], axis=-1)
    e_logits = (xf @ w_re).astype(jnp.float32).reshape(T, MOE_GROUPS, MOE_EPG)
    e_logits = jnp.take_along_axis(e_logits, g_sel[:, None, None], axis=1)[:, 0]
    top_v, top_i = lax.top_k(e_logits, MOE_TOPK)
    e_w = jax.nn.softmax(top_v, axis=-1) * g_w
    e_idx = g_sel[:, None] * MOE_EPG + top_i.astype(jnp.int32)

    A = T * MOE_TOPK
    E = MOE_EXPERTS
    blk = MOE_BLOCK
    e_flat = e_idx.reshape(A)
    tok = jnp.repeat(jnp.arange(T, dtype=jnp.int32), MOE_TOPK)
    wts = e_w.reshape(A)
    order = jnp.argsort(e_flat)
    e_sorted = e_flat[order]
    counts = jnp.zeros((E,), jnp.int32).at[e_flat].add(1)
    start = jnp.cumsum(counts) - counts
    padded = (counts + blk - 1) // blk * blk
    pend = jnp.cumsum(padded)
    pstart = pend - padded
    dest = pstart[e_sorted] + (jnp.arange(A, dtype=jnp.int32) - start[e_sorted])
    n_blocks = (A + E * (blk - 1) + blk - 1) // blk
    R = n_blocks * blk
    row_tok = jnp.zeros((R,), jnp.int32).at[dest].set(tok[order])
    row_w = jnp.zeros((R,), jnp.float32).at[dest].set(wts[order])
    block_start = jnp.arange(n_blocks, dtype=jnp.int32) * blk
    block_expert = jnp.minimum(jnp.searchsorted(pend, block_start, side='right'), E - 1).astype(jnp.int32)

    def block_fn(args):
        toks, e = args
        xb = xf[toks]
        h = jax.nn.silu(xb @ w_gate[e]) * (xb @ w_up[e])
        return h @ w_down[e]

    out_rows = lax.map(block_fn, (row_tok.reshape(n_blocks, blk), block_expert))
    y = jnp.zeros((T, D), jnp.float32).at[row_tok].add(out_rows.reshape(R, D).astype(jnp.float32) * row_w[:, None])
    return y.reshape(B, L, D).astype(x.dtype)


def decoder_layer(x, mem_k, mem_v, conv_prev, pool_prev, S0, pos0, alpha,
                  w_in, w_conv, a_log, dt_bias, w_onorm, w_pool, pool_scale, w_out, ln1_g, ln1_b,
                  w_cq, w_co, ln2_g, ln2_b, w_rg, w_re, w_gate, w_up, w_down, ln3_g, ln3_b):
    B, L, D = x.shape
    f32 = jnp.float32
    proj = x @ w_in
    o1 = QKV_DIM
    o2 = o1 + DN_DIM
    o3 = o2 + DN_HEADS
    o4 = o3 + DN_HEADS
    o5 = o4 + POOL_DIM
    o6 = o5 + D_MODEL
    qkv_raw, z, a_raw, b_raw, p_raw, gate_a, gate_b = jnp.split(proj, [o1, o2, o3, o4, o5, o6], axis=-1)

    qkv, conv_new = causal_short_conv(qkv_raw, conv_prev, w_conv)
    q, k, v = jnp.split(qkv.astype(f32), 3, axis=-1)
    q = l2norm(q.reshape(B, L, DN_HEADS, DN_HD)) * (DN_HD ** -0.5)
    k = l2norm(k.reshape(B, L, DN_HEADS, DN_HD))
    v = v.reshape(B, L, DN_HEADS, DN_HD)
    beta = jax.nn.sigmoid(b_raw.astype(f32))
    g = -jnp.exp(a_log.astype(f32)) * jax.nn.softplus(a_raw.astype(f32) + dt_bias.astype(f32))
    o, S_new = gated_delta_rule(q, k, v, g, beta, S0.astype(f32))
    o = o * lax.rsqrt(jnp.mean(o * o, -1, keepdims=True) + NORM_EPS) * w_onorm
    o = o * jax.nn.silu(z.astype(f32).reshape(B, L, DN_HEADS, DN_HD))
    branch_a = o.reshape(B, L, DN_DIM).astype(x.dtype)

    pooled, pool_new = multiscale_pool(p_raw, pool_prev, pos0)
    pooled = jnp.einsum('blgc,gcd->blgd', pooled.reshape(B, L, POOL_GROUPS, POOL_GD), w_pool.astype(f32))
    branch_b = (pooled.reshape(B, L, POOL_DIM) * pool_scale).astype(x.dtype)

    merged = jax.nn.sigmoid(gate_a) * branch_a + jax.nn.sigmoid(gate_b) * branch_b
    x1 = layer_norm(alpha * x + merged @ w_out, ln1_g, ln1_b)
    x2 = layer_norm(alpha * x1 + memory_attention(x1, mem_k, mem_v, w_cq, w_co), ln2_g, ln2_b)
    x3 = layer_norm(alpha * x2 + hier_moe(x2, w_rg, w_re, w_gate, w_up, w_down), ln3_g, ln3_b)
    return x3, S_new.astype(S0.dtype), conv_new, pool_new


def setup_inputs(seed: int = 0) -> dict:
    key = jax.random.key(seed)
    ks = jax.random.split(key, 40)
    f32 = jnp.float32
    beta_dn = (8 * DEPTH) ** -0.25

    def nrm(k, shape, scale):
        return jax.random.normal(k, shape, f32) * scale

    dt = jnp.exp(jax.random.uniform(ks[11], (DEPTH, DN_HEADS), f32, np.log(1e-3), np.log(1e-1)))
    return {
        "x_prompt": nrm(ks[0], (BATCH, SEQ, D_MODEL), 1.0),
        "x_sample": nrm(ks[1], (DEC_BATCH, DEC_SEQ, D_MODEL), 1.0),
        "cache_mem_k": nrm(ks[2], (DEPTH, DEC_BATCH, MEM_LEN, CA_HEADS, CA_HD), 1.0),
        "cache_mem_v": nrm(ks[3], (DEPTH, DEC_BATCH, MEM_LEN, CA_HEADS, CA_HD), 1.0),
        "state_delta": nrm(ks[4], (DEPTH, DEC_BATCH, DN_HEADS, DN_HD, DN_HD), 0.5),
        "state_conv": nrm(ks[5], (DEPTH, DEC_BATCH, CONV_W - 1, QKV_DIM), 1.0),
        "state_pool": nrm(ks[6], (DEPTH, DEC_BATCH, POOL_MAX - 1, POOL_DIM), 1.0),
        "mem_prompt": nrm(ks[7], (BATCH, MEM_LEN, D_MODEL), 1.0),
        "w_in": nrm(ks[8], (DEPTH, D_MODEL, N_IN), D_MODEL ** -0.5),
        "w_conv": nrm(ks[9], (DEPTH, CONV_W, QKV_DIM), CONV_W ** -0.5),
        "a_log": jnp.log(jax.random.uniform(ks[10], (DEPTH, DN_HEADS), f32, 1.0, 16.0)),
        "dt_bias": dt + jnp.log(-jnp.expm1(-dt)),
        "w_onorm": 1.0 + nrm(ks[12], (DEPTH, DN_HD), 0.05),
        "w_pool": nrm(ks[13], (DEPTH, POOL_GROUPS, POOL_GD, POOL_GD), POOL_GD ** -0.5),
        "pool_scale": 1.0 + nrm(ks[14], (DEPTH, POOL_DIM), 0.05),
        "w_out": nrm(ks[15], (DEPTH, D_MODEL, D_MODEL), beta_dn * D_MODEL ** -0.5),
        "ln1_g": 1.0 + nrm(ks[16], (DEPTH, D_MODEL), 0.05),
        "ln1_b": nrm(ks[17], (DEPTH, D_MODEL), 0.02),
        "w_cq": nrm(ks[18], (DEPTH, D_MODEL, D_MODEL), D_MODEL ** -0.5),
        "w_ck": nrm(ks[19], (DEPTH, D_MODEL, D_MODEL), D_MODEL ** -0.5),
        "w_cv": nrm(ks[20], (DEPTH, D_MODEL, D_MODEL), D_MODEL ** -0.5),
        "w_co": nrm(ks[21], (DEPTH, D_MODEL, D_MODEL), beta_dn * D_MODEL ** -0.5),
        "ln2_g": 1.0 + nrm(ks[22], (DEPTH, D_MODEL), 0.05),
        "ln2_b": nrm(ks[23], (DEPTH, D_MODEL), 0.02),
        "w_router_group": nrm(ks[24], (DEPTH, D_MODEL, MOE_GROUPS), D_MODEL ** -0.5),
        "w_router_expert": nrm(ks[25], (DEPTH, D_MODEL, MOE_EXPERTS), D_MODEL ** -0.5),
        "w_gate": nrm(ks[26], (DEPTH, MOE_EXPERTS, D_MODEL, MOE_FF), D_MODEL ** -0.5),
        "w_up": nrm(ks[27], (DEPTH, MOE_EXPERTS, D_MODEL, MOE_FF), D_MODEL ** -0.5),
        "w_down": nrm(ks[28], (DEPTH, MOE_EXPERTS, MOE_FF, D_MODEL), beta_dn * MOE_FF ** -0.5),
        "ln3_g": 1.0 + nrm(ks[29], (DEPTH, D_MODEL), 0.05),
        "ln3_b": nrm(ks[30], (DEPTH, D_MODEL), 0.02),
    }


def reference(x_prompt, x_sample, cache_mem_k, cache_mem_v, state_delta, state_conv, state_pool, mem_prompt,
              w_in, w_conv, a_log, dt_bias, w_onorm, w_pool, pool_scale, w_out, ln1_g, ln1_b,
              w_cq, w_ck, w_cv, w_co, ln2_g, ln2_b, w_router_group, w_router_expert, w_gate, w_up, w_down,
              ln3_g, ln3_b):
    alpha = (2 * DEPTH) ** 0.25
    h_p, h_s = x_prompt, x_sample
    Bp = x_prompt.shape[0]
    dtp = x_prompt.dtype
    d_p, c_p, p_p, mk_p, mv_p, d_s, c_s, p_s = [], [], [], [], [], [], [], []
    for l in range(DEPTH):
        lw = (w_in[l], w_conv[l], a_log[l], dt_bias[l], w_onorm[l], w_pool[l], pool_scale[l], w_out[l],
              ln1_g[l], ln1_b[l], w_cq[l], w_co[l], ln2_g[l], ln2_b[l], w_router_group[l], w_router_expert[l],
              w_gate[l], w_up[l], w_down[l], ln3_g[l], ln3_b[l])
        mk, mv = memory_kv(mem_prompt, w_ck[l], w_cv[l])
        S0 = jnp.zeros((Bp, DN_HEADS, DN_HD, DN_HD), dtp)
        conv0 = jnp.zeros((Bp, CONV_W - 1, QKV_DIM), dtp)
        pool0 = jnp.zeros((Bp, POOL_MAX - 1, POOL_DIM), dtp)
        h_p, dl, cl, pl = decoder_layer(h_p, mk, mv, conv0, pool0, S0, 0, alpha, *lw)
        d_p.append(dl)
        c_p.append(cl)
        p_p.append(pl)
        mk_p.append(mk)
        mv_p.append(mv)
        h_s, dl, cl, pl = decoder_layer(h_s, cache_mem_k[l], cache_mem_v[l], state_conv[l], state_pool[l],
                                        state_delta[l], PAST_LEN, alpha, *lw)
        d_s.append(dl)
        c_s.append(cl)
        p_s.append(pl)
    return (h_p, h_s, jnp.stack(d_p), jnp.stack(c_p), jnp.stack(p_p), jnp.stack(mk_p), jnp.stack(mv_p),
            jnp.stack(d_s), jnp.stack(c_s), jnp.stack(p_s))
```

```python
import functools

import jax
import jax.numpy as jnp
from jax import lax
from jax.experimental import pallas as pl
from jax.experimental.pallas import tpu as pltpu

F32 = jnp.float32
BF16 = jnp.bfloat16

D_MODEL = 1024
DN_HEADS = 8
DN_HD = 128
QKV_DIM = 3 * DN_HEADS * DN_HD
CONV_W = 4
DN_CHUNK = 64
POOL_WINDOWS = (2, 4, 8, 16)
POOL_GD = D_MODEL // len(POOL_WINDOWS)
POOL_MAX = 16
CA_HEADS = 4
CA_HD = D_MODEL // CA_HEADS
MOE_GROUPS = 8
MOE_EPG = 8
MOE_EXPERTS = MOE_GROUPS * MOE_EPG
MOE_FF = D_MODEL // 4
MOE_BLOCK = 128
PAST_LEN = 16384
LN_EPS = 1e-5
NORM_EPS = 1e-6
ALPHA = 2.0 ** 0.25

LANES = 128
SUBLANES = 8
CONV_HALO = SUBLANES
POOL_HALO = POOL_MAX
MIN_CHUNK = 16
N_PROJ = 7 * D_MODEL + LANES
VMEM_LIMIT = 48 * 1024 * 1024


def _cparams(n_axes, vmem=VMEM_LIMIT):
    return pltpu.CompilerParams(dimension_semantics=("arbitrary",) * n_axes, vmem_limit_bytes=vmem)


def _layer_norm(x, g, b):
    mu = jnp.mean(x, -1, keepdims=True)
    xc = x - mu
    var = jnp.mean(xc * xc, -1, keepdims=True)
    return xc * lax.rsqrt(var + LN_EPS) * g + b


def _softplus(x):
    return jnp.maximum(x, 0.0) + jnp.log1p(jnp.exp(-jnp.abs(x)))


def _dot(a, b):
    return jnp.dot(a, b, preferred_element_type=F32)


def _dot_nt(a, b):
    return lax.dot_general(a, b, (((1,), (1,)), ((), ())), preferred_element_type=F32)


def _dot_tn(a, b):
    return lax.dot_general(a, b, (((0,), (0,)), ((), ())), preferred_element_type=F32)


def _mm_body(x_ref, w_ref, o_ref):
    o_ref[...] = _dot(x_ref[...].astype(BF16), w_ref[...]).astype(o_ref.dtype)


def _mm(x, w, *, tm, tn, out_dtype=F32):
    T, K = x.shape
    N = w.shape[1]
    return pl.pallas_call(
        _mm_body,
        grid=(N // tn, T // tm),
        in_specs=[pl.BlockSpec((tm, K), lambda j, i: (i, 0)), pl.BlockSpec((K, tn), lambda j, i: (0, j))],
        out_specs=pl.BlockSpec((tm, tn), lambda j, i: (i, j)),
        out_shape=jax.ShapeDtypeStruct((T, N), out_dtype),
        compiler_params=_cparams(2),
        name="mm",
    )(x, w)


def _mixer_body(*refs, tl, chunk, pos0, has_state):
    if has_state:
        (qkv_ref, z_ref, p_ref, ga_ref, gb_ref, ab_ref, convp_ref, poolp_ref, s0_ref, wconv_ref, alog_ref, dtb_ref,
         wonorm_ref, wpool_ref, pscale_ref, merged_ref, sout_ref, qkvbuf, pbuf, obuf, gcb, betab, gt_ref) = refs
    else:
        (qkv_ref, z_ref, p_ref, ga_ref, gb_ref, ab_ref, wconv_ref, alog_ref, dtb_ref,
         wonorm_ref, wpool_ref, pscale_ref, merged_ref, sout_ref, qkvbuf, pbuf, obuf, gcb, betab, gt_ref) = refs
    l = pl.program_id(1)
    tlp = max(tl, MIN_CHUNK)
    C = max(chunk, MIN_CHUNK)
    n_chunks = tlp // C

    @pl.when(l == 0)
    def _init():
        qkvbuf[0:CONV_HALO, :] = jnp.zeros((CONV_HALO, QKV_DIM), F32)
        pbuf[0:POOL_HALO, :] = jnp.zeros((POOL_HALO, D_MODEL), F32)
        if has_state:
            qkvbuf[CONV_HALO - (CONV_W - 1):CONV_HALO, :] = convp_ref[0]
            pbuf[POOL_HALO - (POOL_MAX - 1):POOL_HALO, :] = poolp_ref[0]
            sout_ref[...] = s0_ref[...]
        else:
            sout_ref[...] = jnp.zeros(sout_ref.shape, F32)

    qkvbuf[CONV_HALO:CONV_HALO + tl, :] = qkv_ref[...]
    pbuf[POOL_HALO:POOL_HALO + tl, :] = p_ref[...]

    def pad_rows(x):
        if tlp == tl:
            return x
        return jnp.concatenate([x, jnp.zeros((tlp - tl, x.shape[1]), F32)], axis=0)

    ab = ab_ref[...]
    g_all = pad_rows(-jnp.exp(alog_ref[...]) * _softplus(ab + dtb_ref[...]))
    beta_all = pad_rows(jax.nn.sigmoid(ab))
    ri = lax.broadcasted_iota(jnp.int32, (tlp, tlp), 0)
    ci = lax.broadcasted_iota(jnp.int32, (tlp, tlp), 1)
    cum_mat = ((ri >= ci) & ((ri // C) == (ci // C))).astype(F32)
    gcum = jnp.dot(cum_mat, g_all, preferred_element_type=F32, precision=lax.Precision.HIGHEST)
    gcum_t = gcum.T
    for h in range(DN_HEADS):
        gt_ref[h] = jnp.broadcast_to(gcum_t[h:h + 1, :], (SUBLANES, tlp))
        gcb[h] = jnp.broadcast_to(gcum[:, h:h + 1], (tlp, LANES))
        betab[h] = jnp.broadcast_to(beta_all[:, DN_HEADS + h:DN_HEADS + h + 1], (tlp, LANES))

    ii = lax.broadcasted_iota(jnp.int32, (C, C), 0)
    jj = lax.broadcasted_iota(jnp.int32, (C, C), 1)
    causal = ii >= jj
    strict = ii > jj
    eye = (ii == jj).astype(F32)
    n_sq = max(C.bit_length() - 2, 0)

    def head_body(h, carry):
        off = pl.multiple_of(h * DN_HD, DN_HD)

        def conv_slab(base):
            cols = pl.ds(base + off, DN_HD)
            acc = qkvbuf[pl.ds(CONV_HALO, tl), cols] * wconv_ref[CONV_W - 1:CONV_W, cols]
            for i in range(CONV_W - 1):
                r0 = CONV_HALO - (CONV_W - 1) + i
                acc = acc + qkvbuf[pl.ds(r0, tl), cols] * wconv_ref[i:i + 1, cols]
            return pad_rows(acc * jax.nn.sigmoid(acc))

        q = conv_slab(0)
        k = conv_slab(DN_HEADS * DN_HD)
        v = conv_slab(2 * DN_HEADS * DN_HD)
        q = q * lax.rsqrt(jnp.sum(q * q, -1, keepdims=True) + NORM_EPS) * (DN_HD ** -0.5)
        k = k * lax.rsqrt(jnp.sum(k * k, -1, keepdims=True) + NORM_EPS)
        gc_all = gcb[h]
        b_all = betab[h]
        eg_all = jnp.exp(gc_all)
        grow_all = gt_ref[h][0:1, :]
        kb_all = k * b_all
        S = sout_ref[0, h]
        outs = []
        for c in range(n_chunks):
            rs = slice(c * C, (c + 1) * C)
            qc, kc, vc, kb, gc, eg, bc = q[rs], k[rs], v[rs], kb_all[rs], gc_all[rs], eg_all[rs], b_all[rs]
            grow = grow_all[:, c * C:(c + 1) * C]
            decay = jnp.exp(jnp.where(causal, gc[:, 0:C] - grow, -jnp.inf))
            aq = _dot_nt(jnp.concatenate([kb, qc], axis=0).astype(BF16), kc.astype(BF16))
            A = jnp.where(strict, aq[0:C] * decay, 0.0)
            qk = aq[C:2 * C] * decay
            P = eye - A
            Ab = A.astype(BF16)
            Q = _dot(Ab, Ab)
            for _ in range(n_sq - 1):
                Qb = Q.astype(BF16)
                pq = _dot(jnp.concatenate([P, Q], axis=0).astype(BF16), Qb)
                P = P + pq[0:C]
                Q = pq[C:2 * C]
            P = P + _dot(P.astype(BF16), Q.astype(BF16))
            rhs = jnp.concatenate([vc * bc, kb * eg], axis=1)
            sol = _dot(P.astype(BF16), rhs.astype(BF16))
            u, w = sol[:, 0:DN_HD], sol[:, DN_HD:2 * DN_HD]
            ws_qs = _dot(jnp.concatenate([w, qc * eg], axis=0).astype(BF16), S.astype(BF16))
            v_new = u - ws_qs[0:C]
            v_new_b = v_new.astype(BF16)
            outs.append(ws_qs[C:2 * C] + _dot(qk.astype(BF16), v_new_b))
            glast = gc[C - 1:C, :]
            kg = kc * jnp.exp(glast - gc)
            S = S * jnp.exp(glast) + _dot_tn(kg.astype(BF16), v_new_b)
        sout_ref[0, h] = S
        o = outs[0] if n_chunks == 1 else jnp.concatenate(outs, axis=0)
        o = o[0:tl]
        o = o * lax.rsqrt(jnp.mean(o * o, -1, keepdims=True) + NORM_EPS) * wonorm_ref[...]
        zh = z_ref[:, pl.ds(off, DN_HD)]
        obuf[:, pl.ds(off, DN_HD)] = o * (zh * jax.nn.sigmoid(zh))
        return carry

    lax.fori_loop(0, DN_HEADS, head_body, 0)

    if pos0 == 0:
        pos = l * tl + lax.broadcasted_iota(jnp.int32, (tl, 1), 0)
    for gi, win in enumerate(POOL_WINDOWS):
        cs = slice(gi * POOL_GD, (gi + 1) * POOL_GD)
        s = pbuf[POOL_HALO:POOL_HALO + tl, cs]
        for j in range(1, win):
            s = s + pbuf[POOL_HALO - j:POOL_HALO - j + tl, cs]
        if pos0 == 0:
            cnt = jnp.minimum(win, pos + 1).astype(F32)
        else:
            cnt = float(min(win, pos0 + 1))
        pooled = s / cnt - p_ref[:, cs]
        bb = _dot(pooled.astype(BF16), wpool_ref[gi]) * pscale_ref[:, cs]
        merged_ref[:, cs] = (jax.nn.sigmoid(ga_ref[:, cs]) * obuf[:, cs]
                             + jax.nn.sigmoid(gb_ref[:, cs]) * bb).astype(merged_ref.dtype)

    if tl >= POOL_HALO:
        qkvbuf[0:CONV_HALO, :] = qkvbuf[tl:tl + CONV_HALO, :]
        pbuf[0:POOL_HALO, :] = pbuf[tl:tl + POOL_HALO, :]


def _mixer(proj, B, L, conv_prev, pool_prev, s0, wconv, alog, dtb, wonorm, wpool, pscale, *, pos0):
    tl = min(256, L)
    nL = L // tl
    assert nL == 1 or tl >= POOL_HALO
    chunk = min(DN_CHUNK, tl)
    tlp = max(tl, MIN_CHUNK)
    has_state = s0 is not None
    row = lambda b, l: b * nL + l
    in_specs = [
        pl.BlockSpec((tl, QKV_DIM), lambda b, l: (row(b, l), 0)),
        pl.BlockSpec((tl, D_MODEL), lambda b, l: (row(b, l), 3)),
        pl.BlockSpec((tl, D_MODEL), lambda b, l: (row(b, l), 4)),
        pl.BlockSpec((tl, D_MODEL), lambda b, l: (row(b, l), 5)),
        pl.BlockSpec((tl, D_MODEL), lambda b, l: (row(b, l), 6)),
        pl.BlockSpec((tl, LANES), lambda b, l: (row(b, l), 7 * D_MODEL // LANES)),
    ]
    args = [proj] * 6
    if has_state:
        in_specs += [
            pl.BlockSpec((1, CONV_W - 1, QKV_DIM), lambda b, l: (b, 0, 0)),
            pl.BlockSpec((1, POOL_MAX - 1, D_MODEL), lambda b, l: (b, 0, 0)),
            pl.BlockSpec((1, DN_HEADS, DN_HD, DN_HD), lambda b, l: (b, 0, 0, 0)),
        ]
        args += [conv_prev, pool_prev, s0]
    const2 = lambda b, l: (0, 0)
    in_specs += [
        pl.BlockSpec((CONV_W, QKV_DIM), const2),
        pl.BlockSpec((1, LANES), const2),
        pl.BlockSpec((1, LANES), const2),
        pl.BlockSpec((1, DN_HD), const2),
        pl.BlockSpec((len(POOL_WINDOWS), POOL_GD, POOL_GD), lambda b, l: (0, 0, 0)),
        pl.BlockSpec((1, D_MODEL), const2),
    ]
    args += [wconv, alog, dtb, wonorm, wpool, pscale]
    return pl.pallas_call(
        functools.partial(_mixer_body, tl=tl, chunk=chunk, pos0=pos0, has_state=has_state),
        grid=(B, nL),
        in_specs=in_specs,
        out_specs=[
            pl.BlockSpec((tl, D_MODEL), lambda b, l: (row(b, l), 0)),
            pl.BlockSpec((1, DN_HEADS, DN_HD, DN_HD), lambda b, l: (b, 0, 0, 0)),
        ],
        out_shape=[
            jax.ShapeDtypeStruct((B * L, D_MODEL), BF16),
            jax.ShapeDtypeStruct((B, DN_HEADS, DN_HD, DN_HD), F32),
        ],
        scratch_shapes=[
            pltpu.VMEM((CONV_HALO + tl, QKV_DIM), F32),
            pltpu.VMEM((POOL_HALO + tl, D_MODEL), F32),
            pltpu.VMEM((tl, D_MODEL), F32),
            pltpu.VMEM((DN_HEADS, tlp, LANES), F32),
            pltpu.VMEM((DN_HEADS, tlp, LANES), F32),
            pltpu.VMEM((DN_HEADS, SUBLANES, tlp), F32),
        ],
        compiler_params=_cparams(2),
        name="mixer",
    )(*args)


def _proj_ln_body(a_ref, w_ref, r_ref, g_ref, b_ref, o_ref):
    y = ALPHA * r_ref[...] + _dot(a_ref[...], w_ref[...])
    o_ref[...] = _layer_norm(y, g_ref[...], b_ref[...])


def _proj_ln(a, w, resid, g, b, *, tm):
    T = a.shape[0]
    const = lambda i: (0, 0)
    return pl.pallas_call(
        _proj_ln_body,
        grid=(T // tm,),
        in_specs=[
            pl.BlockSpec((tm, D_MODEL), lambda i: (i, 0)),
            pl.BlockSpec((D_MODEL, D_MODEL), const),
            pl.BlockSpec((tm, D_MODEL), lambda i: (i, 0)),
            pl.BlockSpec((1, D_MODEL), const),
            pl.BlockSpec((1, D_MODEL), const),
        ],
        out_specs=pl.BlockSpec((tm, D_MODEL), lambda i: (i, 0)),
        out_shape=jax.ShapeDtypeStruct((T, D_MODEL), F32),
        compiler_params=_cparams(1),
        name="proj_ln",
    )(a, w, resid, g, b)


def _attn_body(q_ref, k_ref, v_ref, o_ref):
    scale = CA_HD ** -0.5
    for hh in range(CA_HEADS):
        cs = slice(hh * CA_HD, (hh + 1) * CA_HD)
        s = _dot_nt(q_ref[:, cs].astype(BF16), k_ref[0, :, cs].astype(BF16)) * scale
        m = jnp.max(s, -1, keepdims=True)
        p = jnp.exp(s - m)
        denom = jnp.sum(p, -1, keepdims=True)
        o = _dot(p.astype(BF16), v_ref[0, :, cs].astype(BF16)) / denom
        o_ref[:, cs] = o.astype(o_ref.dtype)


def _attention(q, mem_k, mem_v, B, L):
    tq = min(512, L)
    nq = L // tq
    M = mem_k.shape[1]
    return pl.pallas_call(
        _attn_body,
        grid=(B, nq),
        in_specs=[
            pl.BlockSpec((tq, D_MODEL), lambda b, i: (b * nq + i, 0)),
            pl.BlockSpec((1, M, D_MODEL), lambda b, i: (b, 0, 0)),
            pl.BlockSpec((1, M, D_MODEL), lambda b, i: (b, 0, 0)),
        ],
        out_specs=pl.BlockSpec((tq, D_MODEL), lambda b, i: (b * nq + i, 0)),
        out_shape=jax.ShapeDtypeStruct((B * L, D_MODEL), BF16),
        compiler_params=_cparams(2),
        name="attention",
    )(q, mem_k, mem_v)


def _router_body(x_ref, whi_ref, wlo_ref, idx_ref, w_ref):
    x = x_ref[...]
    x_hi = x.astype(BF16)
    x_lo = (x - x_hi.astype(F32)).astype(BF16)
    logits = _dot(x_hi, whi_ref[...]) + (_dot(x_hi, wlo_ref[...]) + _dot(x_lo, whi_ref[...]))
    tm = x.shape[0]
    lane = lax.broadcasted_iota(jnp.int32, (tm, LANES), 1)
    neg = -jnp.inf
    g_log = jnp.where(lane < MOE_GROUPS, logits, neg)
    g_max = jnp.max(g_log, -1, keepdims=True)
    g_sel = jnp.min(jnp.where(g_log == g_max, lane, LANES), -1, keepdims=True)
    g_w = 1.0 / jnp.sum(jnp.exp(g_log - g_max), -1, keepdims=True)
    lo = MOE_GROUPS + g_sel * MOE_EPG
    e_log = jnp.where((lane >= lo) & (lane < lo + MOE_EPG), logits, neg)
    v1 = jnp.max(e_log, -1, keepdims=True)
    i1 = jnp.min(jnp.where(e_log == v1, lane, LANES), -1, keepdims=True)
    e_log2 = jnp.where(lane == i1, neg, e_log)
    v2 = jnp.max(e_log2, -1, keepdims=True)
    i2 = jnp.min(jnp.where(e_log2 == v2, lane, LANES), -1, keepdims=True)
    t = jnp.exp(v2 - v1)
    w1 = g_w / (1.0 + t)
    w2 = g_w * t / (1.0 + t)
    first = lax.broadcasted_iota(jnp.int32, (tm, 2), 1) == 0
    idx_ref[...] = jnp.where(first, i1, i2) - MOE_GROUPS
    w_ref[...] = jnp.where(first, w1, w2)


def _router(x, w_hi, w_lo, *, tm):
    T = x.shape[0]
    const = lambda i: (0, 0)
    return pl.pallas_call(
        _router_body,
        grid=(T // tm,),
        in_specs=[
            pl.BlockSpec((tm, D_MODEL), lambda i: (i, 0)),
            pl.BlockSpec((D_MODEL, LANES), const),
            pl.BlockSpec((D_MODEL, LANES), const),
        ],
        out_specs=[pl.BlockSpec((tm, 2), lambda i: (i, 0)), pl.BlockSpec((tm, 2), lambda i: (i, 0))],
        out_shape=[jax.ShapeDtypeStruct((T, 2), jnp.int32), jax.ShapeDtypeStruct((T, 2), F32)],
        compiler_params=_cparams(1),
        name="router",
    )(x, w_hi, w_lo)


def _plan_body(idx_ref, dest_ref, pend_ref, carry, pstart):
    ph = pl.program_id(0)
    i = pl.program_id(1)
    tm = idx_ref.shape[0]
    lane = lax.broadcasted_iota(jnp.int32, (tm, LANES), 1)
    oh0 = (lane == idx_ref[:, 0:1]).astype(F32)
    oh1 = (lane == idx_ref[:, 1:2]).astype(F32)
    both = oh0 + oh1

    @pl.when((ph == 0) & (i == 0))
    def _():
        carry[...] = jnp.zeros(carry.shape, F32)

    @pl.when(ph == 0)
    def _():
        carry[...] += jnp.sum(both, 0, keepdims=True)

    @pl.when((ph == 1) & (i == 0))
    def _():
        padded = jnp.floor((carry[...] + (MOE_BLOCK - 1)) * (1.0 / MOE_BLOCK)) * MOE_BLOCK
        a = lax.broadcasted_iota(jnp.int32, (LANES, LANES), 0)
        b = lax.broadcasted_iota(jnp.int32, (LANES, LANES), 1)
        upper = (a < b).astype(F32)
        ps = jnp.dot(jnp.broadcast_to(padded, (SUBLANES, LANES)), upper, preferred_element_type=F32,
                     precision=lax.Precision.HIGHEST)
        pstart[...] = ps[0:1]
        carry[...] = jnp.zeros(carry.shape, F32)

    @pl.when(ph == 1)
    def _():
        r = lax.broadcasted_iota(jnp.int32, (tm, tm), 0)
        c = lax.broadcasted_iota(jnp.int32, (tm, tm), 1)
        before = (r > c).astype(BF16)
        base = _dot(before, both.astype(BF16)) + carry[...] + pstart[...]
        d0 = jnp.sum(oh0 * base, -1, keepdims=True)
        d1 = jnp.sum(oh1 * base, -1, keepdims=True)
        first = lax.broadcasted_iota(jnp.int32, (tm, 2), 1) == 0
        dest_ref[...] = jnp.where(first, d0, d1).astype(jnp.int32)
        carry[...] += jnp.sum(both, 0, keepdims=True)
        padded_tot = jnp.floor((carry[...] + (MOE_BLOCK - 1)) * (1.0 / MOE_BLOCK)) * MOE_BLOCK
        pend_ref[...] = jnp.broadcast_to(pstart[...] + padded_tot, pend_ref.shape)


def _plan(idx, *, tm):
    T = idx.shape[0]
    return pl.pallas_call(
        _plan_body,
        grid=(2, T // tm),
        in_specs=[pl.BlockSpec((tm, 2), lambda p, i: (i, 0))],
        out_specs=[pl.BlockSpec((tm, 2), lambda p, i: (i * p, 0)), pl.BlockSpec((SUBLANES, LANES), lambda p, i: (0, 0))],
        out_shape=[jax.ShapeDtypeStruct((T, 2), jnp.int32), jax.ShapeDtypeStruct((SUBLANES, LANES), F32)],
        scratch_shapes=[pltpu.VMEM((1, LANES), F32), pltpu.VMEM((1, LANES), F32)],
        compiler_params=_cparams(2),
        name="plan",
    )(idx)


def _row_copy(src_ref, s, dst_ref, d, sem):
    return pltpu.make_async_copy(src_ref.at[pl.ds(s, 1), :], dst_ref.at[pl.ds(d, 1), :], sem)


def _dispatch_body(dest_ref, x_ref, xs_in_ref, xs_ref, sem):
    del xs_in_ref
    tm = x_ref.shape[0]

    def start(t, c):
        _row_copy(x_ref, t, xs_ref, dest_ref[0, 0, 2 * t], sem).start()
        _row_copy(x_ref, t, xs_ref, dest_ref[0, 0, 2 * t + 1], sem).start()
        return c

    lax.fori_loop(0, tm, start, 0)

    def wait(t, c):
        _row_copy(x_ref, 0, xs_ref, 0, sem).wait()
        return c

    lax.fori_loop(0, 2 * tm, wait, 0)


def _dispatch(dest, x, xs, *, tm):
    T = x.shape[0]
    nt = T // tm
    return pl.pallas_call(
        _dispatch_body,
        grid=(nt,),
        in_specs=[
            pl.BlockSpec((1, 1, 2 * tm), lambda i: (i, 0, 0), memory_space=pltpu.SMEM),
            pl.BlockSpec((tm, D_MODEL), lambda i: (i, 0)),
            pl.BlockSpec(memory_space=pl.ANY),
        ],
        out_specs=pl.BlockSpec(memory_space=pl.ANY),
        out_shape=jax.ShapeDtypeStruct(xs.shape, xs.dtype),
        scratch_shapes=[pltpu.SemaphoreType.DMA(())],
        input_output_aliases={2: 0},
        compiler_params=_cparams(1),
        name="dispatch",
    )(dest.reshape(nt, 1, 2 * tm), x, xs)


def _combine_body(dest_ref, x_ref, ew_ref, g_ref, b_ref, rows_ref, o_ref, gbuf, sem):
    tm = x_ref.shape[0]

    def start(t, c):
        _row_copy(rows_ref, dest_ref[0, 0, 2 * t], gbuf.at[0], t, sem).start()
        _row_copy(rows_ref, dest_ref[0, 0, 2 * t + 1], gbuf.at[1], t, sem).start()
        return c

    lax.fori_loop(0, tm, start, 0)

    def wait(t, c):
        _row_copy(rows_ref, 0, gbuf.at[0], 0, sem).wait()
        return c

    lax.fori_loop(0, 2 * tm, wait, 0)
    y = ew_ref[:, 0:1] * gbuf[0] + ew_ref[:, 1:2] * gbuf[1]
    o_ref[...] = _layer_norm(ALPHA * x_ref[...] + y, g_ref[...], b_ref[...])


def _combine(dest, x, ew, g, b, rows, *, tm):
    T = x.shape[0]
    nt = T // tm
    const = lambda i: (0, 0)
    return pl.pallas_call(
        _combine_body,
        grid=(nt,),
        in_specs=[
            pl.BlockSpec((1, 1, 2 * tm), lambda i: (i, 0, 0), memory_space=pltpu.SMEM),
            pl.BlockSpec((tm, D_MODEL), lambda i: (i, 0)),
            pl.BlockSpec((tm, 2), lambda i: (i, 0)),
            pl.BlockSpec((1, D_MODEL), const),
            pl.BlockSpec((1, D_MODEL), const),
            pl.BlockSpec(memory_space=pl.ANY),
        ],
        out_specs=pl.BlockSpec((tm, D_MODEL), lambda i: (i, 0)),
        out_shape=jax.ShapeDtypeStruct((T, D_MODEL), F32),
        scratch_shapes=[pltpu.VMEM((2, tm, D_MODEL), F32), pltpu.SemaphoreType.DMA(())],
        compiler_params=_cparams(1),
        name="combine",
    )(dest.reshape(nt, 1, 2 * tm), x, ew, g, b, rows)


def _experts_body(be_ref, nu_ref, xs_ref, wg_ref, wu_ref, wd_ref, o_ref):
    del be_ref
    i = pl.program_id(0)

    @pl.when(i < nu_ref[0])
    def _():
        xb = xs_ref[...].astype(BF16)
        gate = _dot(xb, wg_ref[0].astype(BF16))
        up = _dot(xb, wu_ref[0].astype(BF16))
        hid = gate * jax.nn.sigmoid(gate) * up
        o_ref[...] = _dot(hid.astype(BF16), wd_ref[0].astype(BF16))

    @pl.when(i >= nu_ref[0])
    def _():
        o_ref[...] = jnp.zeros(o_ref.shape, F32)


def _experts(block_expert, n_used, xs, w_gate, w_up, w_down):
    R = xs.shape[0]
    nb = R // MOE_BLOCK
    grid_spec = pltpu.PrefetchScalarGridSpec(
        num_scalar_prefetch=2,
        grid=(nb,),
        in_specs=[
            pl.BlockSpec((MOE_BLOCK, D_MODEL), lambda i, be, nu: (i, 0)),
            pl.BlockSpec((1, D_MODEL, MOE_FF), lambda i, be, nu: (be[i], 0, 0)),
            pl.BlockSpec((1, D_MODEL, MOE_FF), lambda i, be, nu: (be[i], 0, 0)),
            pl.BlockSpec((1, MOE_FF, D_MODEL), lambda i, be, nu: (be[i], 0, 0)),
        ],
        out_specs=pl.BlockSpec((MOE_BLOCK, D_MODEL), lambda i, be, nu: (i, 0)),
    )
    return pl.pallas_call(
        _experts_body,
        grid_spec=grid_spec,
        out_shape=jax.ShapeDtypeStruct((R, D_MODEL), F32),
        compiler_params=_cparams(1),
        name="experts",
    )(block_expert, n_used, xs, w_gate, w_up, w_down)


def _row_tile(T, pref):
    t = pref
    while T % t:
        t //= 2
    return t


def _group_to_x2(x, mem_k, mem_v, conv_prev, pool_prev, s0, pos0, W):
    B, L, _ = x.shape
    T = B * L
    xf = x.reshape(T, D_MODEL)
    tm = _row_tile(T, 512)
    proj = _mm(xf, W["w_in"], tm=tm, tn=N_PROJ // 3)
    merged, s_new = _mixer(proj, B, L, conv_prev, pool_prev, s0, W["w_conv"], W["a_log"], W["dt_bias"], W["w_onorm"],
                           W["w_pool"], W["pool_scale"], pos0=pos0)
    x1 = _proj_ln(merged, W["w_out"], xf, W["ln1_g"], W["ln1_b"], tm=tm)
    q = _mm(x1, W["w_cq"], tm=tm, tn=D_MODEL)
    att = _attention(q, mem_k.reshape(B, -1, D_MODEL), mem_v.reshape(B, -1, D_MODEL), B, L)
    x2 = _proj_ln(att, W["w_co"], x1, W["ln2_g"], W["ln2_b"], tm=tm)
    proj3 = proj.reshape(B, L, N_PROJ)
    qkv_raw = proj3[:, max(L - (CONV_W - 1), 0):, 0:QKV_DIM]
    p_raw = proj3[:, max(L - (POOL_MAX - 1), 0):, 4 * D_MODEL:5 * D_MODEL]
    if conv_prev is None:
        conv_prev = jnp.zeros((B, CONV_W - 1, QKV_DIM), F32)
        pool_prev = jnp.zeros((B, POOL_MAX - 1, D_MODEL), F32)
    conv_new = jnp.concatenate([conv_prev, qkv_raw], axis=1)[:, -(CONV_W - 1):]
    pool_new = jnp.concatenate([pool_prev, p_raw], axis=1)[:, -(POOL_MAX - 1):]
    return x2, s_new, conv_new, pool_new


def kernel(x_prompt, x_sample, cache_mem_k, cache_mem_v, state_delta, state_conv, state_pool, mem_prompt, w_in, w_conv, a_log, dt_bias, w_onorm, w_pool, pool_scale, w_out, ln1_g, ln1_b, w_cq, w_ck, w_cv, w_co, ln2_g, ln2_b, w_router_group, w_router_expert, w_gate, w_up, w_down, ln3_g, ln3_b):
    Bp, Lp, _ = x_prompt.shape
    Bs, Ls, _ = x_sample.shape
    Tp, Ts = Bp * Lp, Bs * Ls
    lyr = 0

    def pad_lanes(v):
        return jnp.pad(v.astype(F32), (0, LANES - v.shape[0])).reshape(1, LANES)

    o1, o2, o3, o4 = QKV_DIM, QKV_DIM + D_MODEL, QKV_DIM + D_MODEL + 2 * DN_HEADS, QKV_DIM + 2 * D_MODEL + 2 * DN_HEADS
    wi = w_in[lyr]
    w_in_r = jnp.concatenate(
        [wi[:, :o2], wi[:, o3:], wi[:, o2:o3], jnp.zeros((D_MODEL, LANES - 2 * DN_HEADS), F32)], axis=1).astype(BF16)
    del o1, o4
    W = {
        "w_in": w_in_r,
        "w_conv": w_conv[lyr],
        "a_log": pad_lanes(a_log[lyr]),
        "dt_bias": pad_lanes(dt_bias[lyr]),
        "w_onorm": w_onorm[lyr].reshape(1, DN_HD),
        "w_pool": w_pool[lyr].astype(BF16),
        "pool_scale": pool_scale[lyr].reshape(1, D_MODEL),
        "w_out": w_out[lyr].astype(BF16),
        "ln1_g": ln1_g[lyr].reshape(1, D_MODEL), "ln1_b": ln1_b[lyr].reshape(1, D_MODEL),
        "w_cq": w_cq[lyr].astype(BF16),
        "w_co": w_co[lyr].astype(BF16),
        "ln2_g": ln2_g[lyr].reshape(1, D_MODEL), "ln2_b": ln2_b[lyr].reshape(1, D_MODEL),
    }

    M = mem_prompt.shape[1]
    memf = mem_prompt.reshape(Bp * M, D_MODEL)
    tmm = _row_tile(Bp * M, 512)
    mk = _mm(memf, w_ck[lyr].astype(BF16), tm=tmm, tn=D_MODEL).reshape(Bp, M, CA_HEADS, CA_HD)
    mv = _mm(memf, w_cv[lyr].astype(BF16), tm=tmm, tn=D_MODEL).reshape(Bp, M, CA_HEADS, CA_HD)
    x2_p, d_p, c_p, p_p = _group_to_x2(x_prompt, mk, mv, None, None, None, 0, W)
    x2_s, d_s, c_s, p_s = _group_to_x2(x_sample, cache_mem_k[lyr], cache_mem_v[lyr], state_conv[lyr],
                                       state_pool[lyr], state_delta[lyr], PAST_LEN, W)

    w_r = jnp.concatenate([w_router_group[lyr], w_router_expert[lyr],
                           jnp.zeros((D_MODEL, LANES - MOE_GROUPS - MOE_EXPERTS), F32)], axis=1)
    w_r_hi = w_r.astype(BF16)
    w_r_lo = (w_r - w_r_hi.astype(F32)).astype(BF16)
    tp, ts = _row_tile(Tp, 256), _row_tile(Ts, 256)
    idx_p, ew_p = _router(x2_p, w_r_hi, w_r_lo, tm=tp)
    idx_s, ew_s = _router(x2_s, w_r_hi, w_r_lo, tm=ts)
    T = Tp + Ts
    dest, pend = _plan(jnp.concatenate([idx_p, idx_s], axis=0), tm=_row_tile(T, 256))
    n_blocks = (2 * T + MOE_EXPERTS * (MOE_BLOCK - 1) + MOE_BLOCK - 1) // MOE_BLOCK
    pend_e = pend[0, :MOE_EXPERTS].astype(jnp.int32)
    block_start = jnp.arange(n_blocks, dtype=jnp.int32) * MOE_BLOCK
    block_expert = jnp.minimum(jnp.sum(block_start[:, None] >= pend_e[None, :], axis=1), MOE_EXPERTS - 1).astype(jnp.int32)
    n_used = (pend_e[MOE_EXPERTS - 1:] // MOE_BLOCK).astype(jnp.int32)
    xs = jnp.zeros((n_blocks * MOE_BLOCK, D_MODEL), F32)
    xs = _dispatch(dest[:Tp], x2_p, xs, tm=tp)
    xs = _dispatch(dest[Tp:], x2_s, xs, tm=ts)
    rows = _experts(block_expert, n_used, xs, w_gate[lyr], w_up[lyr], w_down[lyr])
    g3, b3 = ln3_g[lyr].reshape(1, D_MODEL), ln3_b[lyr].reshape(1, D_MODEL)
    y_p = _combine(dest[:Tp], x2_p, ew_p, g3, b3, rows, tm=tp).reshape(Bp, Lp, D_MODEL)
    y_s = _combine(dest[Tp:], x2_s, ew_s, g3, b3, rows, tm=ts).reshape(Bs, Ls, D_MODEL)

    return (y_p, y_s, d_p[None], c_p[None], p_p[None], mk[None], mv[None], d_s[None], c_s[None], p_s[None])
```

```python
import functools

import jax
import jax.numpy as jnp
from jax import lax
from jax.experimental import pallas as pl
from jax.experimental.pallas import tpu as pltpu

F32 = jnp.float32
BF16 = jnp.bfloat16

D_MODEL = 1024
DN_HEADS = 8
DN_HD = 128
QKV_DIM = 3 * DN_HEADS * DN_HD
CONV_W = 4
DN_CHUNK = 64
POOL_WINDOWS = (2, 4, 8, 16)
POOL_GD = D_MODEL // len(POOL_WINDOWS)
POOL_MAX = 16
CA_HEADS = 4
CA_HD = D_MODEL // CA_HEADS
MOE_GROUPS = 8
MOE_EPG = 8
MOE_EXPERTS = MOE_GROUPS * MOE_EPG
MOE_FF = D_MODEL // 4
MOE_BLOCK = 128
PAST_LEN = 16384
LN_EPS = 1e-5
NORM_EPS = 1e-6
ALPHA = 2.0 ** 0.25

LANES = 128
SUBLANES = 8
CONV_HALO = SUBLANES
POOL_HALO = POOL_MAX
MIN_CHUNK = 16
N_PROJ = 7 * D_MODEL + LANES
VMEM_LIMIT = 48 * 1024 * 1024


def _cparams(n_axes, vmem=VMEM_LIMIT):
    return pltpu.CompilerParams(dimension_semantics=("arbitrary",) * n_axes, vmem_limit_bytes=vmem)


def _layer_norm(x, g, b):
    mu = jnp.mean(x, -1, keepdims=True)
    xc = x - mu
    var = jnp.mean(xc * xc, -1, keepdims=True)
    return xc * lax.rsqrt(var + LN_EPS) * g + b


def _softplus(x):
    return jnp.maximum(x, 0.0) + jnp.log1p(jnp.exp(-jnp.abs(x)))


def _dot(a, b):
    return jnp.dot(a, b, preferred_element_type=F32)


def _dot_nt(a, b):
    return lax.dot_general(a, b, (((1,), (1,)), ((), ())), preferred_element_type=F32)


def _dot_tn(a, b):
    return lax.dot_general(a, b, (((0,), (0,)), ((), ())), preferred_element_type=F32)


def _mm_body(x_ref, w_ref, o_ref):
    o_ref[...] = _dot(x_ref[...].astype(BF16), w_ref[...]).astype(o_ref.dtype)


def _mm(x, w, *, tm, tn, out_dtype=F32):
    T, K = x.shape
    N = w.shape[1]
    return pl.pallas_call(
        _mm_body,
        grid=(N // tn, T // tm),
        in_specs=[pl.BlockSpec((tm, K), lambda j, i: (i, 0)), pl.BlockSpec((K, tn), lambda j, i: (0, j))],
        out_specs=pl.BlockSpec((tm, tn), lambda j, i: (i, j)),
        out_shape=jax.ShapeDtypeStruct((T, N), out_dtype),
        compiler_params=_cparams(2),
        name="mm",
    )(x, w)


def _mixer_body(*refs, tl, chunk, pos0, has_state):
    if has_state:
        (qkv_ref, z_ref, p_ref, ga_ref, gb_ref, ab_ref, convp_ref, poolp_ref, s0_ref, wconv_ref, alog_ref, dtb_ref,
         wonorm_ref, wpool_ref, pscale_ref, merged_ref, sout_ref, qkvbuf, pbuf, obuf, gcb, betab, gt_ref) = refs
    else:
        (qkv_ref, z_ref, p_ref, ga_ref, gb_ref, ab_ref, wconv_ref, alog_ref, dtb_ref,
         wonorm_ref, wpool_ref, pscale_ref, merged_ref, sout_ref, qkvbuf, pbuf, obuf, gcb, betab, gt_ref) = refs
    l = pl.program_id(1)
    tlp = max(tl, MIN_CHUNK)
    C = max(chunk, MIN_CHUNK)
    n_chunks = tlp // C

    @pl.when(l == 0)
    def _init():
        qkvbuf[0:CONV_HALO, :] = jnp.zeros((CONV_HALO, QKV_DIM), F32)
        pbuf[0:POOL_HALO, :] = jnp.zeros((POOL_HALO, D_MODEL), F32)
        if has_state:
            qkvbuf[CONV_HALO - (CONV_W - 1):CONV_HALO, :] = convp_ref[0]
            pbuf[POOL_HALO - (POOL_MAX - 1):POOL_HALO, :] = poolp_ref[0]
            sout_ref[...] = s0_ref[...]
        else:
            sout_ref[...] = jnp.zeros(sout_ref.shape, F32)

    qkvbuf[CONV_HALO:CONV_HALO + tl, :] = qkv_ref[...]
    pbuf[POOL_HALO:POOL_HALO + tl, :] = p_ref[...]

    def pad_rows(x):
        if tlp == tl:
            return x
        return jnp.concatenate([x, jnp.zeros((tlp - tl, x.shape[1]), F32)], axis=0)

    ab = ab_ref[...]
    g_all = pad_rows(-jnp.exp(alog_ref[...]) * _softplus(ab + dtb_ref[...]))
    beta_all = pad_rows(jax.nn.sigmoid(ab))
    ri = lax.broadcasted_iota(jnp.int32, (tlp, tlp), 0)
    ci = lax.broadcasted_iota(jnp.int32, (tlp, tlp), 1)
    cum_mat = ((ri >= ci) & ((ri // C) == (ci // C))).astype(F32)
    gcum = jnp.dot(cum_mat, g_all, preferred_element_type=F32, precision=lax.Precision.HIGHEST)
    gcum_t = gcum.T
    for h in range(DN_HEADS):
        gt_ref[h] = jnp.broadcast_to(gcum_t[h:h + 1, :], (SUBLANES, tlp))
        gcb[h] = jnp.broadcast_to(gcum[:, h:h + 1], (tlp, LANES))
        betab[h] = jnp.broadcast_to(beta_all[:, DN_HEADS + h:DN_HEADS + h + 1], (tlp, LANES))

    ii = lax.broadcasted_iota(jnp.int32, (C, C), 0)
    jj = lax.broadcasted_iota(jnp.int32, (C, C), 1)
    causal = ii >= jj
    strict = ii > jj
    eye = (ii == jj).astype(F32)
    n_sq = max(C.bit_length() - 2, 0)

    def conv_slab(col0):
        cols = slice(col0, col0 + DN_HD)
        acc = qkvbuf[CONV_HALO:CONV_HALO + tl, cols] * wconv_ref[CONV_W - 1:CONV_W, cols]
        for i in range(CONV_W - 1):
            r0 = CONV_HALO - (CONV_W - 1) + i
            acc = acc + qkvbuf[r0:r0 + tl, cols] * wconv_ref[i:i + 1, cols]
        return pad_rows(acc * jax.nn.sigmoid(acc))

    heads = range(DN_HEADS)
    chunks = range(n_chunks)
    probs = [(h, c) for h in heads for c in chunks]
    rows = [slice(c * C, (c + 1) * C) for c in chunks]
    q, k, v, gcs, bs, egs, kbs, grows = [], [], [], [], [], [], [], []
    for h in heads:
        qh = conv_slab(h * DN_HD)
        kh = conv_slab((DN_HEADS + h) * DN_HD)
        v.append(conv_slab((2 * DN_HEADS + h) * DN_HD))
        q.append(qh * lax.rsqrt(jnp.sum(qh * qh, -1, keepdims=True) + NORM_EPS) * (DN_HD ** -0.5))
        kh = kh * lax.rsqrt(jnp.sum(kh * kh, -1, keepdims=True) + NORM_EPS)
        k.append(kh)
        gcs.append(gcb[h])
        bs.append(betab[h])
        egs.append(jnp.exp(gcs[h]))
        kbs.append(kh * bs[h])
        grows.append(gt_ref[h][0:1, :])

    aq = {(h, c): _dot_nt(jnp.concatenate([kbs[h][rows[c]], q[h][rows[c]]], axis=0).astype(BF16),
                          k[h][rows[c]].astype(BF16)) for h, c in probs}
    decay = {(h, c): jnp.exp(jnp.where(causal, gcs[h][rows[c], 0:C] - grows[h][:, c * C:(c + 1) * C], -jnp.inf))
             for h, c in probs}
    A = {p: jnp.where(strict, aq[p][0:C] * decay[p], 0.0) for p in probs}
    qk = {p: (aq[p][C:2 * C] * decay[p]).astype(BF16) for p in probs}
    P = {p: eye - A[p] for p in probs}
    Q = {p: _dot(A[p].astype(BF16), A[p].astype(BF16)) for p in probs}
    for _ in range(n_sq - 1):
        pq = {p: _dot(jnp.concatenate([P[p], Q[p]], axis=0).astype(BF16), Q[p].astype(BF16)) for p in probs}
        P = {p: P[p] + pq[p][0:C] for p in probs}
        Q = {p: pq[p][C:2 * C] for p in probs}
    P = {p: P[p] + _dot(P[p].astype(BF16), Q[p].astype(BF16)) for p in probs}
    sol = {(h, c): _dot(P[(h, c)].astype(BF16),
                        jnp.concatenate([v[h][rows[c]] * bs[h][rows[c]], kbs[h][rows[c]] * egs[h][rows[c]]],
                                        axis=1).astype(BF16)) for h, c in probs}

    S = [sout_ref[0, h] for h in heads]
    outs = {}
    for c in chunks:
        r = rows[c]
        ws_qs = [_dot(jnp.concatenate([sol[(h, c)][:, DN_HD:2 * DN_HD], q[h][r] * egs[h][r]], axis=0).astype(BF16),
                      S[h].astype(BF16)) for h in heads]
        v_new = [(sol[(h, c)][:, 0:DN_HD] - ws_qs[h][0:C]).astype(BF16) for h in heads]
        glast = [gcs[h][r][C - 1:C, :] for h in heads]
        S = [S[h] * jnp.exp(glast[h]) + _dot_tn((k[h][r] * jnp.exp(glast[h] - gcs[h][r])).astype(BF16), v_new[h])
             for h in heads]
        for h in heads:
            outs[(h, c)] = ws_qs[h][C:2 * C] + _dot(qk[(h, c)], v_new[h])
    for h in heads:
        sout_ref[0, h] = S[h]
        o = outs[(h, 0)] if n_chunks == 1 else jnp.concatenate([outs[(h, c)] for c in chunks], axis=0)
        o = o[0:tl]
        o = o * lax.rsqrt(jnp.mean(o * o, -1, keepdims=True) + NORM_EPS) * wonorm_ref[...]
        zh = z_ref[:, h * DN_HD:(h + 1) * DN_HD]
        obuf[:, h * DN_HD:(h + 1) * DN_HD] = o * (zh * jax.nn.sigmoid(zh))

    if pos0 == 0:
        pos = l * tl + lax.broadcasted_iota(jnp.int32, (tl, 1), 0)
    for gi, win in enumerate(POOL_WINDOWS):
        cs = slice(gi * POOL_GD, (gi + 1) * POOL_GD)
        s = pbuf[POOL_HALO:POOL_HALO + tl, cs]
        for j in range(1, win):
            s = s + pbuf[POOL_HALO - j:POOL_HALO - j + tl, cs]
        if pos0 == 0:
            cnt = jnp.minimum(win, pos + 1).astype(F32)
        else:
            cnt = float(min(win, pos0 + 1))
        pooled = s / cnt - p_ref[:, cs]
        bb = _dot(pooled.astype(BF16), wpool_ref[gi]) * pscale_ref[:, cs]
        merged_ref[:, cs] = (jax.nn.sigmoid(ga_ref[:, cs]) * obuf[:, cs]
                             + jax.nn.sigmoid(gb_ref[:, cs]) * bb).astype(merged_ref.dtype)

    if tl >= POOL_HALO:
        qkvbuf[0:CONV_HALO, :] = qkvbuf[tl:tl + CONV_HALO, :]
        pbuf[0:POOL_HALO, :] = pbuf[tl:tl + POOL_HALO, :]


def _mixer(proj, B, L, conv_prev, pool_prev, s0, wconv, alog, dtb, wonorm, wpool, pscale, *, pos0):
    tl = min(256, L)
    nL = L // tl
    assert nL == 1 or tl >= POOL_HALO
    chunk = min(DN_CHUNK, tl)
    tlp = max(tl, MIN_CHUNK)
    has_state = s0 is not None
    row = lambda b, l: b * nL + l
    in_specs = [
        pl.BlockSpec((tl, QKV_DIM), lambda b, l: (row(b, l), 0)),
        pl.BlockSpec((tl, D_MODEL), lambda b, l: (row(b, l), 3)),
        pl.BlockSpec((tl, D_MODEL), lambda b, l: (row(b, l), 4)),
        pl.BlockSpec((tl, D_MODEL), lambda b, l: (row(b, l), 5)),
        pl.BlockSpec((tl, D_MODEL), lambda b, l: (row(b, l), 6)),
        pl.BlockSpec((tl, LANES), lambda b, l: (row(b, l), 7 * D_MODEL // LANES)),
    ]
    args = [proj] * 6
    if has_state:
        in_specs += [
            pl.BlockSpec((1, CONV_W - 1, QKV_DIM), lambda b, l: (b, 0, 0)),
            pl.BlockSpec((1, POOL_MAX - 1, D_MODEL), lambda b, l: (b, 0, 0)),
            pl.BlockSpec((1, DN_HEADS, DN_HD, DN_HD), lambda b, l: (b, 0, 0, 0)),
        ]
        args += [conv_prev, pool_prev, s0]
    const2 = lambda b, l: (0, 0)
    in_specs += [
        pl.BlockSpec((CONV_W, QKV_DIM), const2),
        pl.BlockSpec((1, LANES), const2),
        pl.BlockSpec((1, LANES), const2),
        pl.BlockSpec((1, DN_HD), const2),
        pl.BlockSpec((len(POOL_WINDOWS), POOL_GD, POOL_GD), lambda b, l: (0, 0, 0)),
        pl.BlockSpec((1, D_MODEL), const2),
    ]
    args += [wconv, alog, dtb, wonorm, wpool, pscale]
    return pl.pallas_call(
        functools.partial(_mixer_body, tl=tl, chunk=chunk, pos0=pos0, has_state=has_state),
        grid=(B, nL),
        in_specs=in_specs,
        out_specs=[
            pl.BlockSpec((tl, D_MODEL), lambda b, l: (row(b, l), 0)),
            pl.BlockSpec((1, DN_HEADS, DN_HD, DN_HD), lambda b, l: (b, 0, 0, 0)),
        ],
        out_shape=[
            jax.ShapeDtypeStruct((B * L, D_MODEL), BF16),
            jax.ShapeDtypeStruct((B, DN_HEADS, DN_HD, DN_HD), F32),
        ],
        scratch_shapes=[
            pltpu.VMEM((CONV_HALO + tl, QKV_DIM), F32),
            pltpu.VMEM((POOL_HALO + tl, D_MODEL), F32),
            pltpu.VMEM((tl, D_MODEL), F32),
            pltpu.VMEM((DN_HEADS, tlp, LANES), F32),
            pltpu.VMEM((DN_HEADS, tlp, LANES), F32),
            pltpu.VMEM((DN_HEADS, SUBLANES, tlp), F32),
        ],
        compiler_params=_cparams(2),
        name="mixer",
    )(*args)


def _proj_ln_body(a_ref, w_ref, r_ref, g_ref, b_ref, o_ref):
    y = ALPHA * r_ref[...] + _dot(a_ref[...], w_ref[...])
    o_ref[...] = _layer_norm(y, g_ref[...], b_ref[...])


def _proj_ln(a, w, resid, g, b, *, tm):
    T = a.shape[0]
    const = lambda i: (0, 0)
    return pl.pallas_call(
        _proj_ln_body,
        grid=(T // tm,),
        in_specs=[
            pl.BlockSpec((tm, D_MODEL), lambda i: (i, 0)),
            pl.BlockSpec((D_MODEL, D_MODEL), const),
            pl.BlockSpec((tm, D_MODEL), lambda i: (i, 0)),
            pl.BlockSpec((1, D_MODEL), const),
            pl.BlockSpec((1, D_MODEL), const),
        ],
        out_specs=pl.BlockSpec((tm, D_MODEL), lambda i: (i, 0)),
        out_shape=jax.ShapeDtypeStruct((T, D_MODEL), F32),
        compiler_params=_cparams(1),
        name="proj_ln",
    )(a, w, resid, g, b)


def _attn_body(q_ref, k_ref, v_ref, o_ref):
    scale = CA_HD ** -0.5
    for hh in range(CA_HEADS):
        cs = slice(hh * CA_HD, (hh + 1) * CA_HD)
        s = _dot_nt(q_ref[:, cs].astype(BF16), k_ref[0, :, cs].astype(BF16)) * scale
        m = jnp.max(s, -1, keepdims=True)
        p = jnp.exp(s - m)
        denom = jnp.sum(p, -1, keepdims=True)
        o = _dot(p.astype(BF16), v_ref[0, :, cs].astype(BF16)) / denom
        o_ref[:, cs] = o.astype(o_ref.dtype)


def _attention(q, mem_k, mem_v, B, L):
    tq = min(512, L)
    nq = L // tq
    M = mem_k.shape[1]
    return pl.pallas_call(
        _attn_body,
        grid=(B, nq),
        in_specs=[
            pl.BlockSpec((tq, D_MODEL), lambda b, i: (b * nq + i, 0)),
            pl.BlockSpec((1, M, D_MODEL), lambda b, i: (b, 0, 0)),
            pl.BlockSpec((1, M, D_MODEL), lambda b, i: (b, 0, 0)),
        ],
        out_specs=pl.BlockSpec((tq, D_MODEL), lambda b, i: (b * nq + i, 0)),
        out_shape=jax.ShapeDtypeStruct((B * L, D_MODEL), BF16),
        compiler_params=_cparams(2),
        name="attention",
    )(q, mem_k, mem_v)


def _router_body(x_ref, whi_ref, wlo_ref, idx_ref, w_ref):
    x = x_ref[...]
    x_hi = x.astype(BF16)
    x_lo = (x - x_hi.astype(F32)).astype(BF16)
    logits = _dot(x_hi, whi_ref[...]) + (_dot(x_hi, wlo_ref[...]) + _dot(x_lo, whi_ref[...]))
    tm = x.shape[0]
    lane = lax.broadcasted_iota(jnp.int32, (tm, LANES), 1)
    neg = -jnp.inf
    g_log = jnp.where(lane < MOE_GROUPS, logits, neg)
    g_max = jnp.max(g_log, -1, keepdims=True)
    g_sel = jnp.min(jnp.where(g_log == g_max, lane, LANES), -1, keepdims=True)
    g_w = 1.0 / jnp.sum(jnp.exp(g_log - g_max), -1, keepdims=True)
    lo = MOE_GROUPS + g_sel * MOE_EPG
    e_log = jnp.where((lane >= lo) & (lane < lo + MOE_EPG), logits, neg)
    v1 = jnp.max(e_log, -1, keepdims=True)
    i1 = jnp.min(jnp.where(e_log == v1, lane, LANES), -1, keepdims=True)
    e_log2 = jnp.where(lane == i1, neg, e_log)
    v2 = jnp.max(e_log2, -1, keepdims=True)
    i2 = jnp.min(jnp.where(e_log2 == v2, lane, LANES), -1, keepdims=True)
    t = jnp.exp(v2 - v1)
    w1 = g_w / (1.0 + t)
    w2 = g_w * t / (1.0 + t)
    first = lax.broadcasted_iota(jnp.int32, (tm, 2), 1) == 0
    idx_ref[...] = jnp.where(first, i1, i2) - MOE_GROUPS
    w_ref[...] = jnp.where(first, w1, w2)


def _router(x, w_hi, w_lo, *, tm):
    T = x.shape[0]
    const = lambda i: (0, 0)
    return pl.pallas_call(
        _router_body,
        grid=(T // tm,),
        in_specs=[
            pl.BlockSpec((tm, D_MODEL), lambda i: (i, 0)),
            pl.BlockSpec((D_MODEL, LANES), const),
            pl.BlockSpec((D_MODEL, LANES), const),
        ],
        out_specs=[pl.BlockSpec((tm, 2), lambda i: (i, 0)), pl.BlockSpec((tm, 2), lambda i: (i, 0))],
        out_shape=[jax.ShapeDtypeStruct((T, 2), jnp.int32), jax.ShapeDtypeStruct((T, 2), F32)],
        compiler_params=_cparams(1),
        name="router",
    )(x, w_hi, w_lo)


def _plan_body(idx_ref, dest_ref, pend_ref, carry, pstart):
    ph = pl.program_id(0)
    i = pl.program_id(1)
    tm = idx_ref.shape[0]
    lane = lax.broadcasted_iota(jnp.int32, (tm, LANES), 1)
    oh0 = (lane == idx_ref[:, 0:1]).astype(F32)
    oh1 = (lane == idx_ref[:, 1:2]).astype(F32)
    both = oh0 + oh1

    @pl.when((ph == 0) & (i == 0))
    def _():
        carry[...] = jnp.zeros(carry.shape, F32)

    @pl.when(ph == 0)
    def _():
        carry[...] += jnp.sum(both, 0, keepdims=True)

    @pl.when((ph == 1) & (i == 0))
    def _():
        padded = jnp.floor((carry[...] + (MOE_BLOCK - 1)) * (1.0 / MOE_BLOCK)) * MOE_BLOCK
        a = lax.broadcasted_iota(jnp.int32, (LANES, LANES), 0)
        b = lax.broadcasted_iota(jnp.int32, (LANES, LANES), 1)
        upper = (a < b).astype(F32)
        ps = jnp.dot(jnp.broadcast_to(padded, (SUBLANES, LANES)), upper, preferred_element_type=F32,
                     precision=lax.Precision.HIGHEST)
        pstart[...] = ps[0:1]
        carry[...] = jnp.zeros(carry.shape, F32)

    @pl.when(ph == 1)
    def _():
        r = lax.broadcasted_iota(jnp.int32, (tm, tm), 0)
        c = lax.broadcasted_iota(jnp.int32, (tm, tm), 1)
        before = (r > c).astype(BF16)
        base = _dot(before, both.astype(BF16)) + carry[...] + pstart[...]
        d0 = jnp.sum(oh0 * base, -1, keepdims=True)
        d1 = jnp.sum(oh1 * base, -1, keepdims=True)
        first = lax.broadcasted_iota(jnp.int32, (tm, 2), 1) == 0
        dest_ref[...] = jnp.where(first, d0, d1).astype(jnp.int32)
        carry[...] += jnp.sum(both, 0, keepdims=True)
        padded_tot = jnp.floor((carry[...] + (MOE_BLOCK - 1)) * (1.0 / MOE_BLOCK)) * MOE_BLOCK
        pend_ref[...] = jnp.broadcast_to(pstart[...] + padded_tot, pend_ref.shape)


def _plan(idx, *, tm):
    T = idx.shape[0]
    return pl.pallas_call(
        _plan_body,
        grid=(2, T // tm),
        in_specs=[pl.BlockSpec((tm, 2), lambda p, i: (i, 0))],
        out_specs=[pl.BlockSpec((tm, 2), lambda p, i: (i * p, 0)), pl.BlockSpec((SUBLANES, LANES), lambda p, i: (0, 0))],
        out_shape=[jax.ShapeDtypeStruct((T, 2), jnp.int32), jax.ShapeDtypeStruct((SUBLANES, LANES), F32)],
        scratch_shapes=[pltpu.VMEM((1, LANES), F32), pltpu.VMEM((1, LANES), F32)],
        compiler_params=_cparams(2),
        name="plan",
    )(idx)


def _row_copy(src_ref, s, dst_ref, d, sem):
    return pltpu.make_async_copy(src_ref.at[pl.ds(s, 1), :], dst_ref.at[pl.ds(d, 1), :], sem)


def _dispatch_body(dest_ref, x_ref, xs_in_ref, xs_ref, sem):
    del xs_in_ref
    tm = x_ref.shape[0]

    def start(t, c):
        _row_copy(x_ref, t, xs_ref, dest_ref[0, 0, 2 * t], sem).start()
        _row_copy(x_ref, t, xs_ref, dest_ref[0, 0, 2 * t + 1], sem).start()
        return c

    lax.fori_loop(0, tm, start, 0)

    def wait(t, c):
        _row_copy(x_ref, 0, xs_ref, 0, sem).wait()
        return c

    lax.fori_loop(0, 2 * tm, wait, 0)


def _dispatch(dest, x, xs, *, tm):
    T = x.shape[0]
    nt = T // tm
    return pl.pallas_call(
        _dispatch_body,
        grid=(nt,),
        in_specs=[
            pl.BlockSpec((1, 1, 2 * tm), lambda i: (i, 0, 0), memory_space=pltpu.SMEM),
            pl.BlockSpec((tm, D_MODEL), lambda i: (i, 0)),
            pl.BlockSpec(memory_space=pl.ANY),
        ],
        out_specs=pl.BlockSpec(memory_space=pl.ANY),
        out_shape=jax.ShapeDtypeStruct(xs.shape, xs.dtype),
        scratch_shapes=[pltpu.SemaphoreType.DMA(())],
        input_output_aliases={2: 0},
        compiler_params=_cparams(1),
        name="dispatch",
    )(dest.reshape(nt, 1, 2 * tm), x, xs)


def _combine_body(dest_ref, x_ref, ew_ref, g_ref, b_ref, rows_ref, o_ref, gbuf, sem):
    tm = x_ref.shape[0]

    def start(t, c):
        _row_copy(rows_ref, dest_ref[0, 0, 2 * t], gbuf.at[0], t, sem).start()
        _row_copy(rows_ref, dest_ref[0, 0, 2 * t + 1], gbuf.at[1], t, sem).start()
        return c

    lax.fori_loop(0, tm, start, 0)

    def wait(t, c):
        _row_copy(rows_ref, 0, gbuf.at[0], 0, sem).wait()
        return c

    lax.fori_loop(0, 2 * tm, wait, 0)
    y = ew_ref[:, 0:1] * gbuf[0] + ew_ref[:, 1:2] * gbuf[1]
    o_ref[...] = _layer_norm(ALPHA * x_ref[...] + y, g_ref[...], b_ref[...])


def _combine(dest, x, ew, g, b, rows, *, tm):
    T = x.shape[0]
    nt = T // tm
    const = lambda i: (0, 0)
    return pl.pallas_call(
        _combine_body,
        grid=(nt,),
        in_specs=[
            pl.BlockSpec((1, 1, 2 * tm), lambda i: (i, 0, 0), memory_space=pltpu.SMEM),
            pl.BlockSpec((tm, D_MODEL), lambda i: (i, 0)),
            pl.BlockSpec((tm, 2), lambda i: (i, 0)),
            pl.BlockSpec((1, D_MODEL), const),
            pl.BlockSpec((1, D_MODEL), const),
            pl.BlockSpec(memory_space=pl.ANY),
        ],
        out_specs=pl.BlockSpec((tm, D_MODEL), lambda i: (i, 0)),
        out_shape=jax.ShapeDtypeStruct((T, D_MODEL), F32),
        scratch_shapes=[pltpu.VMEM((2, tm, D_MODEL), F32), pltpu.SemaphoreType.DMA(())],
        compiler_params=_cparams(1),
        name="combine",
    )(dest.reshape(nt, 1, 2 * tm), x, ew, g, b, rows)


def _experts_body(be_ref, nu_ref, xs_ref, wg_ref, wu_ref, wd_ref, o_ref):
    del be_ref
    i = pl.program_id(0)

    @pl.when(i < nu_ref[0])
    def _():
        xb = xs_ref[...].astype(BF16)
        gate = _dot(xb, wg_ref[0].astype(BF16))
        up = _dot(xb, wu_ref[0].astype(BF16))
        hid = gate * jax.nn.sigmoid(gate) * up
        o_ref[...] = _dot(hid.astype(BF16), wd_ref[0].astype(BF16))

    @pl.when(i >= nu_ref[0])
    def _():
        o_ref[...] = jnp.zeros(o_ref.shape, F32)


def _experts(block_expert, n_used, xs, w_gate, w_up, w_down):
    R = xs.shape[0]
    nb = R // MOE_BLOCK
    grid_spec = pltpu.PrefetchScalarGridSpec(
        num_scalar_prefetch=2,
        grid=(nb,),
        in_specs=[
            pl.BlockSpec((MOE_BLOCK, D_MODEL), lambda i, be, nu: (i, 0)),
            pl.BlockSpec((1, D_MODEL, MOE_FF), lambda i, be, nu: (be[i], 0, 0)),
            pl.BlockSpec((1, D_MODEL, MOE_FF), lambda i, be, nu: (be[i], 0, 0)),
            pl.BlockSpec((1, MOE_FF, D_MODEL), lambda i, be, nu: (be[i], 0, 0)),
        ],
        out_specs=pl.BlockSpec((MOE_BLOCK, D_MODEL), lambda i, be, nu: (i, 0)),
    )
    return pl.pallas_call(
        _experts_body,
        grid_spec=grid_spec,
        out_shape=jax.ShapeDtypeStruct((R, D_MODEL), F32),
        compiler_params=_cparams(1),
        name="experts",
    )(block_expert, n_used, xs, w_gate, w_up, w_down)


def _row_tile(T, pref):
    t = pref
    while T % t:
        t //= 2
    return t


def _group_to_x2(x, mem_k, mem_v, conv_prev, pool_prev, s0, pos0, W):
    B, L, _ = x.shape
    T = B * L
    xf = x.reshape(T, D_MODEL)
    tm = _row_tile(T, 512)
    proj = _mm(xf, W["w_in"], tm=tm, tn=N_PROJ // 3)
    merged, s_new = _mixer(proj, B, L, conv_prev, pool_prev, s0, W["w_conv"], W["a_log"], W["dt_bias"], W["w_onorm"],
                           W["w_pool"], W["pool_scale"], pos0=pos0)
    x1 = _proj_ln(merged, W["w_out"], xf, W["ln1_g"], W["ln1_b"], tm=tm)
    q = _mm(x1, W["w_cq"], tm=tm, tn=D_MODEL)
    att = _attention(q, mem_k.reshape(B, -1, D_MODEL), mem_v.reshape(B, -1, D_MODEL), B, L)
    x2 = _proj_ln(att, W["w_co"], x1, W["ln2_g"], W["ln2_b"], tm=tm)
    proj3 = proj.reshape(B, L, N_PROJ)
    qkv_raw = proj3[:, max(L - (CONV_W - 1), 0):, 0:QKV_DIM]
    p_raw = proj3[:, max(L - (POOL_MAX - 1), 0):, 4 * D_MODEL:5 * D_MODEL]
    if conv_prev is None:
        conv_prev = jnp.zeros((B, CONV_W - 1, QKV_DIM), F32)
        pool_prev = jnp.zeros((B, POOL_MAX - 1, D_MODEL), F32)
    conv_new = jnp.concatenate([conv_prev, qkv_raw], axis=1)[:, -(CONV_W - 1):]
    pool_new = jnp.concatenate([pool_prev, p_raw], axis=1)[:, -(POOL_MAX - 1):]
    return x2, s_new, conv_new, pool_new


def kernel(x_prompt, x_sample, cache_mem_k, cache_mem_v, state_delta, state_conv, state_pool, mem_prompt, w_in, w_conv, a_log, dt_bias, w_onorm, w_pool, pool_scale, w_out, ln1_g, ln1_b, w_cq, w_ck, w_cv, w_co, ln2_g, ln2_b, w_router_group, w_router_expert, w_gate, w_up, w_down, ln3_g, ln3_b):
    Bp, Lp, _ = x_prompt.shape
    Bs, Ls, _ = x_sample.shape
    Tp, Ts = Bp * Lp, Bs * Ls
    lyr = 0

    def pad_lanes(v):
        return jnp.pad(v.astype(F32), (0, LANES - v.shape[0])).reshape(1, LANES)

    o1, o2, o3, o4 = QKV_DIM, QKV_DIM + D_MODEL, QKV_DIM + D_MODEL + 2 * DN_HEADS, QKV_DIM + 2 * D_MODEL + 2 * DN_HEADS
    wi = w_in[lyr]
    w_in_r = jnp.concatenate(
        [wi[:, :o2], wi[:, o3:], wi[:, o2:o3], jnp.zeros((D_MODEL, LANES - 2 * DN_HEADS), F32)], axis=1).astype(BF16)
    del o1, o4
    W = {
        "w_in": w_in_r,
        "w_conv": w_conv[lyr],
        "a_log": pad_lanes(a_log[lyr]),
        "dt_bias": pad_lanes(dt_bias[lyr]),
        "w_onorm": w_onorm[lyr].reshape(1, DN_HD),
        "w_pool": w_pool[lyr].astype(BF16),
        "pool_scale": pool_scale[lyr].reshape(1, D_MODEL),
        "w_out": w_out[lyr].astype(BF16),
        "ln1_g": ln1_g[lyr].reshape(1, D_MODEL), "ln1_b": ln1_b[lyr].reshape(1, D_MODEL),
        "w_cq": w_cq[lyr].astype(BF16),
        "w_co": w_co[lyr].astype(BF16),
        "ln2_g": ln2_g[lyr].reshape(1, D_MODEL), "ln2_b": ln2_b[lyr].reshape(1, D_MODEL),
    }

    M = mem_prompt.shape[1]
    memf = mem_prompt.reshape(Bp * M, D_MODEL)
    tmm = _row_tile(Bp * M, 512)
    mk = _mm(memf, w_ck[lyr].astype(BF16), tm=tmm, tn=D_MODEL).reshape(Bp, M, CA_HEADS, CA_HD)
    mv = _mm(memf, w_cv[lyr].astype(BF16), tm=tmm, tn=D_MODEL).reshape(Bp, M, CA_HEADS, CA_HD)
    x2_p, d_p, c_p, p_p = _group_to_x2(x_prompt, mk, mv, None, None, None, 0, W)
    x2_s, d_s, c_s, p_s = _group_to_x2(x_sample, cache_mem_k[lyr], cache_mem_v[lyr], state_conv[lyr],
                                       state_pool[lyr], state_delta[lyr], PAST_LEN, W)

    w_r = jnp.concatenate([w_router_group[lyr], w_router_expert[lyr],
                           jnp.zeros((D_MODEL, LANES - MOE_GROUPS - MOE_EXPERTS), F32)], axis=1)
    w_r_hi = w_r.astype(BF16)
    w_r_lo = (w_r - w_r_hi.astype(F32)).astype(BF16)
    tp, ts = _row_tile(Tp, 256), _row_tile(Ts, 256)
    idx_p, ew_p = _router(x2_p, w_r_hi, w_r_lo, tm=tp)
    idx_s, ew_s = _router(x2_s, w_r_hi, w_r_lo, tm=ts)
    T = Tp + Ts
    dest, pend = _plan(jnp.concatenate([idx_p, idx_s], axis=0), tm=_row_tile(T, 256))
    n_blocks = (2 * T + MOE_EXPERTS * (MOE_BLOCK - 1) + MOE_BLOCK - 1) // MOE_BLOCK
    pend_e = pend[0, :MOE_EXPERTS].astype(jnp.int32)
    block_start = jnp.arange(n_blocks, dtype=jnp.int32) * MOE_BLOCK
    block_expert = jnp.minimum(jnp.sum(block_start[:, None] >= pend_e[None, :], axis=1), MOE_EXPERTS - 1).astype(jnp.int32)
    n_used = (pend_e[MOE_EXPERTS - 1:] // MOE_BLOCK).astype(jnp.int32)
    xs = jnp.zeros((n_blocks * MOE_BLOCK, D_MODEL), F32)
    xs = _dispatch(dest[:Tp], x2_p, xs, tm=tp)
    xs = _dispatch(dest[Tp:], x2_s, xs, tm=ts)
    rows = _experts(block_expert, n_used, xs, w_gate[lyr], w_up[lyr], w_down[lyr])
    g3, b3 = ln3_g[lyr].reshape(1, D_MODEL), ln3_b[lyr].reshape(1, D_MODEL)
    y_p = _combine(dest[:Tp], x2_p, ew_p, g3, b3, rows, tm=tp).reshape(Bp, Lp, D_MODEL)
    y_s = _combine(dest[Tp:], x2_s, ew_s, g3, b3, rows, tm=ts).reshape(Bs, Ls, D_MODEL)

    return (y_p, y_s, d_p[None], c_p[None], p_p[None], mk[None], mv[None], d_s[None], c_s[None], p_s[None])
```

```python
import functools

import jax
import jax.numpy as jnp
from jax import lax
from jax.experimental import pallas as pl
from jax.experimental.pallas import tpu as pltpu

F32 = jnp.float32
BF16 = jnp.bfloat16

D_MODEL = 1024
DN_HEADS = 8
DN_HD = 128
QKV_DIM = 3 * DN_HEADS * DN_HD
CONV_W = 4
DN_CHUNK = 64
POOL_WINDOWS = (2, 4, 8, 16)
POOL_GD = D_MODEL // len(POOL_WINDOWS)
POOL_MAX = 16
CA_HEADS = 4
CA_HD = D_MODEL // CA_HEADS
MOE_GROUPS = 8
MOE_EPG = 8
MOE_EXPERTS = MOE_GROUPS * MOE_EPG
MOE_FF = D_MODEL // 4
MOE_BLOCK = 128
PAST_LEN = 16384
LN_EPS = 1e-5
NORM_EPS = 1e-6
ALPHA = 2.0 ** 0.25

LANES = 128
SUBLANES = 8
CONV_HALO = SUBLANES
POOL_HALO = POOL_MAX
MIN_CHUNK = 16
N_PROJ = 7 * D_MODEL + LANES
VMEM_LIMIT = 48 * 1024 * 1024
ROW_DMA_UNROLL = 8


def _cparams(n_axes, vmem=VMEM_LIMIT):
    return pltpu.CompilerParams(dimension_semantics=("arbitrary",) * n_axes, vmem_limit_bytes=vmem)


def _layer_norm(x, g, b):
    mu = jnp.mean(x, -1, keepdims=True)
    xc = x - mu
    var = jnp.mean(xc * xc, -1, keepdims=True)
    return xc * lax.rsqrt(var + LN_EPS) * g + b


def _softplus(x):
    return jnp.maximum(x, 0.0) + jnp.log1p(jnp.exp(-jnp.abs(x)))


def _dot(a, b):
    return jnp.dot(a, b, preferred_element_type=F32)


def _dot_nt(a, b):
    return lax.dot_general(a, b, (((1,), (1,)), ((), ())), preferred_element_type=F32)


def _dot_tn(a, b):
    return lax.dot_general(a, b, (((0,), (0,)), ((), ())), preferred_element_type=F32)


def _mm_body(x_ref, w_ref, o_ref):
    o_ref[...] = _dot(x_ref[...].astype(BF16), w_ref[...]).astype(o_ref.dtype)


def _mm(x, w, *, tm, tn, out_dtype=F32):
    T, K = x.shape
    N = w.shape[1]
    return pl.pallas_call(
        _mm_body,
        grid=(N // tn, T // tm),
        in_specs=[pl.BlockSpec((tm, K), lambda j, i: (i, 0)), pl.BlockSpec((K, tn), lambda j, i: (0, j))],
        out_specs=pl.BlockSpec((tm, tn), lambda j, i: (i, j)),
        out_shape=jax.ShapeDtypeStruct((T, N), out_dtype),
        compiler_params=_cparams(2),
        name="mm",
    )(x, w)


def _mixer_body(*refs, tl, chunk, pos0, has_state):
    if has_state:
        (qkv_ref, z_ref, p_ref, ga_ref, gb_ref, ab_ref, convp_ref, poolp_ref, s0_ref, wconv_ref, alog_ref, dtb_ref,
         wonorm_ref, wpool_ref, pscale_ref, merged_ref, sout_ref, qkvbuf, pbuf, obuf, gcb, betab, gt_ref) = refs
    else:
        (qkv_ref, z_ref, p_ref, ga_ref, gb_ref, ab_ref, wconv_ref, alog_ref, dtb_ref,
         wonorm_ref, wpool_ref, pscale_ref, merged_ref, sout_ref, qkvbuf, pbuf, obuf, gcb, betab, gt_ref) = refs
    l = pl.program_id(1)
    tlp = max(tl, MIN_CHUNK)
    C = max(chunk, MIN_CHUNK)
    n_chunks = tlp // C

    @pl.when(l == 0)
    def _init():
        qkvbuf[0:CONV_HALO, :] = jnp.zeros((CONV_HALO, QKV_DIM), F32)
        pbuf[0:POOL_HALO, :] = jnp.zeros((POOL_HALO, D_MODEL), F32)
        if has_state:
            qkvbuf[CONV_HALO - (CONV_W - 1):CONV_HALO, :] = convp_ref[0]
            pbuf[POOL_HALO - (POOL_MAX - 1):POOL_HALO, :] = poolp_ref[0]
            sout_ref[...] = s0_ref[...]
        else:
            sout_ref[...] = jnp.zeros(sout_ref.shape, F32)

    qkvbuf[CONV_HALO:CONV_HALO + tl, :] = qkv_ref[...]
    pbuf[POOL_HALO:POOL_HALO + tl, :] = p_ref[...]

    def pad_rows(x):
        if tlp == tl:
            return x
        return jnp.concatenate([x, jnp.zeros((tlp - tl, x.shape[1]), F32)], axis=0)

    ab = ab_ref[...]
    g_all = pad_rows(-jnp.exp(alog_ref[...]) * _softplus(ab + dtb_ref[...]))
    beta_all = pad_rows(jax.nn.sigmoid(ab))
    ri = lax.broadcasted_iota(jnp.int32, (tlp, tlp), 0)
    ci = lax.broadcasted_iota(jnp.int32, (tlp, tlp), 1)
    cum_mat = ((ri >= ci) & ((ri // C) == (ci // C))).astype(F32)
    gcum = jnp.dot(cum_mat, g_all, preferred_element_type=F32, precision=lax.Precision.HIGHEST)
    gcum_t = gcum.T
    for h in range(DN_HEADS):
        gt_ref[h] = jnp.broadcast_to(gcum_t[h:h + 1, :], (SUBLANES, tlp))
        gcb[h] = jnp.broadcast_to(gcum[:, h:h + 1], (tlp, LANES))
        betab[h] = jnp.broadcast_to(beta_all[:, DN_HEADS + h:DN_HEADS + h + 1], (tlp, LANES))

    ii = lax.broadcasted_iota(jnp.int32, (C, C), 0)
    jj = lax.broadcasted_iota(jnp.int32, (C, C), 1)
    causal = ii >= jj
    strict = ii > jj
    eye = (ii == jj).astype(F32)
    n_sq = max(C.bit_length() - 2, 0)

    def conv_slab(col0):
        cols = slice(col0, col0 + DN_HD)
        acc = qkvbuf[CONV_HALO:CONV_HALO + tl, cols] * wconv_ref[CONV_W - 1:CONV_W, cols]
        for i in range(CONV_W - 1):
            r0 = CONV_HALO - (CONV_W - 1) + i
            acc = acc + qkvbuf[r0:r0 + tl, cols] * wconv_ref[i:i + 1, cols]
        return pad_rows(acc * jax.nn.sigmoid(acc))

    heads = range(DN_HEADS)
    chunks = range(n_chunks)
    probs = [(h, c) for h in heads for c in chunks]
    rows = [slice(c * C, (c + 1) * C) for c in chunks]
    q, k, v, gcs, bs, egs, kbs, grows = [], [], [], [], [], [], [], []
    for h in heads:
        qh = conv_slab(h * DN_HD)
        kh = conv_slab((DN_HEADS + h) * DN_HD)
        v.append(conv_slab((2 * DN_HEADS + h) * DN_HD))
        q.append(qh * lax.rsqrt(jnp.sum(qh * qh, -1, keepdims=True) + NORM_EPS) * (DN_HD ** -0.5))
        kh = kh * lax.rsqrt(jnp.sum(kh * kh, -1, keepdims=True) + NORM_EPS)
        k.append(kh)
        gcs.append(gcb[h])
        bs.append(betab[h])
        egs.append(jnp.exp(gcs[h]))
        kbs.append(kh * bs[h])
        grows.append(gt_ref[h][0:1, :])

    aq = {(h, c): _dot_nt(jnp.concatenate([kbs[h][rows[c]], q[h][rows[c]]], axis=0).astype(BF16),
                          k[h][rows[c]].astype(BF16)) for h, c in probs}
    decay = {(h, c): jnp.exp(jnp.where(causal, gcs[h][rows[c], 0:C] - grows[h][:, c * C:(c + 1) * C], -jnp.inf))
             for h, c in probs}
    A = {p: jnp.where(strict, aq[p][0:C] * decay[p], 0.0) for p in probs}
    qk = {p: (aq[p][C:2 * C] * decay[p]).astype(BF16) for p in probs}
    P = {p: eye - A[p] for p in probs}
    Q = {p: _dot(A[p].astype(BF16), A[p].astype(BF16)) for p in probs}
    for _ in range(n_sq - 1):
        pq = {p: _dot(jnp.concatenate([P[p], Q[p]], axis=0).astype(BF16), Q[p].astype(BF16)) for p in probs}
        P = {p: P[p] + pq[p][0:C] for p in probs}
        Q = {p: pq[p][C:2 * C] for p in probs}
    P = {p: P[p] + _dot(P[p].astype(BF16), Q[p].astype(BF16)) for p in probs}
    sol = {(h, c): _dot(P[(h, c)].astype(BF16),
                        jnp.concatenate([v[h][rows[c]] * bs[h][rows[c]], kbs[h][rows[c]] * egs[h][rows[c]]],
                                        axis=1).astype(BF16)) for h, c in probs}

    S = [sout_ref[0, h] for h in heads]
    outs = {}
    for c in chunks:
        r = rows[c]
        ws_qs = [_dot(jnp.concatenate([sol[(h, c)][:, DN_HD:2 * DN_HD], q[h][r] * egs[h][r]], axis=0).astype(BF16),
                      S[h].astype(BF16)) for h in heads]
        v_new = [(sol[(h, c)][:, 0:DN_HD] - ws_qs[h][0:C]).astype(BF16) for h in heads]
        glast = [gcs[h][r][C - 1:C, :] for h in heads]
        S = [S[h] * jnp.exp(glast[h]) + _dot_tn((k[h][r] * jnp.exp(glast[h] - gcs[h][r])).astype(BF16), v_new[h])
             for h in heads]
        for h in heads:
            outs[(h, c)] = ws_qs[h][C:2 * C] + _dot(qk[(h, c)], v_new[h])
    for h in heads:
        sout_ref[0, h] = S[h]
        o = outs[(h, 0)] if n_chunks == 1 else jnp.concatenate([outs[(h, c)] for c in chunks], axis=0)
        o = o[0:tl]
        o = o * lax.rsqrt(jnp.mean(o * o, -1, keepdims=True) + NORM_EPS) * wonorm_ref[...]
        zh = z_ref[:, h * DN_HD:(h + 1) * DN_HD]
        obuf[:, h * DN_HD:(h + 1) * DN_HD] = o * (zh * jax.nn.sigmoid(zh))

    if pos0 == 0:
        pos = l * tl + lax.broadcasted_iota(jnp.int32, (tl, 1), 0)
    for gi, win in enumerate(POOL_WINDOWS):
        cs = slice(gi * POOL_GD, (gi + 1) * POOL_GD)
        s = pbuf[POOL_HALO:POOL_HALO + tl, cs]
        for j in range(1, win):
            s = s + pbuf[POOL_HALO - j:POOL_HALO - j + tl, cs]
        if pos0 == 0:
            cnt = jnp.minimum(win, pos + 1).astype(F32)
        else:
            cnt = float(min(win, pos0 + 1))
        pooled = s / cnt - p_ref[:, cs]
        bb = _dot(pooled.astype(BF16), wpool_ref[gi]) * pscale_ref[:, cs]
        merged_ref[:, cs] = (jax.nn.sigmoid(ga_ref[:, cs]) * obuf[:, cs]
                             + jax.nn.sigmoid(gb_ref[:, cs]) * bb).astype(merged_ref.dtype)

    if tl >= POOL_HALO:
        qkvbuf[0:CONV_HALO, :] = qkvbuf[tl:tl + CONV_HALO, :]
        pbuf[0:POOL_HALO, :] = pbuf[tl:tl + POOL_HALO, :]


def _mixer(proj, B, L, conv_prev, pool_prev, s0, wconv, alog, dtb, wonorm, wpool, pscale, *, pos0):
    tl = min(256, L)
    nL = L // tl
    assert nL == 1 or tl >= POOL_HALO
    chunk = min(DN_CHUNK, tl)
    tlp = max(tl, MIN_CHUNK)
    has_state = s0 is not None
    row = lambda b, l: b * nL + l
    in_specs = [
        pl.BlockSpec((tl, QKV_DIM), lambda b, l: (row(b, l), 0)),
        pl.BlockSpec((tl, D_MODEL), lambda b, l: (row(b, l), 3)),
        pl.BlockSpec((tl, D_MODEL), lambda b, l: (row(b, l), 4)),
        pl.BlockSpec((tl, D_MODEL), lambda b, l: (row(b, l), 5)),
        pl.BlockSpec((tl, D_MODEL), lambda b, l: (row(b, l), 6)),
        pl.BlockSpec((tl, LANES), lambda b, l: (row(b, l), 7 * D_MODEL // LANES)),
    ]
    args = [proj] * 6
    if has_state:
        in_specs += [
            pl.BlockSpec((1, CONV_W - 1, QKV_DIM), lambda b, l: (b, 0, 0)),
            pl.BlockSpec((1, POOL_MAX - 1, D_MODEL), lambda b, l: (b, 0, 0)),
            pl.BlockSpec((1, DN_HEADS, DN_HD, DN_HD), lambda b, l: (b, 0, 0, 0)),
        ]
        args += [conv_prev, pool_prev, s0]
    const2 = lambda b, l: (0, 0)
    in_specs += [
        pl.BlockSpec((CONV_W, QKV_DIM), const2),
        pl.BlockSpec((1, LANES), const2),
        pl.BlockSpec((1, LANES), const2),
        pl.BlockSpec((1, DN_HD), const2),
        pl.BlockSpec((len(POOL_WINDOWS), POOL_GD, POOL_GD), lambda b, l: (0, 0, 0)),
        pl.BlockSpec((1, D_MODEL), const2),
    ]
    args += [wconv, alog, dtb, wonorm, wpool, pscale]
    return pl.pallas_call(
        functools.partial(_mixer_body, tl=tl, chunk=chunk, pos0=pos0, has_state=has_state),
        grid=(B, nL),
        in_specs=in_specs,
        out_specs=[
            pl.BlockSpec((tl, D_MODEL), lambda b, l: (row(b, l), 0)),
            pl.BlockSpec((1, DN_HEADS, DN_HD, DN_HD), lambda b, l: (b, 0, 0, 0)),
        ],
        out_shape=[
            jax.ShapeDtypeStruct((B * L, D_MODEL), BF16),
            jax.ShapeDtypeStruct((B, DN_HEADS, DN_HD, DN_HD), F32),
        ],
        scratch_shapes=[
            pltpu.VMEM((CONV_HALO + tl, QKV_DIM), F32),
            pltpu.VMEM((POOL_HALO + tl, D_MODEL), F32),
            pltpu.VMEM((tl, D_MODEL), F32),
            pltpu.VMEM((DN_HEADS, tlp, LANES), F32),
            pltpu.VMEM((DN_HEADS, tlp, LANES), F32),
            pltpu.VMEM((DN_HEADS, SUBLANES, tlp), F32),
        ],
        compiler_params=_cparams(2),
        name="mixer",
    )(*args)


def _proj_ln_body(a_ref, w_ref, r_ref, g_ref, b_ref, o_ref):
    y = ALPHA * r_ref[...] + _dot(a_ref[...], w_ref[...])
    o_ref[...] = _layer_norm(y, g_ref[...], b_ref[...])


def _proj_ln(a, w, resid, g, b, *, tm):
    T = a.shape[0]
    const = lambda i: (0, 0)
    return pl.pallas_call(
        _proj_ln_body,
        grid=(T // tm,),
        in_specs=[
            pl.BlockSpec((tm, D_MODEL), lambda i: (i, 0)),
            pl.BlockSpec((D_MODEL, D_MODEL), const),
            pl.BlockSpec((tm, D_MODEL), lambda i: (i, 0)),
            pl.BlockSpec((1, D_MODEL), const),
            pl.BlockSpec((1, D_MODEL), const),
        ],
        out_specs=pl.BlockSpec((tm, D_MODEL), lambda i: (i, 0)),
        out_shape=jax.ShapeDtypeStruct((T, D_MODEL), F32),
        compiler_params=_cparams(1),
        name="proj_ln",
    )(a, w, resid, g, b)


def _attn_body(q_ref, k_ref, v_ref, o_ref):
    scale = CA_HD ** -0.5
    for hh in range(CA_HEADS):
        cs = slice(hh * CA_HD, (hh + 1) * CA_HD)
        s = _dot_nt(q_ref[:, cs].astype(BF16), k_ref[0, :, cs].astype(BF16)) * scale
        m = jnp.max(s, -1, keepdims=True)
        p = jnp.exp(s - m)
        denom = jnp.sum(p, -1, keepdims=True)
        o = _dot(p.astype(BF16), v_ref[0, :, cs].astype(BF16)) / denom
        o_ref[:, cs] = o.astype(o_ref.dtype)


def _attention(q, mem_k, mem_v, B, L):
    tq = min(512, L)
    nq = L // tq
    M = mem_k.shape[1]
    kv_spec = pl.BlockSpec((1, M, D_MODEL), lambda b, i: (b, 0, 0))
    return pl.pallas_call(
        _attn_body,
        grid=(B, nq),
        in_specs=[pl.BlockSpec((tq, D_MODEL), lambda b, i: (b * nq + i, 0)), kv_spec, kv_spec],
        out_specs=pl.BlockSpec((tq, D_MODEL), lambda b, i: (b * nq + i, 0)),
        out_shape=jax.ShapeDtypeStruct((B * L, D_MODEL), BF16),
        compiler_params=_cparams(2),
        name="attention",
    )(q, mem_k, mem_v)


HALVES = CA_HD // LANES
KV_ROWS = CA_HEADS * HALVES


def _attn_rows_body(q_ref, xk_ref, xv_ref, o_ref, *, bb, L, M):
    scale = CA_HD ** -0.5
    R = M * KV_ROWS
    lane = lax.broadcasted_iota(jnp.int32, (L, R), 1)
    for i in range(bb):
        q = q_ref[i * L:(i + 1) * L, :]
        qm = jnp.concatenate([q[:, j * LANES:(j + 1) * LANES] for j in range(KV_ROWS)], axis=0).astype(BF16)
        xk = xk_ref[i * R:(i + 1) * R, :].astype(BF16)
        xv = xv_ref[i * R:(i + 1) * R, :].astype(BF16)
        g = _dot_nt(qm, xk)
        ps, invs = [], []
        for hh in range(CA_HEADS):
            s = None
            for half in range(HALVES):
                j = hh * HALVES + half
                part = g[j * L:(j + 1) * L]
                if half:
                    part = pltpu.roll(part, R - half * CA_HEADS, 1)
                s = part if s is None else s + part
            s = jnp.where((lane % KV_ROWS) == hh, s * scale, -jnp.inf)
            p = jnp.exp(s - jnp.max(s, -1, keepdims=True))
            inv = 1.0 / jnp.sum(p, -1, keepdims=True)
            for half in range(HALVES):
                ps.append(pltpu.roll(p, half * CA_HEADS, 1) if half else p)
                invs.append(inv)
        o = _dot(jnp.concatenate(ps, axis=0).astype(BF16), xv)
        for j in range(KV_ROWS):
            o_ref[i * L:(i + 1) * L, j * LANES:(j + 1) * LANES] = (o[j * L:(j + 1) * L] * invs[j]).astype(o_ref.dtype)


def _attention_rows(q, xk, xv, B, L, M):
    bb = 4 if B % 4 == 0 else 1
    R = M * KV_ROWS
    kv_spec = pl.BlockSpec((bb * R, LANES), lambda b: (b, 0))
    return pl.pallas_call(
        functools.partial(_attn_rows_body, bb=bb, L=L, M=M),
        grid=(B // bb,),
        in_specs=[pl.BlockSpec((bb * L, D_MODEL), lambda b: (b, 0)), kv_spec, kv_spec],
        out_specs=pl.BlockSpec((bb * L, D_MODEL), lambda b: (b, 0)),
        out_shape=jax.ShapeDtypeStruct((B * L, D_MODEL), BF16),
        compiler_params=_cparams(1),
        name="attention_rows",
    )(q, xk, xv)


def _router_body(x_ref, whi_ref, wlo_ref, idx_ref, w_ref):
    x = x_ref[...]
    x_hi = x.astype(BF16)
    x_lo = (x - x_hi.astype(F32)).astype(BF16)
    logits = _dot(x_hi, whi_ref[...]) + (_dot(x_hi, wlo_ref[...]) + _dot(x_lo, whi_ref[...]))
    tm = x.shape[0]
    lane = lax.broadcasted_iota(jnp.int32, (tm, LANES), 1)
    neg = -jnp.inf
    g_log = jnp.where(lane < MOE_GROUPS, logits, neg)
    g_max = jnp.max(g_log, -1, keepdims=True)
    g_sel = jnp.min(jnp.where(g_log == g_max, lane, LANES), -1, keepdims=True)
    g_w = 1.0 / jnp.sum(jnp.exp(g_log - g_max), -1, keepdims=True)
    lo = MOE_GROUPS + g_sel * MOE_EPG
    e_log = jnp.where((lane >= lo) & (lane < lo + MOE_EPG), logits, neg)
    v1 = jnp.max(e_log, -1, keepdims=True)
    i1 = jnp.min(jnp.where(e_log == v1, lane, LANES), -1, keepdims=True)
    e_log2 = jnp.where(lane == i1, neg, e_log)
    v2 = jnp.max(e_log2, -1, keepdims=True)
    i2 = jnp.min(jnp.where(e_log2 == v2, lane, LANES), -1, keepdims=True)
    t = jnp.exp(v2 - v1)
    w1 = g_w / (1.0 + t)
    w2 = g_w * t / (1.0 + t)
    first = lax.broadcasted_iota(jnp.int32, (tm, 2), 1) == 0
    idx_ref[...] = jnp.where(first, i1, i2) - MOE_GROUPS
    w_ref[...] = jnp.where(first, w1, w2)


def _router(x, w_hi, w_lo, *, tm):
    T = x.shape[0]
    const = lambda i: (0, 0)
    return pl.pallas_call(
        _router_body,
        grid=(T // tm,),
        in_specs=[
            pl.BlockSpec((tm, D_MODEL), lambda i: (i, 0)),
            pl.BlockSpec((D_MODEL, LANES), const),
            pl.BlockSpec((D_MODEL, LANES), const),
        ],
        out_specs=[pl.BlockSpec((tm, 2), lambda i: (i, 0)), pl.BlockSpec((tm, 2), lambda i: (i, 0))],
        out_shape=[jax.ShapeDtypeStruct((T, 2), jnp.int32), jax.ShapeDtypeStruct((T, 2), F32)],
        compiler_params=_cparams(1),
        name="router",
    )(x, w_hi, w_lo)


def _plan_body(idx_ref, dest_ref, pend_ref, carry, pstart):
    ph = pl.program_id(0)
    i = pl.program_id(1)
    tm = idx_ref.shape[0]
    lane = lax.broadcasted_iota(jnp.int32, (tm, LANES), 1)
    oh0 = (lane == idx_ref[:, 0:1]).astype(F32)
    oh1 = (lane == idx_ref[:, 1:2]).astype(F32)
    both = oh0 + oh1

    @pl.when((ph == 0) & (i == 0))
    def _():
        carry[...] = jnp.zeros(carry.shape, F32)

    @pl.when(ph == 0)
    def _():
        carry[...] += jnp.sum(both, 0, keepdims=True)

    @pl.when((ph == 1) & (i == 0))
    def _():
        padded = jnp.floor((carry[...] + (MOE_BLOCK - 1)) * (1.0 / MOE_BLOCK)) * MOE_BLOCK
        a = lax.broadcasted_iota(jnp.int32, (LANES, LANES), 0)
        b = lax.broadcasted_iota(jnp.int32, (LANES, LANES), 1)
        upper = (a < b).astype(F32)
        ps = jnp.dot(jnp.broadcast_to(padded, (SUBLANES, LANES)), upper, preferred_element_type=F32,
                     precision=lax.Precision.HIGHEST)
        pstart[...] = ps[0:1]
        carry[...] = jnp.zeros(carry.shape, F32)

    @pl.when(ph == 1)
    def _():
        r = lax.broadcasted_iota(jnp.int32, (tm, tm), 0)
        c = lax.broadcasted_iota(jnp.int32, (tm, tm), 1)
        before = (r > c).astype(BF16)
        base = _dot(before, both.astype(BF16)) + carry[...] + pstart[...]
        d0 = jnp.sum(oh0 * base, -1, keepdims=True)
        d1 = jnp.sum(oh1 * base, -1, keepdims=True)
        first = lax.broadcasted_iota(jnp.int32, (tm, 2), 1) == 0
        dest_ref[...] = jnp.where(first, d0, d1).astype(jnp.int32)
        carry[...] += jnp.sum(both, 0, keepdims=True)
        padded_tot = jnp.floor((carry[...] + (MOE_BLOCK - 1)) * (1.0 / MOE_BLOCK)) * MOE_BLOCK
        pend_ref[...] = jnp.broadcast_to(pstart[...] + padded_tot, pend_ref.shape)


def _plan(idx, *, tm):
    T = idx.shape[0]
    return pl.pallas_call(
        _plan_body,
        grid=(2, T // tm),
        in_specs=[pl.BlockSpec((tm, 2), lambda p, i: (i, 0))],
        out_specs=[pl.BlockSpec((tm, 2), lambda p, i: (i * p, 0)), pl.BlockSpec((SUBLANES, LANES), lambda p, i: (0, 0))],
        out_shape=[jax.ShapeDtypeStruct((T, 2), jnp.int32), jax.ShapeDtypeStruct((SUBLANES, LANES), F32)],
        scratch_shapes=[pltpu.VMEM((1, LANES), F32), pltpu.VMEM((1, LANES), F32)],
        compiler_params=_cparams(2),
        name="plan",
    )(idx)


def _row_copy(src_ref, s, dst_ref, d, sem):
    return pltpu.make_async_copy(src_ref.at[pl.ds(s, 1), :], dst_ref.at[pl.ds(d, 1), :], sem)


def _dispatch_body(dest_ref, x_ref, xs_in_ref, xs_ref, sem):
    del xs_in_ref
    tm = x_ref.shape[0]

    def start(t, c):
        _row_copy(x_ref, t, xs_ref, dest_ref[0, 0, 2 * t], sem).start()
        _row_copy(x_ref, t, xs_ref, dest_ref[0, 0, 2 * t + 1], sem).start()
        return c

    lax.fori_loop(0, tm, start, 0, unroll=ROW_DMA_UNROLL)
    for _ in range(2):
        pltpu.make_async_copy(x_ref, xs_ref.at[pl.ds(0, tm), :], sem).wait()


def _dispatch(dest, x, xs, *, tm):
    T = x.shape[0]
    nt = T // tm
    return pl.pallas_call(
        _dispatch_body,
        grid=(nt,),
        in_specs=[
            pl.BlockSpec((1, 1, 2 * tm), lambda i: (i, 0, 0), memory_space=pltpu.SMEM),
            pl.BlockSpec((tm, D_MODEL), lambda i: (i, 0)),
            pl.BlockSpec(memory_space=pl.ANY),
        ],
        out_specs=pl.BlockSpec(memory_space=pl.ANY),
        out_shape=jax.ShapeDtypeStruct(xs.shape, xs.dtype),
        scratch_shapes=[pltpu.SemaphoreType.DMA(())],
        input_output_aliases={2: 0},
        compiler_params=_cparams(1),
        name="dispatch",
    )(dest.reshape(nt, 1, 2 * tm), x, xs)


def _combine_body(dest_ref, x_ref, ew_ref, g_ref, b_ref, rows_ref, o_ref, gbuf, sem):
    tm = x_ref.shape[0]

    def start(t, c):
        _row_copy(rows_ref, dest_ref[0, 0, 2 * t], gbuf.at[0], t, sem).start()
        _row_copy(rows_ref, dest_ref[0, 0, 2 * t + 1], gbuf.at[1], t, sem).start()
        return c

    lax.fori_loop(0, tm, start, 0, unroll=ROW_DMA_UNROLL)
    for half in range(2):
        pltpu.make_async_copy(rows_ref.at[pl.ds(0, tm), :], gbuf.at[half], sem).wait()
    y = ew_ref[:, 0:1] * gbuf[0] + ew_ref[:, 1:2] * gbuf[1]
    o_ref[...] = _layer_norm(ALPHA * x_ref[...] + y, g_ref[...], b_ref[...])


def _combine(dest, x, ew, g, b, rows, *, tm):
    T = x.shape[0]
    nt = T // tm
    const = lambda i: (0, 0)
    return pl.pallas_call(
        _combine_body,
        grid=(nt,),
        in_specs=[
            pl.BlockSpec((1, 1, 2 * tm), lambda i: (i, 0, 0), memory_space=pltpu.SMEM),
            pl.BlockSpec((tm, D_MODEL), lambda i: (i, 0)),
            pl.BlockSpec((tm, 2), lambda i: (i, 0)),
            pl.BlockSpec((1, D_MODEL), const),
            pl.BlockSpec((1, D_MODEL), const),
            pl.BlockSpec(memory_space=pl.ANY),
        ],
        out_specs=pl.BlockSpec((tm, D_MODEL), lambda i: (i, 0)),
        out_shape=jax.ShapeDtypeStruct((T, D_MODEL), F32),
        scratch_shapes=[pltpu.VMEM((2, tm, D_MODEL), F32), pltpu.SemaphoreType.DMA(())],
        compiler_params=_cparams(1),
        name="combine",
    )(dest.reshape(nt, 1, 2 * tm), x, ew, g, b, rows)


def _experts_body(be_ref, nu_ref, xs_ref, wg_ref, wu_ref, wd_ref, o_ref):
    del be_ref
    i = pl.program_id(0)

    @pl.when(i < nu_ref[0])
    def _():
        xb = xs_ref[...].astype(BF16)
        gate = _dot(xb, wg_ref[0].astype(BF16))
        up = _dot(xb, wu_ref[0].astype(BF16))
        hid = gate * jax.nn.sigmoid(gate) * up
        o_ref[...] = _dot(hid.astype(BF16), wd_ref[0].astype(BF16))

    @pl.when(i >= nu_ref[0])
    def _():
        o_ref[...] = jnp.zeros(o_ref.shape, F32)


def _experts(block_expert, n_used, xs, w_gate, w_up, w_down):
    R = xs.shape[0]
    nb = R // MOE_BLOCK
    grid_spec = pltpu.PrefetchScalarGridSpec(
        num_scalar_prefetch=2,
        grid=(nb,),
        in_specs=[
            pl.BlockSpec((MOE_BLOCK, D_MODEL), lambda i, be, nu: (i, 0)),
            pl.BlockSpec((1, D_MODEL, MOE_FF), lambda i, be, nu: (be[i], 0, 0)),
            pl.BlockSpec((1, D_MODEL, MOE_FF), lambda i, be, nu: (be[i], 0, 0)),
            pl.BlockSpec((1, MOE_FF, D_MODEL), lambda i, be, nu: (be[i], 0, 0)),
        ],
        out_specs=pl.BlockSpec((MOE_BLOCK, D_MODEL), lambda i, be, nu: (i, 0)),
    )
    return pl.pallas_call(
        _experts_body,
        grid_spec=grid_spec,
        out_shape=jax.ShapeDtypeStruct((R, D_MODEL), F32),
        compiler_params=_cparams(1),
        name="experts",
    )(block_expert, n_used, xs, w_gate, w_up, w_down)


def _row_tile(T, pref):
    t = pref
    while T % t:
        t //= 2
    return t


def _group_to_x2(x, mem_k, mem_v, conv_prev, pool_prev, s0, pos0, W):
    B, L, _ = x.shape
    T = B * L
    xf = x.reshape(T, D_MODEL)
    tm = _row_tile(T, 512)
    proj = _mm(xf, W["w_in"], tm=tm, tn=N_PROJ // 3)
    merged, s_new = _mixer(proj, B, L, conv_prev, pool_prev, s0, W["w_conv"], W["a_log"], W["dt_bias"], W["w_onorm"],
                           W["w_pool"], W["pool_scale"], pos0=pos0)
    x1 = _proj_ln(merged, W["w_out"], xf, W["ln1_g"], W["ln1_b"], tm=tm)
    q = _mm(x1, W["w_cq"], tm=tm, tn=D_MODEL)
    if mem_k.ndim == 3:
        att = _attention(q, mem_k, mem_v, B, L)
    else:
        M = mem_k.shape[1]

        def rows_view(m):
            m = m.reshape(B, M, CA_HEADS, HALVES, LANES)
            return jnp.swapaxes(m, 2, 3).reshape(B * M * KV_ROWS, LANES)

        att = _attention_rows(q, rows_view(mem_k), rows_view(mem_v), B, L, M)
    x2 = _proj_ln(att, W["w_co"], x1, W["ln2_g"], W["ln2_b"], tm=tm)
    proj3 = proj.reshape(B, L, N_PROJ)
    qkv_raw = proj3[:, max(L - (CONV_W - 1), 0):, 0:QKV_DIM]
    p_raw = proj3[:, max(L - (POOL_MAX - 1), 0):, 4 * D_MODEL:5 * D_MODEL]
    if conv_prev is None:
        conv_prev = jnp.zeros((B, CONV_W - 1, QKV_DIM), F32)
        pool_prev = jnp.zeros((B, POOL_MAX - 1, D_MODEL), F32)
    conv_new = jnp.concatenate([conv_prev, qkv_raw], axis=1)[:, -(CONV_W - 1):]
    pool_new = jnp.concatenate([pool_prev, p_raw], axis=1)[:, -(POOL_MAX - 1):]
    return x2, s_new, conv_new, pool_new


def kernel(x_prompt, x_sample, cache_mem_k, cache_mem_v, state_delta, state_conv, state_pool, mem_prompt, w_in, w_conv, a_log, dt_bias, w_onorm, w_pool, pool_scale, w_out, ln1_g, ln1_b, w_cq, w_ck, w_cv, w_co, ln2_g, ln2_b, w_router_group, w_router_expert, w_gate, w_up, w_down, ln3_g, ln3_b):
    Bp, Lp, _ = x_prompt.shape
    Bs, Ls, _ = x_sample.shape
    Tp, Ts = Bp * Lp, Bs * Ls
    lyr = 0

    def pad_lanes(v):
        return jnp.pad(v.astype(F32), (0, LANES - v.shape[0])).reshape(1, LANES)

    o1, o2, o3, o4 = QKV_DIM, QKV_DIM + D_MODEL, QKV_DIM + D_MODEL + 2 * DN_HEADS, QKV_DIM + 2 * D_MODEL + 2 * DN_HEADS
    wi = w_in[lyr]
    w_in_r = jnp.concatenate(
        [wi[:, :o2], wi[:, o3:], wi[:, o2:o3], jnp.zeros((D_MODEL, LANES - 2 * DN_HEADS), F32)], axis=1).astype(BF16)
    del o1, o4
    W = {
        "w_in": w_in_r,
        "w_conv": w_conv[lyr],
        "a_log": pad_lanes(a_log[lyr]),
        "dt_bias": pad_lanes(dt_bias[lyr]),
        "w_onorm": w_onorm[lyr].reshape(1, DN_HD),
        "w_pool": w_pool[lyr].astype(BF16),
        "pool_scale": pool_scale[lyr].reshape(1, D_MODEL),
        "w_out": w_out[lyr].astype(BF16),
        "ln1_g": ln1_g[lyr].reshape(1, D_MODEL), "ln1_b": ln1_b[lyr].reshape(1, D_MODEL),
        "w_cq": w_cq[lyr].astype(BF16),
        "w_co": w_co[lyr].astype(BF16),
        "ln2_g": ln2_g[lyr].reshape(1, D_MODEL), "ln2_b": ln2_b[lyr].reshape(1, D_MODEL),
    }

    M = mem_prompt.shape[1]
    memf = mem_prompt.reshape(Bp * M, D_MODEL)
    tmm = _row_tile(Bp * M, 512)
    mk = _mm(memf, w_ck[lyr].astype(BF16), tm=tmm, tn=D_MODEL).reshape(Bp, M, D_MODEL)
    mv = _mm(memf, w_cv[lyr].astype(BF16), tm=tmm, tn=D_MODEL).reshape(Bp, M, D_MODEL)
    x2_p, d_p, c_p, p_p = _group_to_x2(x_prompt, mk, mv, None, None, None, 0, W)
    mk = mk.reshape(Bp, M, CA_HEADS, CA_HD)
    mv = mv.reshape(Bp, M, CA_HEADS, CA_HD)
    x2_s, d_s, c_s, p_s = _group_to_x2(x_sample, cache_mem_k[lyr], cache_mem_v[lyr], state_conv[lyr],
                                       state_pool[lyr], state_delta[lyr], PAST_LEN, W)

    w_r = jnp.concatenate([w_router_group[lyr], w_router_expert[lyr],
                           jnp.zeros((D_MODEL, LANES - MOE_GROUPS - MOE_EXPERTS), F32)], axis=1)
    w_r_hi = w_r.astype(BF16)
    w_r_lo = (w_r - w_r_hi.astype(F32)).astype(BF16)
    tp, ts = _row_tile(Tp, 256), _row_tile(Ts, 256)
    idx_p, ew_p = _router(x2_p, w_r_hi, w_r_lo, tm=tp)
    idx_s, ew_s = _router(x2_s, w_r_hi, w_r_lo, tm=ts)
    T = Tp + Ts
    dest, pend = _plan(jnp.concatenate([idx_p, idx_s], axis=0), tm=_row_tile(T, 1024))
    n_blocks = (2 * T + MOE_EXPERTS * (MOE_BLOCK - 1) + MOE_BLOCK - 1) // MOE_BLOCK
    pend_e = pend[0, :MOE_EXPERTS].astype(jnp.int32)
    block_start = jnp.arange(n_blocks, dtype=jnp.int32) * MOE_BLOCK
    block_expert = jnp.minimum(jnp.sum(block_start[:, None] >= pend_e[None, :], axis=1), MOE_EXPERTS - 1).astype(jnp.int32)
    n_used = (pend_e[MOE_EXPERTS - 1:] // MOE_BLOCK).astype(jnp.int32)
    xs = jnp.zeros((n_blocks * MOE_BLOCK, D_MODEL), F32)
    xs = _dispatch(dest[:Tp], x2_p, xs, tm=tp)
    xs = _dispatch(dest[Tp:], x2_s, xs, tm=ts)
    rows = _experts(block_expert, n_used, xs, w_gate[lyr], w_up[lyr], w_down[lyr])
    g3, b3 = ln3_g[lyr].reshape(1, D_MODEL), ln3_b[lyr].reshape(1, D_MODEL)
    y_p = _combine(dest[:Tp], x2_p, ew_p, g3, b3, rows, tm=tp).reshape(Bp, Lp, D_MODEL)
    y_s = _combine(dest[Tp:], x2_s, ew_s, g3, b3, rows, tm=ts).reshape(Bs, Ls, D_MODEL)

    return (y_p, y_s, d_p[None], c_p[None], p_p[None], mk[None], mv[None], d_s[None], c_s[None], p_s[None])
```

```python
import functools

import jax
import jax.numpy as jnp
from jax import lax
from jax.experimental import pallas as pl
from jax.experimental.pallas import tpu as pltpu

F32 = jnp.float32
BF16 = jnp.bfloat16

D_MODEL = 1024
DN_HEADS = 8
DN_HD = 128
QKV_DIM = 3 * DN_HEADS * DN_HD
CONV_W = 4
DN_CHUNK = 64
POOL_WINDOWS = (2, 4, 8, 16)
POOL_GD = D_MODEL // len(POOL_WINDOWS)
POOL_MAX = 16
CA_HEADS = 4
CA_HD = D_MODEL // CA_HEADS
MOE_GROUPS = 8
MOE_EPG = 8
MOE_EXPERTS = MOE_GROUPS * MOE_EPG
MOE_FF = D_MODEL // 4
MOE_BLOCK = 256
PAST_LEN = 16384
LN_EPS = 1e-5
NORM_EPS = 1e-6
ALPHA = 2.0 ** 0.25

LANES = 128
SUBLANES = 8
CONV_HALO = SUBLANES
POOL_HALO = POOL_MAX
MIN_CHUNK = 16
N_PROJ = 7 * D_MODEL + LANES
VMEM_LIMIT = 48 * 1024 * 1024
ROW_DMA_UNROLL = 8


def _cparams(n_axes, vmem=VMEM_LIMIT):
    return pltpu.CompilerParams(dimension_semantics=("arbitrary",) * n_axes, vmem_limit_bytes=vmem)


def _layer_norm(x, g, b):
    mu = jnp.mean(x, -1, keepdims=True)
    xc = x - mu
    var = jnp.mean(xc * xc, -1, keepdims=True)
    return xc * lax.rsqrt(var + LN_EPS) * g + b


def _softplus(x):
    return jnp.maximum(x, 0.0) + jnp.log1p(jnp.exp(-jnp.abs(x)))


def _dot(a, b):
    return jnp.dot(a, b, preferred_element_type=F32)


def _dot_nt(a, b):
    return lax.dot_general(a, b, (((1,), (1,)), ((), ())), preferred_element_type=F32)


def _dot_tn(a, b):
    return lax.dot_general(a, b, (((0,), (0,)), ((), ())), preferred_element_type=F32)


def _mm_body(x_ref, w_ref, o_ref):
    o_ref[...] = _dot(x_ref[...].astype(BF16), w_ref[...]).astype(o_ref.dtype)


def _mm(x, w, *, tm, tn, out_dtype=F32):
    T, K = x.shape
    N = w.shape[1]
    return pl.pallas_call(
        _mm_body,
        grid=(N // tn, T // tm),
        in_specs=[pl.BlockSpec((tm, K), lambda j, i: (i, 0)), pl.BlockSpec((K, tn), lambda j, i: (0, j))],
        out_specs=pl.BlockSpec((tm, tn), lambda j, i: (i, j)),
        out_shape=jax.ShapeDtypeStruct((T, N), out_dtype),
        compiler_params=_cparams(2),
        name="mm",
    )(x, w)


def _mixer_body(*refs, tl, chunk, pos0, has_state):
    if has_state:
        (qkv_ref, z_ref, p_ref, ga_ref, gb_ref, ab_ref, convp_ref, poolp_ref, s0_ref, wconv_ref, alog_ref, dtb_ref,
         wonorm_ref, wpool_ref, pscale_ref, merged_ref, sout_ref, qkvbuf, pbuf, obuf, gcb, betab, gt_ref) = refs
    else:
        (qkv_ref, z_ref, p_ref, ga_ref, gb_ref, ab_ref, wconv_ref, alog_ref, dtb_ref,
         wonorm_ref, wpool_ref, pscale_ref, merged_ref, sout_ref, qkvbuf, pbuf, obuf, gcb, betab, gt_ref) = refs
    l = pl.program_id(1)
    tlp = max(tl, MIN_CHUNK)
    C = max(chunk, MIN_CHUNK)
    n_chunks = tlp // C

    @pl.when(l == 0)
    def _init():
        qkvbuf[0:CONV_HALO, :] = jnp.zeros((CONV_HALO, QKV_DIM), F32)
        pbuf[0:POOL_HALO, :] = jnp.zeros((POOL_HALO, D_MODEL), F32)
        if has_state:
            qkvbuf[CONV_HALO - (CONV_W - 1):CONV_HALO, :] = convp_ref[0]
            pbuf[POOL_HALO - (POOL_MAX - 1):POOL_HALO, :] = poolp_ref[0]
            sout_ref[...] = s0_ref[...]
        else:
            sout_ref[...] = jnp.zeros(sout_ref.shape, F32)

    qkvbuf[CONV_HALO:CONV_HALO + tl, :] = qkv_ref[...]
    pbuf[POOL_HALO:POOL_HALO + tl, :] = p_ref[...]

    def pad_rows(x):
        if tlp == tl:
            return x
        return jnp.concatenate([x, jnp.zeros((tlp - tl, x.shape[1]), F32)], axis=0)

    ab = ab_ref[...]
    g_all = pad_rows(-jnp.exp(alog_ref[...]) * _softplus(ab + dtb_ref[...]))
    beta_all = pad_rows(jax.nn.sigmoid(ab))
    ri = lax.broadcasted_iota(jnp.int32, (tlp, tlp), 0)
    ci = lax.broadcasted_iota(jnp.int32, (tlp, tlp), 1)
    cum_mat = ((ri >= ci) & ((ri // C) == (ci // C))).astype(F32)
    gcum = jnp.dot(cum_mat, g_all, preferred_element_type=F32, precision=lax.Precision.HIGHEST)
    gcum_t = gcum.T
    for h in range(DN_HEADS):
        gt_ref[h] = jnp.broadcast_to(gcum_t[h:h + 1, :], (SUBLANES, tlp))
        gcb[h] = jnp.broadcast_to(gcum[:, h:h + 1], (tlp, LANES))
        betab[h] = jnp.broadcast_to(beta_all[:, DN_HEADS + h:DN_HEADS + h + 1], (tlp, LANES))

    ii = lax.broadcasted_iota(jnp.int32, (C, C), 0)
    jj = lax.broadcasted_iota(jnp.int32, (C, C), 1)
    causal = ii >= jj
    strict = ii > jj
    eye = (ii == jj).astype(F32)
    n_sq = max(C.bit_length() - 2, 0)

    def conv_slab(col0):
        cols = slice(col0, col0 + DN_HD)
        acc = qkvbuf[CONV_HALO:CONV_HALO + tl, cols] * wconv_ref[CONV_W - 1:CONV_W, cols]
        for i in range(CONV_W - 1):
            r0 = CONV_HALO - (CONV_W - 1) + i
            acc = acc + qkvbuf[r0:r0 + tl, cols] * wconv_ref[i:i + 1, cols]
        return pad_rows(acc * jax.nn.sigmoid(acc))

    heads = range(DN_HEADS)
    chunks = range(n_chunks)
    probs = [(h, c) for h in heads for c in chunks]
    rows = [slice(c * C, (c + 1) * C) for c in chunks]
    q, k, v, gcs, bs, egs, kbs, grows = [], [], [], [], [], [], [], []
    for h in heads:
        qh = conv_slab(h * DN_HD)
        kh = conv_slab((DN_HEADS + h) * DN_HD)
        v.append(conv_slab((2 * DN_HEADS + h) * DN_HD))
        q.append(qh * lax.rsqrt(jnp.sum(qh * qh, -1, keepdims=True) + NORM_EPS) * (DN_HD ** -0.5))
        kh = kh * lax.rsqrt(jnp.sum(kh * kh, -1, keepdims=True) + NORM_EPS)
        k.append(kh)
        gcs.append(gcb[h])
        bs.append(betab[h])
        egs.append(jnp.exp(gcs[h]))
        kbs.append(kh * bs[h])
        grows.append(gt_ref[h][0:1, :])

    aq = {(h, c): _dot_nt(jnp.concatenate([kbs[h][rows[c]], q[h][rows[c]]], axis=0).astype(BF16),
                          k[h][rows[c]].astype(BF16)) for h, c in probs}
    decay = {(h, c): jnp.exp(jnp.where(causal, gcs[h][rows[c], 0:C] - grows[h][:, c * C:(c + 1) * C], -jnp.inf))
             for h, c in probs}
    A = {p: jnp.where(strict, aq[p][0:C] * decay[p], 0.0) for p in probs}
    qk = {p: (aq[p][C:2 * C] * decay[p]).astype(BF16) for p in probs}
    P = {p: eye - A[p] for p in probs}
    Q = {p: _dot(A[p].astype(BF16), A[p].astype(BF16)) for p in probs}
    for _ in range(n_sq - 1):
        pq = {p: _dot(jnp.concatenate([P[p], Q[p]], axis=0).astype(BF16), Q[p].astype(BF16)) for p in probs}
        P = {p: P[p] + pq[p][0:C] for p in probs}
        Q = {p: pq[p][C:2 * C] for p in probs}
    P = {p: P[p] + _dot(P[p].astype(BF16), Q[p].astype(BF16)) for p in probs}
    sol = {(h, c): _dot(P[(h, c)].astype(BF16),
                        jnp.concatenate([v[h][rows[c]] * bs[h][rows[c]], kbs[h][rows[c]] * egs[h][rows[c]]],
                                        axis=1).astype(BF16)) for h, c in probs}

    S = [sout_ref[0, h] for h in heads]
    outs = {}
    for c in chunks:
        r = rows[c]
        ws_qs = [_dot(jnp.concatenate([sol[(h, c)][:, DN_HD:2 * DN_HD], q[h][r] * egs[h][r]], axis=0).astype(BF16),
                      S[h].astype(BF16)) for h in heads]
        v_new = [(sol[(h, c)][:, 0:DN_HD] - ws_qs[h][0:C]).astype(BF16) for h in heads]
        glast = [gcs[h][r][C - 1:C, :] for h in heads]
        S = [S[h] * jnp.exp(glast[h]) + _dot_tn((k[h][r] * jnp.exp(glast[h] - gcs[h][r])).astype(BF16), v_new[h])
             for h in heads]
        for h in heads:
            outs[(h, c)] = ws_qs[h][C:2 * C] + _dot(qk[(h, c)], v_new[h])
    for h in heads:
        sout_ref[0, h] = S[h]
        o = outs[(h, 0)] if n_chunks == 1 else jnp.concatenate([outs[(h, c)] for c in chunks], axis=0)
        o = o[0:tl]
        o = o * lax.rsqrt(jnp.mean(o * o, -1, keepdims=True) + NORM_EPS) * wonorm_ref[...]
        zh = z_ref[:, h * DN_HD:(h + 1) * DN_HD]
        obuf[:, h * DN_HD:(h + 1) * DN_HD] = o * (zh * jax.nn.sigmoid(zh))

    if pos0 == 0:
        pos = l * tl + lax.broadcasted_iota(jnp.int32, (tl, 1), 0)
    for gi, win in enumerate(POOL_WINDOWS):
        cs = slice(gi * POOL_GD, (gi + 1) * POOL_GD)
        s = pbuf[POOL_HALO:POOL_HALO + tl, cs]
        for j in range(1, win):
            s = s + pbuf[POOL_HALO - j:POOL_HALO - j + tl, cs]
        if pos0 == 0:
            cnt = jnp.minimum(win, pos + 1).astype(F32)
        else:
            cnt = float(min(win, pos0 + 1))
        pooled = s / cnt - p_ref[:, cs]
        bb = _dot(pooled.astype(BF16), wpool_ref[gi]) * pscale_ref[:, cs]
        merged_ref[:, cs] = (jax.nn.sigmoid(ga_ref[:, cs]) * obuf[:, cs]
                             + jax.nn.sigmoid(gb_ref[:, cs]) * bb).astype(merged_ref.dtype)

    if tl >= POOL_HALO:
        qkvbuf[0:CONV_HALO, :] = qkvbuf[tl:tl + CONV_HALO, :]
        pbuf[0:POOL_HALO, :] = pbuf[tl:tl + POOL_HALO, :]


def _mixer(proj, B, L, conv_prev, pool_prev, s0, wconv, alog, dtb, wonorm, wpool, pscale, *, pos0):
    tl = min(256, L)
    nL = L // tl
    assert nL == 1 or tl >= POOL_HALO
    chunk = min(DN_CHUNK, tl)
    tlp = max(tl, MIN_CHUNK)
    has_state = s0 is not None
    row = lambda b, l: b * nL + l
    in_specs = [
        pl.BlockSpec((tl, QKV_DIM), lambda b, l: (row(b, l), 0)),
        pl.BlockSpec((tl, D_MODEL), lambda b, l: (row(b, l), 3)),
        pl.BlockSpec((tl, D_MODEL), lambda b, l: (row(b, l), 4)),
        pl.BlockSpec((tl, D_MODEL), lambda b, l: (row(b, l), 5)),
        pl.BlockSpec((tl, D_MODEL), lambda b, l: (row(b, l), 6)),
        pl.BlockSpec((tl, LANES), lambda b, l: (row(b, l), 7 * D_MODEL // LANES)),
    ]
    args = [proj] * 6
    if has_state:
        in_specs += [
            pl.BlockSpec((1, CONV_W - 1, QKV_DIM), lambda b, l: (b, 0, 0)),
            pl.BlockSpec((1, POOL_MAX - 1, D_MODEL), lambda b, l: (b, 0, 0)),
            pl.BlockSpec((1, DN_HEADS, DN_HD, DN_HD), lambda b, l: (b, 0, 0, 0)),
        ]
        args += [conv_prev, pool_prev, s0]
    const2 = lambda b, l: (0, 0)
    in_specs += [
        pl.BlockSpec((CONV_W, QKV_DIM), const2),
        pl.BlockSpec((1, LANES), const2),
        pl.BlockSpec((1, LANES), const2),
        pl.BlockSpec((1, DN_HD), const2),
        pl.BlockSpec((len(POOL_WINDOWS), POOL_GD, POOL_GD), lambda b, l: (0, 0, 0)),
        pl.BlockSpec((1, D_MODEL), const2),
    ]
    args += [wconv, alog, dtb, wonorm, wpool, pscale]
    return pl.pallas_call(
        functools.partial(_mixer_body, tl=tl, chunk=chunk, pos0=pos0, has_state=has_state),
        grid=(B, nL),
        in_specs=in_specs,
        out_specs=[
            pl.BlockSpec((tl, D_MODEL), lambda b, l: (row(b, l), 0)),
            pl.BlockSpec((1, DN_HEADS, DN_HD, DN_HD), lambda b, l: (b, 0, 0, 0)),
        ],
        out_shape=[
            jax.ShapeDtypeStruct((B * L, D_MODEL), BF16),
            jax.ShapeDtypeStruct((B, DN_HEADS, DN_HD, DN_HD), F32),
        ],
        scratch_shapes=[
            pltpu.VMEM((CONV_HALO + tl, QKV_DIM), F32),
            pltpu.VMEM((POOL_HALO + tl, D_MODEL), F32),
            pltpu.VMEM((tl, D_MODEL), F32),
            pltpu.VMEM((DN_HEADS, tlp, LANES), F32),
            pltpu.VMEM((DN_HEADS, tlp, LANES), F32),
            pltpu.VMEM((DN_HEADS, SUBLANES, tlp), F32),
        ],
        compiler_params=_cparams(2),
        name="mixer",
    )(*args)


def _proj_ln_q_body(a_ref, w_ref, r_ref, g_ref, b_ref, wq_ref, o_ref, q_ref):
    y = ALPHA * r_ref[...] + _dot(a_ref[...], w_ref[...])
    x1 = _layer_norm(y, g_ref[...], b_ref[...])
    o_ref[...] = x1
    q_ref[...] = _dot(x1.astype(BF16), wq_ref[...]).astype(q_ref.dtype)


def _proj_ln_route_body(a_ref, w_ref, r_ref, g_ref, b_ref, whi_ref, wlo_ref, o_ref, idx_ref, ew_ref):
    y = ALPHA * r_ref[...] + _dot(a_ref[...], w_ref[...])
    x2 = _layer_norm(y, g_ref[...], b_ref[...])
    o_ref[...] = x2
    idx_ref[...], ew_ref[...] = _route(x2, whi_ref[...], wlo_ref[...])


def _proj_ln(a, w, resid, g, b, tail, *, tm, route):
    T = a.shape[0]
    const = lambda i: (0, 0)
    rows = lambda i: (i, 0)
    in_specs = [
        pl.BlockSpec((tm, D_MODEL), rows),
        pl.BlockSpec((D_MODEL, D_MODEL), const),
        pl.BlockSpec((tm, D_MODEL), rows),
        pl.BlockSpec((1, D_MODEL), const),
        pl.BlockSpec((1, D_MODEL), const),
    ] + [pl.BlockSpec(t.shape, const) for t in tail]
    out_specs = [pl.BlockSpec((tm, D_MODEL), rows)]
    out_shape = [jax.ShapeDtypeStruct((T, D_MODEL), F32)]
    if route:
        cols = lambda i: (0, i)
        out_specs += [pl.BlockSpec((SUBLANES, tm), cols), pl.BlockSpec((SUBLANES, tm), cols)]
        out_shape += [jax.ShapeDtypeStruct((SUBLANES, T), jnp.int32), jax.ShapeDtypeStruct((SUBLANES, T), F32)]
    else:
        out_specs += [pl.BlockSpec((tm, D_MODEL), rows)]
        out_shape += [jax.ShapeDtypeStruct((T, D_MODEL), BF16)]
    return pl.pallas_call(
        _proj_ln_route_body if route else _proj_ln_q_body,
        grid=(T // tm,),
        in_specs=in_specs,
        out_specs=out_specs,
        out_shape=out_shape,
        compiler_params=_cparams(1),
        name="proj_ln_route" if route else "proj_ln_q",
    )(a, w, resid, g, b, *tail)


def _attn_body(q_ref, k_ref, v_ref, o_ref):
    scale = CA_HD ** -0.5
    for hh in range(CA_HEADS):
        cs = slice(hh * CA_HD, (hh + 1) * CA_HD)
        s = _dot_nt(q_ref[:, cs].astype(BF16), k_ref[0, :, cs].astype(BF16)) * scale
        m = jnp.max(s, -1, keepdims=True)
        p = jnp.exp(s - m)
        denom = jnp.sum(p, -1, keepdims=True)
        o = _dot(p.astype(BF16), v_ref[0, :, cs].astype(BF16)) / denom
        o_ref[:, cs] = o.astype(o_ref.dtype)


def _attention(q, mem_k, mem_v, B, L):
    tq = min(512, L)
    nq = L // tq
    M = mem_k.shape[1]
    kv_spec = pl.BlockSpec((1, M, D_MODEL), lambda b, i: (b, 0, 0))
    return pl.pallas_call(
        _attn_body,
        grid=(B, nq),
        in_specs=[pl.BlockSpec((tq, D_MODEL), lambda b, i: (b * nq + i, 0)), kv_spec, kv_spec],
        out_specs=pl.BlockSpec((tq, D_MODEL), lambda b, i: (b * nq + i, 0)),
        out_shape=jax.ShapeDtypeStruct((B * L, D_MODEL), BF16),
        compiler_params=_cparams(2),
        name="attention",
    )(q, mem_k, mem_v)


HALVES = CA_HD // LANES
KV_ROWS = CA_HEADS * HALVES


def _attn_rows_body(q_ref, xk_ref, xv_ref, o_ref, *, bb, L, M):
    scale = CA_HD ** -0.5
    R = M * KV_ROWS
    lane = lax.broadcasted_iota(jnp.int32, (L, R), 1)
    q_all = q_ref[...].astype(F32)
    for i in range(bb):
        q = q_all[i * L:(i + 1) * L, :]
        qm = jnp.concatenate([q[:, j * LANES:(j + 1) * LANES] for j in range(KV_ROWS)], axis=0).astype(BF16)
        xk = xk_ref[i * R:(i + 1) * R, :].astype(BF16)
        xv = xv_ref[i * R:(i + 1) * R, :].astype(BF16)
        g = _dot_nt(qm, xk)
        ps, invs = [], []
        for hh in range(CA_HEADS):
            s = None
            for half in range(HALVES):
                j = hh * HALVES + half
                part = g[j * L:(j + 1) * L]
                if half:
                    part = pltpu.roll(part, R - half * CA_HEADS, 1)
                s = part if s is None else s + part
            s = jnp.where((lane % KV_ROWS) == hh, s * scale, -jnp.inf)
            p = jnp.exp(s - jnp.max(s, -1, keepdims=True))
            inv = 1.0 / jnp.sum(p, -1, keepdims=True)
            for half in range(HALVES):
                ps.append(pltpu.roll(p, half * CA_HEADS, 1) if half else p)
                invs.append(inv)
        o = _dot(jnp.concatenate(ps, axis=0).astype(BF16), xv)
        for j in range(KV_ROWS):
            o_ref[i * L:(i + 1) * L, j * LANES:(j + 1) * LANES] = (o[j * L:(j + 1) * L] * invs[j]).astype(o_ref.dtype)


def _attention_rows(q, xk, xv, B, L, M):
    bb = 4 if B % 4 == 0 else 1
    R = M * KV_ROWS
    kv_spec = pl.BlockSpec((bb * R, LANES), lambda b: (b, 0))
    return pl.pallas_call(
        functools.partial(_attn_rows_body, bb=bb, L=L, M=M),
        grid=(B // bb,),
        in_specs=[pl.BlockSpec((bb * L, D_MODEL), lambda b: (b, 0)), kv_spec, kv_spec],
        out_specs=pl.BlockSpec((bb * L, D_MODEL), lambda b: (b, 0)),
        out_shape=jax.ShapeDtypeStruct((B * L, D_MODEL), BF16),
        compiler_params=_cparams(1),
        name="attention_rows",
    )(q, xk, xv)


def _route(x, wt_hi, wt_lo):
    assert MOE_GROUPS == SUBLANES and MOE_EPG == SUBLANES
    x_hi = x.astype(BF16)
    x_lo = (x - x_hi.astype(F32)).astype(BF16)
    logits = _dot_nt(wt_hi, x_hi) + (_dot_nt(wt_lo, x_hi) + _dot_nt(wt_hi, x_lo))
    tm = x.shape[0]
    sub = lax.broadcasted_iota(jnp.int32, (SUBLANES, tm), 0)
    neg = -jnp.inf
    g_log = logits[0:MOE_GROUPS]
    g_max = jnp.max(g_log, 0, keepdims=True)
    g_sel = jnp.min(jnp.where(g_log == g_max, sub, MOE_GROUPS), 0, keepdims=True)
    g_w = 1.0 / jnp.sum(jnp.exp(g_log - g_max), 0, keepdims=True)
    e_log = logits[MOE_GROUPS:MOE_GROUPS + MOE_EPG]
    for g in range(1, MOE_GROUPS):
        e_log = jnp.where(g_sel == g, logits[MOE_GROUPS + g * MOE_EPG:MOE_GROUPS + (g + 1) * MOE_EPG], e_log)
    v1 = jnp.max(e_log, 0, keepdims=True)
    i1 = jnp.min(jnp.where(e_log == v1, sub, MOE_EPG), 0, keepdims=True)
    e_log2 = jnp.where(sub == i1, neg, e_log)
    v2 = jnp.max(e_log2, 0, keepdims=True)
    i2 = jnp.min(jnp.where(e_log2 == v2, sub, MOE_EPG), 0, keepdims=True)
    t = jnp.exp(v2 - v1)
    w1 = g_w / (1.0 + t)
    w2 = g_w * t / (1.0 + t)
    first = sub == 0
    return g_sel * MOE_EPG + jnp.where(first, i1, i2), jnp.where(first, w1, w2)


def _plan_body(idx_ref, dest_ref, pend_ref, carry, pstart):
    ph = pl.program_id(0)
    i = pl.program_id(1)
    tm = idx_ref.shape[0]
    lane = lax.broadcasted_iota(jnp.int32, (tm, LANES), 1)
    oh0 = (lane == idx_ref[:, 0:1]).astype(F32)
    oh1 = (lane == idx_ref[:, 1:2]).astype(F32)
    both = oh0 + oh1

    @pl.when((ph == 0) & (i == 0))
    def _():
        carry[...] = jnp.zeros(carry.shape, F32)

    @pl.when(ph == 0)
    def _():
        carry[...] += jnp.sum(both, 0, keepdims=True)

    @pl.when((ph == 1) & (i == 0))
    def _():
        padded = jnp.floor((carry[...] + (MOE_BLOCK - 1)) * (1.0 / MOE_BLOCK)) * MOE_BLOCK
        a = lax.broadcasted_iota(jnp.int32, (LANES, LANES), 0)
        b = lax.broadcasted_iota(jnp.int32, (LANES, LANES), 1)
        upper = (a < b).astype(F32)
        ps = jnp.dot(jnp.broadcast_to(padded, (SUBLANES, LANES)), upper, preferred_element_type=F32,
                     precision=lax.Precision.HIGHEST)
        pstart[...] = ps[0:1]
        carry[...] = jnp.zeros(carry.shape, F32)

    @pl.when(ph == 1)
    def _():
        r = lax.broadcasted_iota(jnp.int32, (tm, tm), 0)
        c = lax.broadcasted_iota(jnp.int32, (tm, tm), 1)
        before = (r > c).astype(BF16)
        base = _dot(before, both.astype(BF16)) + carry[...] + pstart[...]
        d0 = jnp.sum(oh0 * base, -1, keepdims=True)
        d1 = jnp.sum(oh1 * base, -1, keepdims=True)
        first = lax.broadcasted_iota(jnp.int32, (tm, 2), 1) == 0
        dest_ref[...] = jnp.where(first, d0, d1).astype(jnp.int32)
        carry[...] += jnp.sum(both, 0, keepdims=True)
        padded_tot = jnp.floor((carry[...] + (MOE_BLOCK - 1)) * (1.0 / MOE_BLOCK)) * MOE_BLOCK
        pend_ref[...] = jnp.broadcast_to(pstart[...] + padded_tot, pend_ref.shape)


def _plan(idx, *, tm):
    T = idx.shape[0]
    return pl.pallas_call(
        _plan_body,
        grid=(2, T // tm),
        in_specs=[pl.BlockSpec((tm, 2), lambda p, i: (i, 0))],
        out_specs=[pl.BlockSpec((tm, 2), lambda p, i: (i * p, 0)), pl.BlockSpec((SUBLANES, LANES), lambda p, i: (0, 0))],
        out_shape=[jax.ShapeDtypeStruct((T, 2), jnp.int32), jax.ShapeDtypeStruct((SUBLANES, LANES), F32)],
        scratch_shapes=[pltpu.VMEM((1, LANES), F32), pltpu.VMEM((1, LANES), F32)],
        compiler_params=_cparams(2),
        name="plan",
    )(idx)


def _row_copy(src_ref, s, dst_ref, d, sem):
    return pltpu.make_async_copy(src_ref.at[pl.ds(s, 1), :], dst_ref.at[pl.ds(d, 1), :], sem)


def _dispatch_body(dest_ref, x_ref, xs_in_ref, xs_ref, sem):
    del xs_in_ref
    tm = x_ref.shape[0]

    def start(t, c):
        _row_copy(x_ref, t, xs_ref, dest_ref[0, 0, 2 * t], sem).start()
        _row_copy(x_ref, t, xs_ref, dest_ref[0, 0, 2 * t + 1], sem).start()
        return c

    lax.fori_loop(0, tm, start, 0, unroll=ROW_DMA_UNROLL)
    for _ in range(2):
        pltpu.make_async_copy(x_ref, xs_ref.at[pl.ds(0, tm), :], sem).wait()


def _dispatch(dest, x, xs, *, tm):
    T = x.shape[0]
    nt = T // tm
    return pl.pallas_call(
        _dispatch_body,
        grid=(nt,),
        in_specs=[
            pl.BlockSpec((1, 1, 2 * tm), lambda i: (i, 0, 0), memory_space=pltpu.SMEM),
            pl.BlockSpec((tm, D_MODEL), lambda i: (i, 0)),
            pl.BlockSpec(memory_space=pl.ANY),
        ],
        out_specs=pl.BlockSpec(memory_space=pl.ANY),
        out_shape=jax.ShapeDtypeStruct(xs.shape, xs.dtype),
        scratch_shapes=[pltpu.SemaphoreType.DMA(())],
        input_output_aliases={2: 0},
        compiler_params=_cparams(1),
        name="dispatch",
    )(dest.reshape(nt, 1, 2 * tm), x, xs)


def _combine_body(dest_ref, x_ref, ew_ref, g_ref, b_ref, rows_ref, o_ref, gbuf, sem):
    tm = x_ref.shape[0]

    def start(t, c):
        _row_copy(rows_ref, dest_ref[0, 0, 2 * t], gbuf.at[0], t, sem).start()
        _row_copy(rows_ref, dest_ref[0, 0, 2 * t + 1], gbuf.at[1], t, sem).start()
        return c

    lax.fori_loop(0, tm, start, 0, unroll=ROW_DMA_UNROLL)
    for half in range(2):
        pltpu.make_async_copy(rows_ref.at[pl.ds(0, tm), :], gbuf.at[half], sem).wait()
    y = ew_ref[:, 0:1] * gbuf[0] + ew_ref[:, 1:2] * gbuf[1]
    o_ref[...] = _layer_norm(ALPHA * x_ref[...] + y, g_ref[...], b_ref[...])


def _combine(dest, x, ew, g, b, rows, *, tm):
    T = x.shape[0]
    nt = T // tm
    const = lambda i: (0, 0)
    return pl.pallas_call(
        _combine_body,
        grid=(nt,),
        in_specs=[
            pl.BlockSpec((1, 1, 2 * tm), lambda i: (i, 0, 0), memory_space=pltpu.SMEM),
            pl.BlockSpec((tm, D_MODEL), lambda i: (i, 0)),
            pl.BlockSpec((tm, 2), lambda i: (i, 0)),
            pl.BlockSpec((1, D_MODEL), const),
            pl.BlockSpec((1, D_MODEL), const),
            pl.BlockSpec(memory_space=pl.ANY),
        ],
        out_specs=pl.BlockSpec((tm, D_MODEL), lambda i: (i, 0)),
        out_shape=jax.ShapeDtypeStruct((T, D_MODEL), F32),
        scratch_shapes=[pltpu.VMEM((2, tm, D_MODEL), F32), pltpu.SemaphoreType.DMA(())],
        compiler_params=_cparams(1),
        name="combine",
    )(dest.reshape(nt, 1, 2 * tm), x, ew, g, b, rows)


def _experts_body(be_ref, nu_ref, xs_ref, wg_ref, wu_ref, wd_ref, o_ref):
    del be_ref
    i = pl.program_id(0)

    @pl.when(i < nu_ref[0])
    def _():
        xb = xs_ref[...].astype(BF16)
        gate = _dot(xb, wg_ref[0].astype(BF16))
        up = _dot(xb, wu_ref[0].astype(BF16))
        hid = gate * jax.nn.sigmoid(gate) * up
        o_ref[...] = _dot(hid.astype(BF16), wd_ref[0].astype(BF16))

    @pl.when(i >= nu_ref[0])
    def _():
        o_ref[...] = jnp.zeros(o_ref.shape, F32)


def _experts(block_expert, n_used, xs, w_gate, w_up, w_down):
    R = xs.shape[0]
    nb = R // MOE_BLOCK
    grid_spec = pltpu.PrefetchScalarGridSpec(
        num_scalar_prefetch=2,
        grid=(nb,),
        in_specs=[
            pl.BlockSpec((MOE_BLOCK, D_MODEL), lambda i, be, nu: (i, 0)),
            pl.BlockSpec((1, D_MODEL, MOE_FF), lambda i, be, nu: (be[i], 0, 0)),
            pl.BlockSpec((1, D_MODEL, MOE_FF), lambda i, be, nu: (be[i], 0, 0)),
            pl.BlockSpec((1, MOE_FF, D_MODEL), lambda i, be, nu: (be[i], 0, 0)),
        ],
        out_specs=pl.BlockSpec((MOE_BLOCK, D_MODEL), lambda i, be, nu: (i, 0)),
    )
    return pl.pallas_call(
        _experts_body,
        grid_spec=grid_spec,
        out_shape=jax.ShapeDtypeStruct((R, D_MODEL), F32),
        compiler_params=_cparams(1),
        name="experts",
    )(block_expert, n_used, xs, w_gate, w_up, w_down)


def _row_tile(T, pref):
    t = pref
    while T % t:
        t //= 2
    return t


def _group_to_x2(x, mem_k, mem_v, conv_prev, pool_prev, s0, pos0, W):
    B, L, _ = x.shape
    T = B * L
    xf = x.reshape(T, D_MODEL)
    tm = _row_tile(T, 512)
    proj = _mm(xf, W["w_in"], tm=tm, tn=N_PROJ // 3)
    merged, s_new = _mixer(proj, B, L, conv_prev, pool_prev, s0, W["w_conv"], W["a_log"], W["dt_bias"], W["w_onorm"],
                           W["w_pool"], W["pool_scale"], pos0=pos0)
    x1, q = _proj_ln(merged, W["w_out"], xf, W["ln1_g"], W["ln1_b"], (W["w_cq"],), tm=tm, route=False)
    if mem_k.ndim == 3:
        att = _attention(q, mem_k, mem_v, B, L)
    else:
        M = mem_k.shape[1]

        def rows_view(m):
            m = m.reshape(B, M, CA_HEADS, HALVES, LANES)
            return jnp.swapaxes(m, 2, 3).reshape(B * M * KV_ROWS, LANES)

        att = _attention_rows(q, rows_view(mem_k), rows_view(mem_v), B, L, M)
    x2, idx, ew = _proj_ln(att, W["w_co"], x1, W["ln2_g"], W["ln2_b"], (W["w_r_hi"], W["w_r_lo"]), tm=tm, route=True)
    idx, ew = idx[0:2].T, ew[0:2].T
    proj3 = proj.reshape(B, L, N_PROJ)
    qkv_raw = proj3[:, max(L - (CONV_W - 1), 0):, 0:QKV_DIM]
    p_raw = proj3[:, max(L - (POOL_MAX - 1), 0):, 4 * D_MODEL:5 * D_MODEL]
    if conv_prev is None:
        conv_prev = jnp.zeros((B, CONV_W - 1, QKV_DIM), F32)
        pool_prev = jnp.zeros((B, POOL_MAX - 1, D_MODEL), F32)
    conv_new = jnp.concatenate([conv_prev, qkv_raw], axis=1)[:, -(CONV_W - 1):]
    pool_new = jnp.concatenate([pool_prev, p_raw], axis=1)[:, -(POOL_MAX - 1):]
    return x2, idx, ew, s_new, conv_new, pool_new


def kernel(x_prompt, x_sample, cache_mem_k, cache_mem_v, state_delta, state_conv, state_pool, mem_prompt, w_in, w_conv, a_log, dt_bias, w_onorm, w_pool, pool_scale, w_out, ln1_g, ln1_b, w_cq, w_ck, w_cv, w_co, ln2_g, ln2_b, w_router_group, w_router_expert, w_gate, w_up, w_down, ln3_g, ln3_b):
    Bp, Lp, _ = x_prompt.shape
    Bs, Ls, _ = x_sample.shape
    Tp, Ts = Bp * Lp, Bs * Ls
    lyr = 0

    def pad_lanes(v):
        return jnp.pad(v.astype(F32), (0, LANES - v.shape[0])).reshape(1, LANES)

    o1, o2, o3, o4 = QKV_DIM, QKV_DIM + D_MODEL, QKV_DIM + D_MODEL + 2 * DN_HEADS, QKV_DIM + 2 * D_MODEL + 2 * DN_HEADS
    wi = w_in[lyr]
    w_in_r = jnp.concatenate(
        [wi[:, :o2], wi[:, o3:], wi[:, o2:o3], jnp.zeros((D_MODEL, LANES - 2 * DN_HEADS), F32)], axis=1).astype(BF16)
    del o1, o4
    w_r = jnp.concatenate([w_router_group[lyr], w_router_expert[lyr],
                           jnp.zeros((D_MODEL, LANES - MOE_GROUPS - MOE_EXPERTS), F32)], axis=1).T
    w_r_hi = w_r.astype(BF16)
    W = {
        "w_r_hi": w_r_hi,
        "w_r_lo": (w_r - w_r_hi.astype(F32)).astype(BF16),
        "w_in": w_in_r,
        "w_conv": w_conv[lyr],
        "a_log": pad_lanes(a_log[lyr]),
        "dt_bias": pad_lanes(dt_bias[lyr]),
        "w_onorm": w_onorm[lyr].reshape(1, DN_HD),
        "w_pool": w_pool[lyr].astype(BF16),
        "pool_scale": pool_scale[lyr].reshape(1, D_MODEL),
        "w_out": w_out[lyr].astype(BF16),
        "ln1_g": ln1_g[lyr].reshape(1, D_MODEL), "ln1_b": ln1_b[lyr].reshape(1, D_MODEL),
        "w_cq": w_cq[lyr].astype(BF16),
        "w_co": w_co[lyr].astype(BF16),
        "ln2_g": ln2_g[lyr].reshape(1, D_MODEL), "ln2_b": ln2_b[lyr].reshape(1, D_MODEL),
    }

    M = mem_prompt.shape[1]
    memf = mem_prompt.reshape(Bp * M, D_MODEL)
    tmm = _row_tile(Bp * M, 512)
    mk = _mm(memf, w_ck[lyr].astype(BF16), tm=tmm, tn=D_MODEL).reshape(Bp, M, D_MODEL)
    mv = _mm(memf, w_cv[lyr].astype(BF16), tm=tmm, tn=D_MODEL).reshape(Bp, M, D_MODEL)
    x2_p, idx_p, ew_p, d_p, c_p, p_p = _group_to_x2(x_prompt, mk, mv, None, None, None, 0, W)
    mk = mk.reshape(Bp, M, CA_HEADS, CA_HD)
    mv = mv.reshape(Bp, M, CA_HEADS, CA_HD)
    x2_s, idx_s, ew_s, d_s, c_s, p_s = _group_to_x2(x_sample, cache_mem_k[lyr], cache_mem_v[lyr], state_conv[lyr],
                                       state_pool[lyr], state_delta[lyr], PAST_LEN, W)

    tp, ts = _row_tile(Tp, 256), _row_tile(Ts, 256)
    T = Tp + Ts
    dest, pend = _plan(jnp.concatenate([idx_p, idx_s], axis=0), tm=_row_tile(T, 1024))
    n_blocks = (2 * T + MOE_EXPERTS * (MOE_BLOCK - 1) + MOE_BLOCK - 1) // MOE_BLOCK
    pend_e = pend[0, :MOE_EXPERTS].astype(jnp.int32)
    block_start = jnp.arange(n_blocks, dtype=jnp.int32) * MOE_BLOCK
    block_expert = jnp.minimum(jnp.sum(block_start[:, None] >= pend_e[None, :], axis=1), MOE_EXPERTS - 1).astype(jnp.int32)
    n_used = (pend_e[MOE_EXPERTS - 1:] // MOE_BLOCK).astype(jnp.int32)
    xs = jnp.zeros((n_blocks * MOE_BLOCK, D_MODEL), F32)
    xs = _dispatch(dest[:Tp], x2_p, xs, tm=tp)
    xs = _dispatch(dest[Tp:], x2_s, xs, tm=ts)
    rows = _experts(block_expert, n_used, xs, w_gate[lyr], w_up[lyr], w_down[lyr])
    g3, b3 = ln3_g[lyr].reshape(1, D_MODEL), ln3_b[lyr].reshape(1, D_MODEL)
    y_p = _combine(dest[:Tp], x2_p, ew_p, g3, b3, rows, tm=tp).reshape(Bp, Lp, D_MODEL)
    y_s = _combine(dest[Tp:], x2_s, ew_s, g3, b3, rows, tm=ts).reshape(Bs, Ls, D_MODEL)

    return (y_p, y_s, d_p[None], c_p[None], p_p[None], mk[None], mv[None], d_s[None], c_s[None], p_s[None])
```

```python
import functools

import jax
import jax.numpy as jnp
from jax import lax
from jax.experimental import pallas as pl
from jax.experimental.pallas import tpu as pltpu

F32 = jnp.float32
BF16 = jnp.bfloat16

D_MODEL = 1024
DN_HEADS = 8
DN_HD = 128
QKV_DIM = 3 * DN_HEADS * DN_HD
CONV_W = 4
DN_CHUNK = 64
POOL_WINDOWS = (2, 4, 8, 16)
POOL_GD = D_MODEL // len(POOL_WINDOWS)
POOL_MAX = 16
CA_HEADS = 4
CA_HD = D_MODEL // CA_HEADS
MOE_GROUPS = 8
MOE_EPG = 8
MOE_EXPERTS = MOE_GROUPS * MOE_EPG
MOE_FF = D_MODEL // 4
MOE_BLOCK = 256
PAST_LEN = 16384
LN_EPS = 1e-5
NORM_EPS = 1e-6
ALPHA = 2.0 ** 0.25

LANES = 128
SUBLANES = 8
CONV_HALO = SUBLANES
POOL_HALO = POOL_MAX
MIN_CHUNK = 16
N_PROJ = 7 * D_MODEL + LANES
VMEM_LIMIT = 48 * 1024 * 1024
MIXER_VMEM_LIMIT = 56 * 1024 * 1024
ROW_DMA_UNROLL = 8
FUSE_MIN_SEQ = 128


def _cparams(n_axes, vmem=VMEM_LIMIT):
    return pltpu.CompilerParams(dimension_semantics=("arbitrary",) * n_axes, vmem_limit_bytes=vmem)


def _layer_norm(x, g, b):
    mu = jnp.mean(x, -1, keepdims=True)
    xc = x - mu
    var = jnp.mean(xc * xc, -1, keepdims=True)
    return xc * lax.rsqrt(var + LN_EPS) * g + b


def _softplus(x):
    return jnp.maximum(x, 0.0) + jnp.log1p(jnp.exp(-jnp.abs(x)))


def _dot(a, b):
    return jnp.dot(a, b, preferred_element_type=F32)


def _dot_nt(a, b):
    return lax.dot_general(a, b, (((1,), (1,)), ((), ())), preferred_element_type=F32)


def _dot_tn(a, b):
    return lax.dot_general(a, b, (((0,), (0,)), ((), ())), preferred_element_type=F32)


def _mm_body(x_ref, w_ref, o_ref):
    o_ref[...] = _dot(x_ref[...].astype(BF16), w_ref[...]).astype(o_ref.dtype)


def _mm(x, w, *, tm, tn, out_dtype=F32):
    T, K = x.shape
    N = w.shape[1]
    return pl.pallas_call(
        _mm_body,
        grid=(N // tn, T // tm),
        in_specs=[pl.BlockSpec((tm, K), lambda j, i: (i, 0)), pl.BlockSpec((K, tn), lambda j, i: (0, j))],
        out_specs=pl.BlockSpec((tm, tn), lambda j, i: (i, j)),
        out_shape=jax.ShapeDtypeStruct((T, N), out_dtype),
        compiler_params=_cparams(2),
        name="mm",
    )(x, w)


def _mixer_body(*refs, tl, chunk, pos0, has_state, fused):
    refs = list(refs)
    take = lambda n: [refs.pop(0) for _ in range(n)]
    if fused:
        x_ref, win_hbm = take(2)
    else:
        qkv_ref, z_ref, p_ref, ga_ref, gb_ref, ab_ref = take(6)
    if has_state:
        convp_ref, poolp_ref, s0_ref = take(3)
    wconv_ref, alog_ref, dtb_ref, wonorm_ref, wpool_ref, pscale_ref = take(6)
    merged_ref, sout_ref, convo_ref, poolo_ref = take(4)
    qkvbuf, pbuf, obuf, gcb, betab, gt_ref = take(6)
    if fused:
        z_ref, ga_ref, gb_ref, win_vmem, win_sem = take(5)
    b = pl.program_id(0)
    l = pl.program_id(1)
    tlp = max(tl, MIN_CHUNK)
    C = max(chunk, MIN_CHUNK)
    n_chunks = tlp // C

    @pl.when(l == 0)
    def _init():
        qkvbuf[0:CONV_HALO, :] = jnp.zeros((CONV_HALO, QKV_DIM), F32)
        pbuf[0:POOL_HALO, :] = jnp.zeros((POOL_HALO, D_MODEL), F32)
        if has_state:
            qkvbuf[CONV_HALO - (CONV_W - 1):CONV_HALO, :] = convp_ref[0]
            pbuf[POOL_HALO - (POOL_MAX - 1):POOL_HALO, :] = poolp_ref[0]
            sout_ref[...] = s0_ref[...]
        else:
            sout_ref[...] = jnp.zeros(sout_ref.shape, F32)

    if fused:
        @pl.when((b == 0) & (l == 0))
        def _load_weight():
            cp = pltpu.make_async_copy(win_hbm, win_vmem, win_sem)
            cp.start()
            cp.wait()

        xb = x_ref[...].astype(BF16)

        def proj_cols(c0, width):
            return _dot(xb, win_vmem[:, c0:c0 + width])

        ab = proj_cols(7 * D_MODEL, LANES)
        for c in range(QKV_DIM // D_MODEL):
            qkvbuf[CONV_HALO:CONV_HALO + tl, c * D_MODEL:(c + 1) * D_MODEL] = proj_cols(c * D_MODEL, D_MODEL)
        z_ref[...] = proj_cols(3 * D_MODEL, D_MODEL)
        pbuf[POOL_HALO:POOL_HALO + tl, :] = proj_cols(4 * D_MODEL, D_MODEL)
        ga_ref[...] = proj_cols(5 * D_MODEL, D_MODEL)
        gb_ref[...] = proj_cols(6 * D_MODEL, D_MODEL)
    else:
        ab = ab_ref[...]
        qkvbuf[CONV_HALO:CONV_HALO + tl, :] = qkv_ref[...]
        pbuf[POOL_HALO:POOL_HALO + tl, :] = p_ref[...]

    def pad_rows(x):
        if tlp == tl:
            return x
        return jnp.concatenate([x, jnp.zeros((tlp - tl, x.shape[1]), F32)], axis=0)

    g_all = pad_rows(-jnp.exp(alog_ref[...]) * _softplus(ab + dtb_ref[...]))
    beta_all = pad_rows(jax.nn.sigmoid(ab))
    ri = lax.broadcasted_iota(jnp.int32, (tlp, tlp), 0)
    ci = lax.broadcasted_iota(jnp.int32, (tlp, tlp), 1)
    cum_mat = ((ri >= ci) & ((ri // C) == (ci // C))).astype(F32)
    gcum = jnp.dot(cum_mat, g_all, preferred_element_type=F32, precision=lax.Precision.HIGHEST)
    gcum_t = gcum.T
    for h in range(DN_HEADS):
        gt_ref[h] = jnp.broadcast_to(gcum_t[h:h + 1, :], (SUBLANES, tlp))
        gcb[h] = jnp.broadcast_to(gcum[:, h:h + 1], (tlp, LANES))
        betab[h] = jnp.broadcast_to(beta_all[:, DN_HEADS + h:DN_HEADS + h + 1], (tlp, LANES))

    ii = lax.broadcasted_iota(jnp.int32, (C, C), 0)
    jj = lax.broadcasted_iota(jnp.int32, (C, C), 1)
    causal = ii >= jj
    strict = ii > jj
    eye = (ii == jj).astype(F32)
    n_sq = max(C.bit_length() - 2, 0)

    def conv_slab(col0):
        cols = slice(col0, col0 + DN_HD)
        acc = qkvbuf[CONV_HALO:CONV_HALO + tl, cols] * wconv_ref[CONV_W - 1:CONV_W, cols]
        for i in range(CONV_W - 1):
            r0 = CONV_HALO - (CONV_W - 1) + i
            acc = acc + qkvbuf[r0:r0 + tl, cols] * wconv_ref[i:i + 1, cols]
        return pad_rows(acc * jax.nn.sigmoid(acc))

    heads = range(DN_HEADS)
    chunks = range(n_chunks)
    probs = [(h, c) for h in heads for c in chunks]
    rows = [slice(c * C, (c + 1) * C) for c in chunks]
    q, k, v, gcs, bs, egs, kbs, grows = [], [], [], [], [], [], [], []
    for h in heads:
        qh = conv_slab(h * DN_HD)
        kh = conv_slab((DN_HEADS + h) * DN_HD)
        v.append(conv_slab((2 * DN_HEADS + h) * DN_HD))
        q.append(qh * lax.rsqrt(jnp.sum(qh * qh, -1, keepdims=True) + NORM_EPS) * (DN_HD ** -0.5))
        kh = kh * lax.rsqrt(jnp.sum(kh * kh, -1, keepdims=True) + NORM_EPS)
        k.append(kh)
        gcs.append(gcb[h])
        bs.append(betab[h])
        egs.append(jnp.exp(gcs[h]))
        kbs.append(kh * bs[h])
        grows.append(gt_ref[h][0:1, :])

    aq = {(h, c): _dot_nt(jnp.concatenate([kbs[h][rows[c]], q[h][rows[c]]], axis=0).astype(BF16),
                          k[h][rows[c]].astype(BF16)) for h, c in probs}
    decay = {(h, c): jnp.exp(jnp.where(causal, gcs[h][rows[c], 0:C] - grows[h][:, c * C:(c + 1) * C], -jnp.inf))
             for h, c in probs}
    A = {p: jnp.where(strict, aq[p][0:C] * decay[p], 0.0) for p in probs}
    qk = {p: (aq[p][C:2 * C] * decay[p]).astype(BF16) for p in probs}
    P = {p: eye - A[p] for p in probs}
    Q = {p: _dot(A[p].astype(BF16), A[p].astype(BF16)) for p in probs}
    for _ in range(n_sq - 1):
        pq = {p: _dot(jnp.concatenate([P[p], Q[p]], axis=0).astype(BF16), Q[p].astype(BF16)) for p in probs}
        P = {p: P[p] + pq[p][0:C] for p in probs}
        Q = {p: pq[p][C:2 * C] for p in probs}
    P = {p: P[p] + _dot(P[p].astype(BF16), Q[p].astype(BF16)) for p in probs}
    sol = {(h, c): _dot(P[(h, c)].astype(BF16),
                        jnp.concatenate([v[h][rows[c]] * bs[h][rows[c]], kbs[h][rows[c]] * egs[h][rows[c]]],
                                        axis=1).astype(BF16)) for h, c in probs}

    S = [sout_ref[0, h] for h in heads]
    outs = {}
    for c in chunks:
        r = rows[c]
        ws_qs = [_dot(jnp.concatenate([sol[(h, c)][:, DN_HD:2 * DN_HD], q[h][r] * egs[h][r]], axis=0).astype(BF16),
                      S[h].astype(BF16)) for h in heads]
        v_new = [(sol[(h, c)][:, 0:DN_HD] - ws_qs[h][0:C]).astype(BF16) for h in heads]
        glast = [gcs[h][r][C - 1:C, :] for h in heads]
        S = [S[h] * jnp.exp(glast[h]) + _dot_tn((k[h][r] * jnp.exp(glast[h] - gcs[h][r])).astype(BF16), v_new[h])
             for h in heads]
        for h in heads:
            outs[(h, c)] = ws_qs[h][C:2 * C] + _dot(qk[(h, c)], v_new[h])
    for h in heads:
        sout_ref[0, h] = S[h]
        o = outs[(h, 0)] if n_chunks == 1 else jnp.concatenate([outs[(h, c)] for c in chunks], axis=0)
        o = o[0:tl]
        o = o * lax.rsqrt(jnp.mean(o * o, -1, keepdims=True) + NORM_EPS) * wonorm_ref[...]
        zh = z_ref[:, h * DN_HD:(h + 1) * DN_HD]
        obuf[:, h * DN_HD:(h + 1) * DN_HD] = o * (zh * jax.nn.sigmoid(zh))

    if pos0 == 0:
        pos = l * tl + lax.broadcasted_iota(jnp.int32, (tl, 1), 0)
    for gi, win in enumerate(POOL_WINDOWS):
        cs = slice(gi * POOL_GD, (gi + 1) * POOL_GD)
        s = pbuf[POOL_HALO:POOL_HALO + tl, cs]
        for j in range(1, win):
            s = s + pbuf[POOL_HALO - j:POOL_HALO - j + tl, cs]
        if pos0 == 0:
            cnt = jnp.minimum(win, pos + 1).astype(F32)
        else:
            cnt = float(min(win, pos0 + 1))
        pooled = s / cnt - pbuf[POOL_HALO:POOL_HALO + tl, cs]
        bb = _dot(pooled.astype(BF16), wpool_ref[gi]) * pscale_ref[:, cs]
        merged_ref[:, cs] = (jax.nn.sigmoid(ga_ref[:, cs]) * obuf[:, cs]
                             + jax.nn.sigmoid(gb_ref[:, cs]) * bb).astype(merged_ref.dtype)

    conv_tail = qkvbuf[tl:tl + CONV_HALO, :]
    pool_tail = pbuf[tl:tl + POOL_HALO, :]
    convo_ref[0] = conv_tail
    poolo_ref[0] = pool_tail
    if tl >= POOL_HALO:
        qkvbuf[0:CONV_HALO, :] = conv_tail
        pbuf[0:POOL_HALO, :] = pool_tail


def _mixer(src, B, L, conv_prev, pool_prev, s0, wconv, alog, dtb, wonorm, wpool, pscale, *, pos0, w_in=None):
    tl = min(256, L)
    nL = L // tl
    assert nL == 1 or tl >= POOL_HALO
    chunk = min(DN_CHUNK, tl)
    tlp = max(tl, MIN_CHUNK)
    has_state = s0 is not None
    fused = w_in is not None
    row = lambda b, l: b * nL + l
    if fused:
        in_specs = [pl.BlockSpec((tl, D_MODEL), lambda b, l: (row(b, l), 0)), pl.BlockSpec(memory_space=pl.ANY)]
        args = [src, w_in]
    else:
        in_specs = [
            pl.BlockSpec((tl, QKV_DIM), lambda b, l: (row(b, l), 0)),
            pl.BlockSpec((tl, D_MODEL), lambda b, l: (row(b, l), 3)),
            pl.BlockSpec((tl, D_MODEL), lambda b, l: (row(b, l), 4)),
            pl.BlockSpec((tl, D_MODEL), lambda b, l: (row(b, l), 5)),
            pl.BlockSpec((tl, D_MODEL), lambda b, l: (row(b, l), 6)),
            pl.BlockSpec((tl, LANES), lambda b, l: (row(b, l), 7 * D_MODEL // LANES)),
        ]
        args = [src] * 6
    if has_state:
        in_specs += [
            pl.BlockSpec((1, CONV_W - 1, QKV_DIM), lambda b, l: (b, 0, 0)),
            pl.BlockSpec((1, POOL_MAX - 1, D_MODEL), lambda b, l: (b, 0, 0)),
            pl.BlockSpec((1, DN_HEADS, DN_HD, DN_HD), lambda b, l: (b, 0, 0, 0)),
        ]
        args += [conv_prev, pool_prev, s0]
    const2 = lambda b, l: (0, 0)
    in_specs += [
        pl.BlockSpec((CONV_W, QKV_DIM), const2),
        pl.BlockSpec((1, LANES), const2),
        pl.BlockSpec((1, LANES), const2),
        pl.BlockSpec((1, DN_HD), const2),
        pl.BlockSpec((len(POOL_WINDOWS), POOL_GD, POOL_GD), lambda b, l: (0, 0, 0)),
        pl.BlockSpec((1, D_MODEL), const2),
    ]
    args += [wconv, alog, dtb, wonorm, wpool, pscale]
    scratch = [
        pltpu.VMEM((CONV_HALO + tl, QKV_DIM), F32),
        pltpu.VMEM((POOL_HALO + tl, D_MODEL), F32),
        pltpu.VMEM((tl, D_MODEL), F32),
        pltpu.VMEM((DN_HEADS, tlp, LANES), F32),
        pltpu.VMEM((DN_HEADS, tlp, LANES), F32),
        pltpu.VMEM((DN_HEADS, SUBLANES, tlp), F32),
    ]
    if fused:
        scratch += [pltpu.VMEM((tl, D_MODEL), F32)] * 3
        scratch += [pltpu.VMEM(w_in.shape, w_in.dtype), pltpu.SemaphoreType.DMA(())]
    return pl.pallas_call(
        functools.partial(_mixer_body, tl=tl, chunk=chunk, pos0=pos0, has_state=has_state, fused=fused),
        grid=(B, nL),
        in_specs=in_specs,
        out_specs=[
            pl.BlockSpec((tl, D_MODEL), lambda b, l: (row(b, l), 0)),
            pl.BlockSpec((1, DN_HEADS, DN_HD, DN_HD), lambda b, l: (b, 0, 0, 0)),
            pl.BlockSpec((1, CONV_HALO, QKV_DIM), lambda b, l: (b, 0, 0)),
            pl.BlockSpec((1, POOL_HALO, D_MODEL), lambda b, l: (b, 0, 0)),
        ],
        out_shape=[
            jax.ShapeDtypeStruct((B * L, D_MODEL), BF16),
            jax.ShapeDtypeStruct((B, DN_HEADS, DN_HD, DN_HD), F32),
            jax.ShapeDtypeStruct((B, CONV_HALO, QKV_DIM), F32),
            jax.ShapeDtypeStruct((B, POOL_HALO, D_MODEL), F32),
        ],
        scratch_shapes=scratch,
        compiler_params=_cparams(2, vmem=MIXER_VMEM_LIMIT),
        name="mixer",
    )(*args)


def _proj_ln_q_body(a_ref, w_ref, r_ref, g_ref, b_ref, wq_ref, o_ref, q_ref):
    y = ALPHA * r_ref[...] + _dot(a_ref[...], w_ref[...])
    x1 = _layer_norm(y, g_ref[...], b_ref[...])
    o_ref[...] = x1
    q_ref[...] = _dot(x1.astype(BF16), wq_ref[...]).astype(q_ref.dtype)


def _proj_ln_route_body(a_ref, w_ref, r_ref, g_ref, b_ref, whi_ref, wlo_ref, o_ref, idx_ref, ew_ref):
    y = ALPHA * r_ref[...] + _dot(a_ref[...], w_ref[...])
    x2 = _layer_norm(y, g_ref[...], b_ref[...])
    o_ref[...] = x2
    idx_ref[...], ew_ref[...] = _route(x2, whi_ref[...], wlo_ref[...])


def _proj_ln(a, w, resid, g, b, tail, *, tm, route):
    T = a.shape[0]
    const = lambda i: (0, 0)
    rows = lambda i: (i, 0)
    in_specs = [
        pl.BlockSpec((tm, D_MODEL), rows),
        pl.BlockSpec((D_MODEL, D_MODEL), const),
        pl.BlockSpec((tm, D_MODEL), rows),
        pl.BlockSpec((1, D_MODEL), const),
        pl.BlockSpec((1, D_MODEL), const),
    ] + [pl.BlockSpec(t.shape, const) for t in tail]
    out_specs = [pl.BlockSpec((tm, D_MODEL), rows)]
    out_shape = [jax.ShapeDtypeStruct((T, D_MODEL), F32)]
    if route:
        cols = lambda i: (0, i)
        out_specs += [pl.BlockSpec((SUBLANES, tm), cols), pl.BlockSpec((SUBLANES, tm), cols)]
        out_shape += [jax.ShapeDtypeStruct((SUBLANES, T), jnp.int32), jax.ShapeDtypeStruct((SUBLANES, T), F32)]
    else:
        out_specs += [pl.BlockSpec((tm, D_MODEL), rows)]
        out_shape += [jax.ShapeDtypeStruct((T, D_MODEL), BF16)]
    return pl.pallas_call(
        _proj_ln_route_body if route else _proj_ln_q_body,
        grid=(T // tm,),
        in_specs=in_specs,
        out_specs=out_specs,
        out_shape=out_shape,
        compiler_params=_cparams(1),
        name="proj_ln_route" if route else "proj_ln_q",
    )(a, w, resid, g, b, *tail)


def _attn_body(q_ref, k_ref, v_ref, o_ref):
    scale = CA_HD ** -0.5
    for hh in range(CA_HEADS):
        cs = slice(hh * CA_HD, (hh + 1) * CA_HD)
        s = _dot_nt(q_ref[:, cs].astype(BF16), k_ref[0, :, cs].astype(BF16)) * scale
        m = jnp.max(s, -1, keepdims=True)
        p = jnp.exp(s - m)
        denom = jnp.sum(p, -1, keepdims=True)
        o = _dot(p.astype(BF16), v_ref[0, :, cs].astype(BF16)) / denom
        o_ref[:, cs] = o.astype(o_ref.dtype)


def _attention(q, mem_k, mem_v, B, L):
    tq = min(512, L)
    nq = L // tq
    M = mem_k.shape[1]
    kv_spec = pl.BlockSpec((1, M, D_MODEL), lambda b, i: (b, 0, 0))
    return pl.pallas_call(
        _attn_body,
        grid=(B, nq),
        in_specs=[pl.BlockSpec((tq, D_MODEL), lambda b, i: (b * nq + i, 0)), kv_spec, kv_spec],
        out_specs=pl.BlockSpec((tq, D_MODEL), lambda b, i: (b * nq + i, 0)),
        out_shape=jax.ShapeDtypeStruct((B * L, D_MODEL), BF16),
        compiler_params=_cparams(2),
        name="attention",
    )(q, mem_k, mem_v)


HALVES = CA_HD // LANES
KV_ROWS = CA_HEADS * HALVES


def _attn_rows_body(q_ref, xk_ref, xv_ref, o_ref, *, bb, L, M):
    scale = CA_HD ** -0.5
    R = M * KV_ROWS
    lane = lax.broadcasted_iota(jnp.int32, (L, R), 1)
    q_all = q_ref[...].astype(F32)
    for i in range(bb):
        q = q_all[i * L:(i + 1) * L, :]
        qm = jnp.concatenate([q[:, j * LANES:(j + 1) * LANES] for j in range(KV_ROWS)], axis=0).astype(BF16)
        xk = xk_ref[i * R:(i + 1) * R, :].astype(BF16)
        xv = xv_ref[i * R:(i + 1) * R, :].astype(BF16)
        g = _dot_nt(qm, xk)
        ps, invs = [], []
        for hh in range(CA_HEADS):
            s = None
            for half in range(HALVES):
                j = hh * HALVES + half
                part = g[j * L:(j + 1) * L]
                if half:
                    part = pltpu.roll(part, R - half * CA_HEADS, 1)
                s = part if s is None else s + part
            s = jnp.where((lane % KV_ROWS) == hh, s * scale, -jnp.inf)
            p = jnp.exp(s - jnp.max(s, -1, keepdims=True))
            inv = 1.0 / jnp.sum(p, -1, keepdims=True)
            for half in range(HALVES):
                ps.append(pltpu.roll(p, half * CA_HEADS, 1) if half else p)
                invs.append(inv)
        o = _dot(jnp.concatenate(ps, axis=0).astype(BF16), xv)
        for j in range(KV_ROWS):
            o_ref[i * L:(i + 1) * L, j * LANES:(j + 1) * LANES] = (o[j * L:(j + 1) * L] * invs[j]).astype(o_ref.dtype)


def _attention_rows(q, xk, xv, B, L, M):
    bb = 4 if B % 4 == 0 else 1
    R = M * KV_ROWS
    kv_spec = pl.BlockSpec((bb * R, LANES), lambda b: (b, 0))
    return pl.pallas_call(
        functools.partial(_attn_rows_body, bb=bb, L=L, M=M),
        grid=(B // bb,),
        in_specs=[pl.BlockSpec((bb * L, D_MODEL), lambda b: (b, 0)), kv_spec, kv_spec],
        out_specs=pl.BlockSpec((bb * L, D_MODEL), lambda b: (b, 0)),
        out_shape=jax.ShapeDtypeStruct((B * L, D_MODEL), BF16),
        compiler_params=_cparams(1),
        name="attention_rows",
    )(q, xk, xv)


def _route(x, wt_hi, wt_lo):
    assert MOE_GROUPS == SUBLANES and MOE_EPG == SUBLANES
    x_hi = x.astype(BF16)
    x_lo = (x - x_hi.astype(F32)).astype(BF16)
    logits = _dot_nt(wt_hi, x_hi) + (_dot_nt(wt_lo, x_hi) + _dot_nt(wt_hi, x_lo))
    tm = x.shape[0]
    sub = lax.broadcasted_iota(jnp.int32, (SUBLANES, tm), 0)
    neg = -jnp.inf
    g_log = logits[0:MOE_GROUPS]
    g_max = jnp.max(g_log, 0, keepdims=True)
    g_sel = jnp.min(jnp.where(g_log == g_max, sub, MOE_GROUPS), 0, keepdims=True)
    g_w = 1.0 / jnp.sum(jnp.exp(g_log - g_max), 0, keepdims=True)
    e_log = logits[MOE_GROUPS:MOE_GROUPS + MOE_EPG]
    for g in range(1, MOE_GROUPS):
        e_log = jnp.where(g_sel == g, logits[MOE_GROUPS + g * MOE_EPG:MOE_GROUPS + (g + 1) * MOE_EPG], e_log)
    v1 = jnp.max(e_log, 0, keepdims=True)
    i1 = jnp.min(jnp.where(e_log == v1, sub, MOE_EPG), 0, keepdims=True)
    e_log2 = jnp.where(sub == i1, neg, e_log)
    v2 = jnp.max(e_log2, 0, keepdims=True)
    i2 = jnp.min(jnp.where(e_log2 == v2, sub, MOE_EPG), 0, keepdims=True)
    t = jnp.exp(v2 - v1)
    w1 = g_w / (1.0 + t)
    w2 = g_w * t / (1.0 + t)
    first = sub == 0
    return g_sel * MOE_EPG + jnp.where(first, i1, i2), jnp.where(first, w1, w2)


def _plan_body(idx_ref, dest_ref, pend_ref, carry, pstart):
    ph = pl.program_id(0)
    i = pl.program_id(1)
    tm = idx_ref.shape[0]
    lane = lax.broadcasted_iota(jnp.int32, (tm, LANES), 1)
    oh0 = (lane == idx_ref[:, 0:1]).astype(F32)
    oh1 = (lane == idx_ref[:, 1:2]).astype(F32)
    both = oh0 + oh1

    @pl.when((ph == 0) & (i == 0))
    def _():
        carry[...] = jnp.zeros(carry.shape, F32)

    @pl.when(ph == 0)
    def _():
        carry[...] += jnp.sum(both, 0, keepdims=True)

    @pl.when((ph == 1) & (i == 0))
    def _():
        padded = jnp.floor((carry[...] + (MOE_BLOCK - 1)) * (1.0 / MOE_BLOCK)) * MOE_BLOCK
        a = lax.broadcasted_iota(jnp.int32, (LANES, LANES), 0)
        b = lax.broadcasted_iota(jnp.int32, (LANES, LANES), 1)
        upper = (a < b).astype(F32)
        ps = jnp.dot(jnp.broadcast_to(padded, (SUBLANES, LANES)), upper, preferred_element_type=F32,
                     precision=lax.Precision.HIGHEST)
        pstart[...] = ps[0:1]
        carry[...] = jnp.zeros(carry.shape, F32)

    @pl.when(ph == 1)
    def _():
        r = lax.broadcasted_iota(jnp.int32, (tm, tm), 0)
        c = lax.broadcasted_iota(jnp.int32, (tm, tm), 1)
        before = (r > c).astype(BF16)
        base = _dot(before, both.astype(BF16)) + carry[...] + pstart[...]
        d0 = jnp.sum(oh0 * base, -1, keepdims=True)
        d1 = jnp.sum(oh1 * base, -1, keepdims=True)
        first = lax.broadcasted_iota(jnp.int32, (tm, 2), 1) == 0
        dest_ref[...] = jnp.where(first, d0, d1).astype(jnp.int32)
        carry[...] += jnp.sum(both, 0, keepdims=True)
        padded_tot = jnp.floor((carry[...] + (MOE_BLOCK - 1)) * (1.0 / MOE_BLOCK)) * MOE_BLOCK
        pend_ref[...] = jnp.broadcast_to(pstart[...] + padded_tot, pend_ref.shape)


def _plan(idx, *, tm):
    T = idx.shape[0]
    return pl.pallas_call(
        _plan_body,
        grid=(2, T // tm),
        in_specs=[pl.BlockSpec((tm, 2), lambda p, i: (i, 0))],
        out_specs=[pl.BlockSpec((tm, 2), lambda p, i: (i * p, 0)), pl.BlockSpec((SUBLANES, LANES), lambda p, i: (0, 0))],
        out_shape=[jax.ShapeDtypeStruct((T, 2), jnp.int32), jax.ShapeDtypeStruct((SUBLANES, LANES), F32)],
        scratch_shapes=[pltpu.VMEM((1, LANES), F32), pltpu.VMEM((1, LANES), F32)],
        compiler_params=_cparams(2),
        name="plan",
    )(idx)


def _row_copy(src_ref, s, dst_ref, d, sem):
    return pltpu.make_async_copy(src_ref.at[pl.ds(s, 1), :], dst_ref.at[pl.ds(d, 1), :], sem)


def _dispatch_body(dest_ref, x_ref, xs_in_ref, xs_ref, sem):
    del xs_in_ref
    tm = x_ref.shape[0]

    def start(t, c):
        _row_copy(x_ref, t, xs_ref, dest_ref[0, 0, 2 * t], sem).start()
        _row_copy(x_ref, t, xs_ref, dest_ref[0, 0, 2 * t + 1], sem).start()
        return c

    lax.fori_loop(0, tm, start, 0, unroll=ROW_DMA_UNROLL)
    for _ in range(2):
        pltpu.make_async_copy(x_ref, xs_ref.at[pl.ds(0, tm), :], sem).wait()


def _dispatch(dest, x, xs, *, tm):
    T = x.shape[0]
    nt = T // tm
    return pl.pallas_call(
        _dispatch_body,
        grid=(nt,),
        in_specs=[
            pl.BlockSpec((1, 1, 2 * tm), lambda i: (i, 0, 0), memory_space=pltpu.SMEM),
            pl.BlockSpec((tm, D_MODEL), lambda i: (i, 0)),
            pl.BlockSpec(memory_space=pl.ANY),
        ],
        out_specs=pl.BlockSpec(memory_space=pl.ANY),
        out_shape=jax.ShapeDtypeStruct(xs.shape, xs.dtype),
        scratch_shapes=[pltpu.SemaphoreType.DMA(())],
        input_output_aliases={2: 0},
        compiler_params=_cparams(1),
        name="dispatch",
    )(dest.reshape(nt, 1, 2 * tm), x, xs)


def _combine_body(dest_ref, x_ref, ew_ref, g_ref, b_ref, rows_ref, o_ref, gbuf, sem):
    tm = x_ref.shape[0]

    def start(t, c):
        _row_copy(rows_ref, dest_ref[0, 0, 2 * t], gbuf.at[0], t, sem).start()
        _row_copy(rows_ref, dest_ref[0, 0, 2 * t + 1], gbuf.at[1], t, sem).start()
        return c

    lax.fori_loop(0, tm, start, 0, unroll=ROW_DMA_UNROLL)
    for half in range(2):
        pltpu.make_async_copy(rows_ref.at[pl.ds(0, tm), :], gbuf.at[half], sem).wait()
    y = ew_ref[:, 0:1] * gbuf[0] + ew_ref[:, 1:2] * gbuf[1]
    o_ref[...] = _layer_norm(ALPHA * x_ref[...] + y, g_ref[...], b_ref[...])


def _combine(dest, x, ew, g, b, rows, *, tm):
    T = x.shape[0]
    nt = T // tm
    const = lambda i: (0, 0)
    return pl.pallas_call(
        _combine_body,
        grid=(nt,),
        in_specs=[
            pl.BlockSpec((1, 1, 2 * tm), lambda i: (i, 0, 0), memory_space=pltpu.SMEM),
            pl.BlockSpec((tm, D_MODEL), lambda i: (i, 0)),
            pl.BlockSpec((tm, 2), lambda i: (i, 0)),
            pl.BlockSpec((1, D_MODEL), const),
            pl.BlockSpec((1, D_MODEL), const),
            pl.BlockSpec(memory_space=pl.ANY),
        ],
        out_specs=pl.BlockSpec((tm, D_MODEL), lambda i: (i, 0)),
        out_shape=jax.ShapeDtypeStruct((T, D_MODEL), F32),
        scratch_shapes=[pltpu.VMEM((2, tm, D_MODEL), F32), pltpu.SemaphoreType.DMA(())],
        compiler_params=_cparams(1),
        name="combine",
    )(dest.reshape(nt, 1, 2 * tm), x, ew, g, b, rows)


def _experts_body(be_ref, nu_ref, xs_ref, wg_ref, wu_ref, wd_ref, o_ref):
    del be_ref
    i = pl.program_id(0)

    @pl.when(i < nu_ref[0])
    def _():
        xb = xs_ref[...].astype(BF16)
        gate = _dot(xb, wg_ref[0].astype(BF16))
        up = _dot(xb, wu_ref[0].astype(BF16))
        hid = gate * jax.nn.sigmoid(gate) * up
        o_ref[...] = _dot(hid.astype(BF16), wd_ref[0].astype(BF16))

    @pl.when(i >= nu_ref[0])
    def _():
        o_ref[...] = jnp.zeros(o_ref.shape, F32)


def _experts(block_expert, n_used, xs, w_gate, w_up, w_down):
    R = xs.shape[0]
    nb = R // MOE_BLOCK
    grid_spec = pltpu.PrefetchScalarGridSpec(
        num_scalar_prefetch=2,
        grid=(nb,),
        in_specs=[
            pl.BlockSpec((MOE_BLOCK, D_MODEL), lambda i, be, nu: (i, 0)),
            pl.BlockSpec((1, D_MODEL, MOE_FF), lambda i, be, nu: (be[i], 0, 0)),
            pl.BlockSpec((1, D_MODEL, MOE_FF), lambda i, be, nu: (be[i], 0, 0)),
            pl.BlockSpec((1, MOE_FF, D_MODEL), lambda i, be, nu: (be[i], 0, 0)),
        ],
        out_specs=pl.BlockSpec((MOE_BLOCK, D_MODEL), lambda i, be, nu: (i, 0)),
    )
    return pl.pallas_call(
        _experts_body,
        grid_spec=grid_spec,
        out_shape=jax.ShapeDtypeStruct((R, D_MODEL), F32),
        compiler_params=_cparams(1),
        name="experts",
    )(block_expert, n_used, xs, w_gate, w_up, w_down)


def _row_tile(T, pref):
    t = pref
    while T % t:
        t //= 2
    return t


def _group_to_x2(x, mem_k, mem_v, conv_prev, pool_prev, s0, pos0, W):
    B, L, _ = x.shape
    T = B * L
    xf = x.reshape(T, D_MODEL)
    tm = _row_tile(T, 512)
    mixer_w = (W["w_conv"], W["a_log"], W["dt_bias"], W["w_onorm"], W["w_pool"], W["pool_scale"])
    if L >= FUSE_MIN_SEQ:
        merged, s_new, conv_tail, pool_tail = _mixer(xf, B, L, conv_prev, pool_prev, s0, *mixer_w, pos0=pos0,
                                                     w_in=W["w_in"])
    else:
        proj = _mm(xf, W["w_in"], tm=tm, tn=N_PROJ // 3)
        merged, s_new, conv_tail, pool_tail = _mixer(proj, B, L, conv_prev, pool_prev, s0, *mixer_w, pos0=pos0)
    x1, q = _proj_ln(merged, W["w_out"], xf, W["ln1_g"], W["ln1_b"], (W["w_cq"],), tm=tm, route=False)
    if mem_k.ndim == 3:
        att = _attention(q, mem_k, mem_v, B, L)
    else:
        M = mem_k.shape[1]

        def rows_view(m):
            m = m.reshape(B, M, CA_HEADS, HALVES, LANES)
            return jnp.swapaxes(m, 2, 3).reshape(B * M * KV_ROWS, LANES)

        att = _attention_rows(q, rows_view(mem_k), rows_view(mem_v), B, L, M)
    x2, idx, ew = _proj_ln(att, W["w_co"], x1, W["ln2_g"], W["ln2_b"], (W["w_r_hi"], W["w_r_lo"]), tm=tm, route=True)
    idx, ew = idx[0:2].T, ew[0:2].T
    return x2, idx, ew, s_new, conv_tail[:, -(CONV_W - 1):], pool_tail[:, -(POOL_MAX - 1):]


def kernel(x_prompt, x_sample, cache_mem_k, cache_mem_v, state_delta, state_conv, state_pool, mem_prompt, w_in, w_conv, a_log, dt_bias, w_onorm, w_pool, pool_scale, w_out, ln1_g, ln1_b, w_cq, w_ck, w_cv, w_co, ln2_g, ln2_b, w_router_group, w_router_expert, w_gate, w_up, w_down, ln3_g, ln3_b):
    Bp, Lp, _ = x_prompt.shape
    Bs, Ls, _ = x_sample.shape
    Tp, Ts = Bp * Lp, Bs * Ls
    lyr = 0

    def pad_lanes(v):
        return jnp.pad(v.astype(F32), (0, LANES - v.shape[0])).reshape(1, LANES)

    o1, o2, o3, o4 = QKV_DIM, QKV_DIM + D_MODEL, QKV_DIM + D_MODEL + 2 * DN_HEADS, QKV_DIM + 2 * D_MODEL + 2 * DN_HEADS
    wi = w_in[lyr]
    w_in_r = jnp.concatenate(
        [wi[:, :o2], wi[:, o3:], wi[:, o2:o3], jnp.zeros((D_MODEL, LANES - 2 * DN_HEADS), F32)], axis=1).astype(BF16)
    del o1, o4
    w_r = jnp.concatenate([w_router_group[lyr], w_router_expert[lyr],
                           jnp.zeros((D_MODEL, LANES - MOE_GROUPS - MOE_EXPERTS), F32)], axis=1).T
    w_r_hi = w_r.astype(BF16)
    W = {
        "w_r_hi": w_r_hi,
        "w_r_lo": (w_r - w_r_hi.astype(F32)).astype(BF16),
        "w_in": w_in_r,
        "w_conv": w_conv[lyr],
        "a_log": pad_lanes(a_log[lyr]),
        "dt_bias": pad_lanes(dt_bias[lyr]),
        "w_onorm": w_onorm[lyr].reshape(1, DN_HD),
        "w_pool": w_pool[lyr].astype(BF16),
        "pool_scale": pool_scale[lyr].reshape(1, D_MODEL),
        "w_out": w_out[lyr].astype(BF16),
        "ln1_g": ln1_g[lyr].reshape(1, D_MODEL), "ln1_b": ln1_b[lyr].reshape(1, D_MODEL),
        "w_cq": w_cq[lyr].astype(BF16),
        "w_co": w_co[lyr].astype(BF16),
        "ln2_g": ln2_g[lyr].reshape(1, D_MODEL), "ln2_b": ln2_b[lyr].reshape(1, D_MODEL),
    }

    M = mem_prompt.shape[1]
    memf = mem_prompt.reshape(Bp * M, D_MODEL)
    tmm = _row_tile(Bp * M, 512)
    mk = _mm(memf, w_ck[lyr].astype(BF16), tm=tmm, tn=D_MODEL).reshape(Bp, M, D_MODEL)
    mv = _mm(memf, w_cv[lyr].astype(BF16), tm=tmm, tn=D_MODEL).reshape(Bp, M, D_MODEL)
    x2_p, idx_p, ew_p, d_p, c_p, p_p = _group_to_x2(x_prompt, mk, mv, None, None, None, 0, W)
    mk = mk.reshape(Bp, M, CA_HEADS, CA_HD)
    mv = mv.reshape(Bp, M, CA_HEADS, CA_HD)
    x2_s, idx_s, ew_s, d_s, c_s, p_s = _group_to_x2(x_sample, cache_mem_k[lyr], cache_mem_v[lyr], state_conv[lyr],
                                       state_pool[lyr], state_delta[lyr], PAST_LEN, W)

    tp, ts = _row_tile(Tp, 256), _row_tile(Ts, 256)
    T = Tp + Ts
    dest, pend = _plan(jnp.concatenate([idx_p, idx_s], axis=0), tm=_row_tile(T, 1024))
    n_blocks = (2 * T + MOE_EXPERTS * (MOE_BLOCK - 1) + MOE_BLOCK - 1) // MOE_BLOCK
    pend_e = pend[0, :MOE_EXPERTS].astype(jnp.int32)
    block_start = jnp.arange(n_blocks, dtype=jnp.int32) * MOE_BLOCK
    block_expert = jnp.minimum(jnp.sum(block_start[:, None] >= pend_e[None, :], axis=1), MOE_EXPERTS - 1).astype(jnp.int32)
    n_used = (pend_e[MOE_EXPERTS - 1:] // MOE_BLOCK).astype(jnp.int32)
    xs = jnp.zeros((n_blocks * MOE_BLOCK, D_MODEL), F32)
    xs = _dispatch(dest[:Tp], x2_p, xs, tm=tp)
    xs = _dispatch(dest[Tp:], x2_s, xs, tm=ts)
    rows = _experts(block_expert, n_used, xs, w_gate[lyr], w_up[lyr], w_down[lyr])
    g3, b3 = ln3_g[lyr].reshape(1, D_MODEL), ln3_b[lyr].reshape(1, D_MODEL)
    y_p = _combine(dest[:Tp], x2_p, ew_p, g3, b3, rows, tm=tp).reshape(Bp, Lp, D_MODEL)
    y_s = _combine(dest[Tp:], x2_s, ew_s, g3, b3, rows, tm=ts).reshape(Bs, Ls, D_MODEL)

    return (y_p, y_s, d_p[None], c_p[None], p_p[None], mk[None], mv[None], d_s[None], c_s[None], p_s[None])
```

```python
import functools

import jax
import jax.numpy as jnp
from jax import lax
from jax.experimental import pallas as pl
from jax.experimental.pallas import tpu as pltpu

F32 = jnp.float32
BF16 = jnp.bfloat16

D_MODEL = 1024
DN_HEADS = 8
DN_HD = 128
QKV_DIM = 3 * DN_HEADS * DN_HD
CONV_W = 4
DN_CHUNK = 64
POOL_WINDOWS = (2, 4, 8, 16)
POOL_GD = D_MODEL // len(POOL_WINDOWS)
POOL_MAX = 16
CA_HEADS = 4
CA_HD = D_MODEL // CA_HEADS
MOE_GROUPS = 8
MOE_EPG = 8
MOE_EXPERTS = MOE_GROUPS * MOE_EPG
MOE_FF = D_MODEL // 4
MOE_BLOCK = 256
PAST_LEN = 16384
LN_EPS = 1e-5
NORM_EPS = 1e-6
ALPHA = 2.0 ** 0.25

LANES = 128
SUBLANES = 8
CONV_HALO = SUBLANES
POOL_HALO = POOL_MAX
MIN_CHUNK = 16
N_PROJ = 7 * D_MODEL + LANES
VMEM_LIMIT = 48 * 1024 * 1024
MIXER_VMEM_LIMIT = 56 * 1024 * 1024
ROW_DMA_UNROLL = 8
FUSE_MIN_SEQ = 128
SEQS_PER_STEP = 4


def _cparams(n_axes, vmem=VMEM_LIMIT):
    return pltpu.CompilerParams(dimension_semantics=("arbitrary",) * n_axes, vmem_limit_bytes=vmem)


def _layer_norm(x, g, b):
    mu = jnp.mean(x, -1, keepdims=True)
    xc = x - mu
    var = jnp.mean(xc * xc, -1, keepdims=True)
    return xc * lax.rsqrt(var + LN_EPS) * g + b


def _softplus(x):
    return jnp.maximum(x, 0.0) + jnp.log1p(jnp.exp(-jnp.abs(x)))


def _dot(a, b):
    return jnp.dot(a, b, preferred_element_type=F32)


def _dot_nt(a, b):
    return lax.dot_general(a, b, (((1,), (1,)), ((), ())), preferred_element_type=F32)


def _dot_tn(a, b):
    return lax.dot_general(a, b, (((0,), (0,)), ((), ())), preferred_element_type=F32)


def _mm_body(x_ref, w_ref, o_ref):
    o_ref[...] = _dot(x_ref[...].astype(BF16), w_ref[...]).astype(o_ref.dtype)


def _mm(x, w, *, tm, tn, out_dtype=F32):
    T, K = x.shape
    N = w.shape[1]
    return pl.pallas_call(
        _mm_body,
        grid=(N // tn, T // tm),
        in_specs=[pl.BlockSpec((tm, K), lambda j, i: (i, 0)), pl.BlockSpec((K, tn), lambda j, i: (0, j))],
        out_specs=pl.BlockSpec((tm, tn), lambda j, i: (i, j)),
        out_shape=jax.ShapeDtypeStruct((T, N), out_dtype),
        compiler_params=_cparams(2),
        name="mm",
    )(x, w)


def _mixer_body(*refs, tl, chunk, pos0, has_state, fused, nb):
    refs = list(refs)
    take = lambda n: [refs.pop(0) for _ in range(n)]
    if fused:
        x_ref, win_hbm = take(2)
    else:
        qkv_ref, z_ref, p_ref, ga_ref, gb_ref, ab_ref = take(6)
    if has_state:
        convp_ref, poolp_ref, s0_ref = take(3)
    wconv_ref, alog_ref, dtb_ref, wonorm_ref, wpool_ref, pscale_ref = take(6)
    merged_ref, sout_ref, convo_ref, poolo_ref = take(4)
    qkvbuf, pbuf, obuf, gcb, betab, gt_ref = take(6)
    if fused:
        z_ref, ga_ref, gb_ref, win_vmem, win_sem = take(5)
    b = pl.program_id(0)
    l = pl.program_id(1)
    tlp = max(tl, MIN_CHUNK)
    C = max(chunk, MIN_CHUNK)
    cps = tlp // C
    seqs = range(nb)

    @pl.when(l == 0)
    def _init():
        for s in seqs:
            qkvbuf[s, 0:CONV_HALO, :] = jnp.zeros((CONV_HALO, QKV_DIM), F32)
            pbuf[s, 0:POOL_HALO, :] = jnp.zeros((POOL_HALO, D_MODEL), F32)
            if has_state:
                qkvbuf[s, CONV_HALO - (CONV_W - 1):CONV_HALO, :] = convp_ref[s]
                pbuf[s, POOL_HALO - (POOL_MAX - 1):POOL_HALO, :] = poolp_ref[s]
        if has_state:
            sout_ref[...] = s0_ref[...]
        else:
            sout_ref[...] = jnp.zeros(sout_ref.shape, F32)

    if fused:
        @pl.when((b == 0) & (l == 0))
        def _load_weight():
            cp = pltpu.make_async_copy(win_hbm, win_vmem, win_sem)
            cp.start()
            cp.wait()

        xb = x_ref[...].astype(BF16)

        def proj_cols(c0, width):
            return _dot(xb, win_vmem[:, c0:c0 + width])

        ab = proj_cols(7 * D_MODEL, LANES)
        for c in range(QKV_DIM // D_MODEL):
            qkvbuf[0, CONV_HALO:CONV_HALO + tl, c * D_MODEL:(c + 1) * D_MODEL] = proj_cols(c * D_MODEL, D_MODEL)
        z_ref[...] = proj_cols(3 * D_MODEL, D_MODEL)
        pbuf[0, POOL_HALO:POOL_HALO + tl, :] = proj_cols(4 * D_MODEL, D_MODEL)
        ga_ref[...] = proj_cols(5 * D_MODEL, D_MODEL)
        gb_ref[...] = proj_cols(6 * D_MODEL, D_MODEL)
    else:
        ab = ab_ref[...]
        for s in seqs:
            qkvbuf[s, CONV_HALO:CONV_HALO + tl, :] = qkv_ref[s * tl:(s + 1) * tl, :]
            pbuf[s, POOL_HALO:POOL_HALO + tl, :] = p_ref[s * tl:(s + 1) * tl, :]

    def pad_seq(pieces):
        out = []
        for x in pieces:
            out.append(x)
            if tlp != tl:
                out.append(jnp.zeros((tlp - tl, x.shape[1]), F32))
        return out[0] if len(out) == 1 else jnp.concatenate(out, axis=0)

    def unpad_seq(x):
        if tlp == tl:
            return x
        return jnp.concatenate([x[s * tlp:s * tlp + tl] for s in seqs], axis=0)

    g_raw = -jnp.exp(alog_ref[...]) * _softplus(ab + dtb_ref[...])
    beta_raw = jax.nn.sigmoid(ab)
    g_all = pad_seq([g_raw[s * tl:(s + 1) * tl] for s in seqs])
    beta_all = pad_seq([beta_raw[s * tl:(s + 1) * tl] for s in seqs])
    R = nb * tlp
    ri = lax.broadcasted_iota(jnp.int32, (R, R), 0)
    ci = lax.broadcasted_iota(jnp.int32, (R, R), 1)
    cum_mat = ((ri >= ci) & ((ri // C) == (ci // C))).astype(F32)
    gcum = jnp.dot(cum_mat, g_all, preferred_element_type=F32, precision=lax.Precision.HIGHEST)
    gcum_t = gcum.T
    for h in range(DN_HEADS):
        gt_ref[h] = jnp.broadcast_to(gcum_t[h:h + 1, :], (SUBLANES, R))
        gcb[h] = jnp.broadcast_to(gcum[:, h:h + 1], (R, LANES))
        betab[h] = jnp.broadcast_to(beta_all[:, DN_HEADS + h:DN_HEADS + h + 1], (R, LANES))

    ii = lax.broadcasted_iota(jnp.int32, (C, C), 0)
    jj = lax.broadcasted_iota(jnp.int32, (C, C), 1)
    causal = ii >= jj
    strict = ii > jj
    eye = (ii == jj).astype(F32)
    n_sq = max(C.bit_length() - 2, 0)

    def conv_slab(col0):
        cols = slice(col0, col0 + DN_HD)
        pieces = []
        for s in seqs:
            acc = qkvbuf[s, CONV_HALO:CONV_HALO + tl, cols] * wconv_ref[CONV_W - 1:CONV_W, cols]
            for i in range(CONV_W - 1):
                r0 = CONV_HALO - (CONV_W - 1) + i
                acc = acc + qkvbuf[s, r0:r0 + tl, cols] * wconv_ref[i:i + 1, cols]
            pieces.append(acc * jax.nn.sigmoid(acc))
        return pad_seq(pieces)

    heads = range(DN_HEADS)
    chunks = range(nb * cps)
    probs = [(h, c) for h in heads for c in chunks]
    rows = [slice(c * C, (c + 1) * C) for c in chunks]
    q, k, v, gcs, bs, egs, kbs, grows = [], [], [], [], [], [], [], []
    for h in heads:
        qh = conv_slab(h * DN_HD)
        kh = conv_slab((DN_HEADS + h) * DN_HD)
        v.append(conv_slab((2 * DN_HEADS + h) * DN_HD))
        q.append(qh * lax.rsqrt(jnp.sum(qh * qh, -1, keepdims=True) + NORM_EPS) * (DN_HD ** -0.5))
        kh = kh * lax.rsqrt(jnp.sum(kh * kh, -1, keepdims=True) + NORM_EPS)
        k.append(kh)
        gcs.append(gcb[h])
        bs.append(betab[h])
        egs.append(jnp.exp(gcs[h]))
        kbs.append(kh * bs[h])
        grows.append(gt_ref[h][0:1, :])

    aq = {(h, c): _dot_nt(jnp.concatenate([kbs[h][rows[c]], q[h][rows[c]]], axis=0).astype(BF16),
                          k[h][rows[c]].astype(BF16)) for h, c in probs}
    decay = {(h, c): jnp.exp(jnp.where(causal, gcs[h][rows[c], 0:C] - grows[h][:, c * C:(c + 1) * C], -jnp.inf))
             for h, c in probs}
    A = {p: jnp.where(strict, aq[p][0:C] * decay[p], 0.0) for p in probs}
    qk = {p: (aq[p][C:2 * C] * decay[p]).astype(BF16) for p in probs}
    P = {p: eye - A[p] for p in probs}
    Q = {p: _dot(A[p].astype(BF16), A[p].astype(BF16)) for p in probs}
    for _ in range(n_sq - 1):
        pq = {p: _dot(jnp.concatenate([P[p], Q[p]], axis=0).astype(BF16), Q[p].astype(BF16)) for p in probs}
        P = {p: P[p] + pq[p][0:C] for p in probs}
        Q = {p: pq[p][C:2 * C] for p in probs}
    P = {p: P[p] + _dot(P[p].astype(BF16), Q[p].astype(BF16)) for p in probs}
    sol = {(h, c): _dot(P[(h, c)].astype(BF16),
                        jnp.concatenate([v[h][rows[c]] * bs[h][rows[c]], kbs[h][rows[c]] * egs[h][rows[c]]],
                                        axis=1).astype(BF16)) for h, c in probs}

    S = [[sout_ref[s, h] for h in heads] for s in seqs]
    sh = [(s, h) for s in seqs for h in heads]
    outs = {}
    for j in range(cps):
        cof = {s: s * cps + j for s in seqs}
        ws_qs = {(s, h): _dot(jnp.concatenate([sol[(h, cof[s])][:, DN_HD:2 * DN_HD],
                                               q[h][rows[cof[s]]] * egs[h][rows[cof[s]]]], axis=0).astype(BF16),
                              S[s][h].astype(BF16)) for s, h in sh}
        v_new = {(s, h): (sol[(h, cof[s])][:, 0:DN_HD] - ws_qs[(s, h)][0:C]).astype(BF16) for s, h in sh}
        glast = {(s, h): gcs[h][rows[cof[s]]][C - 1:C, :] for s, h in sh}
        S_new = {(s, h): S[s][h] * jnp.exp(glast[(s, h)])
                 + _dot_tn((k[h][rows[cof[s]]] * jnp.exp(glast[(s, h)] - gcs[h][rows[cof[s]]])).astype(BF16),
                           v_new[(s, h)]) for s, h in sh}
        S = [[S_new[(s, h)] for h in heads] for s in seqs]
        for s, h in sh:
            outs[(h, cof[s])] = ws_qs[(s, h)][C:2 * C] + _dot(qk[(h, cof[s])], v_new[(s, h)])
    for s, h in sh:
        sout_ref[s, h] = S[s][h]
    for h in heads:
        o = outs[(h, 0)] if len(chunks) == 1 else jnp.concatenate([outs[(h, c)] for c in chunks], axis=0)
        o = unpad_seq(o)
        o = o * lax.rsqrt(jnp.mean(o * o, -1, keepdims=True) + NORM_EPS) * wonorm_ref[...]
        zh = z_ref[:, h * DN_HD:(h + 1) * DN_HD]
        obuf[:, h * DN_HD:(h + 1) * DN_HD] = o * (zh * jax.nn.sigmoid(zh))

    if pos0 == 0:
        pos = l * tl + lax.broadcasted_iota(jnp.int32, (tl, 1), 0)
    for gi, win in enumerate(POOL_WINDOWS):
        cs = slice(gi * POOL_GD, (gi + 1) * POOL_GD)
        pooled = []
        for s in seqs:
            acc = pbuf[s, POOL_HALO:POOL_HALO + tl, cs]
            for j in range(1, win):
                acc = acc + pbuf[s, POOL_HALO - j:POOL_HALO - j + tl, cs]
            if pos0 == 0:
                cnt = jnp.minimum(win, pos + 1).astype(F32)
            else:
                cnt = float(min(win, pos0 + 1))
            pooled.append(acc / cnt - pbuf[s, POOL_HALO:POOL_HALO + tl, cs])
        pooled = pooled[0] if nb == 1 else jnp.concatenate(pooled, axis=0)
        bb = _dot(pooled.astype(BF16), wpool_ref[gi]) * pscale_ref[:, cs]
        merged_ref[:, cs] = (jax.nn.sigmoid(ga_ref[:, cs]) * obuf[:, cs]
                             + jax.nn.sigmoid(gb_ref[:, cs]) * bb).astype(merged_ref.dtype)

    for s in seqs:
        conv_tail = qkvbuf[s, tl:tl + CONV_HALO, :]
        pool_tail = pbuf[s, tl:tl + POOL_HALO, :]
        convo_ref[s] = conv_tail
        poolo_ref[s] = pool_tail
        if tl >= POOL_HALO:
            qkvbuf[s, 0:CONV_HALO, :] = conv_tail
            pbuf[s, 0:POOL_HALO, :] = pool_tail


def _mixer(src, B, L, conv_prev, pool_prev, s0, wconv, alog, dtb, wonorm, wpool, pscale, *, pos0, w_in=None):
    tl = min(256, L)
    nL = L // tl
    assert nL == 1 or tl >= POOL_HALO
    chunk = min(DN_CHUNK, tl)
    tlp = max(tl, MIN_CHUNK)
    has_state = s0 is not None
    fused = w_in is not None
    nb = next(n for n in (SEQS_PER_STEP, 2, 1) if B % n == 0) if (nL == 1 and not fused) else 1
    rt = nb * tl
    row = lambda b, l: b * nL + l
    if fused:
        in_specs = [pl.BlockSpec((rt, D_MODEL), lambda b, l: (row(b, l), 0)), pl.BlockSpec(memory_space=pl.ANY)]
        args = [src, w_in]
    else:
        in_specs = [
            pl.BlockSpec((rt, QKV_DIM), lambda b, l: (row(b, l), 0)),
            pl.BlockSpec((rt, D_MODEL), lambda b, l: (row(b, l), 3)),
            pl.BlockSpec((rt, D_MODEL), lambda b, l: (row(b, l), 4)),
            pl.BlockSpec((rt, D_MODEL), lambda b, l: (row(b, l), 5)),
            pl.BlockSpec((rt, D_MODEL), lambda b, l: (row(b, l), 6)),
            pl.BlockSpec((rt, LANES), lambda b, l: (row(b, l), 7 * D_MODEL // LANES)),
        ]
        args = [src] * 6
    if has_state:
        in_specs += [
            pl.BlockSpec((nb, CONV_W - 1, QKV_DIM), lambda b, l: (b, 0, 0)),
            pl.BlockSpec((nb, POOL_MAX - 1, D_MODEL), lambda b, l: (b, 0, 0)),
            pl.BlockSpec((nb, DN_HEADS, DN_HD, DN_HD), lambda b, l: (b, 0, 0, 0)),
        ]
        args += [conv_prev, pool_prev, s0]
    const2 = lambda b, l: (0, 0)
    in_specs += [
        pl.BlockSpec((CONV_W, QKV_DIM), const2),
        pl.BlockSpec((1, LANES), const2),
        pl.BlockSpec((1, LANES), const2),
        pl.BlockSpec((1, DN_HD), const2),
        pl.BlockSpec((len(POOL_WINDOWS), POOL_GD, POOL_GD), lambda b, l: (0, 0, 0)),
        pl.BlockSpec((1, D_MODEL), const2),
    ]
    args += [wconv, alog, dtb, wonorm, wpool, pscale]
    scratch = [
        pltpu.VMEM((nb, CONV_HALO + tl, QKV_DIM), F32),
        pltpu.VMEM((nb, POOL_HALO + tl, D_MODEL), F32),
        pltpu.VMEM((rt, D_MODEL), F32),
        pltpu.VMEM((DN_HEADS, nb * tlp, LANES), F32),
        pltpu.VMEM((DN_HEADS, nb * tlp, LANES), F32),
        pltpu.VMEM((DN_HEADS, SUBLANES, nb * tlp), F32),
    ]
    if fused:
        scratch += [pltpu.VMEM((tl, D_MODEL), F32)] * 3
        scratch += [pltpu.VMEM(w_in.shape, w_in.dtype), pltpu.SemaphoreType.DMA(())]
    return pl.pallas_call(
        functools.partial(_mixer_body, tl=tl, chunk=chunk, pos0=pos0, has_state=has_state, fused=fused, nb=nb),
        grid=(B // nb, nL),
        in_specs=in_specs,
        out_specs=[
            pl.BlockSpec((rt, D_MODEL), lambda b, l: (row(b, l), 0)),
            pl.BlockSpec((nb, DN_HEADS, DN_HD, DN_HD), lambda b, l: (b, 0, 0, 0)),
            pl.BlockSpec((nb, CONV_HALO, QKV_DIM), lambda b, l: (b, 0, 0)),
            pl.BlockSpec((nb, POOL_HALO, D_MODEL), lambda b, l: (b, 0, 0)),
        ],
        out_shape=[
            jax.ShapeDtypeStruct((B * L, D_MODEL), BF16),
            jax.ShapeDtypeStruct((B, DN_HEADS, DN_HD, DN_HD), F32),
            jax.ShapeDtypeStruct((B, CONV_HALO, QKV_DIM), F32),
            jax.ShapeDtypeStruct((B, POOL_HALO, D_MODEL), F32),
        ],
        scratch_shapes=scratch,
        compiler_params=_cparams(2, vmem=MIXER_VMEM_LIMIT),
        name="mixer",
    )(*args)


def _proj_ln_q_body(a_ref, w_ref, r_ref, g_ref, b_ref, wq_ref, o_ref, q_ref):
    y = ALPHA * r_ref[...] + _dot(a_ref[...], w_ref[...])
    x1 = _layer_norm(y, g_ref[...], b_ref[...])
    o_ref[...] = x1
    q_ref[...] = _dot(x1.astype(BF16), wq_ref[...]).astype(q_ref.dtype)


def _proj_ln_route_body(a_ref, w_ref, r_ref, g_ref, b_ref, whi_ref, wlo_ref, o_ref, idx_ref, ew_ref):
    y = ALPHA * r_ref[...] + _dot(a_ref[...], w_ref[...])
    x2 = _layer_norm(y, g_ref[...], b_ref[...])
    o_ref[...] = x2
    idx_ref[...], ew_ref[...] = _route(x2, whi_ref[...], wlo_ref[...])


def _proj_ln(a, w, resid, g, b, tail, *, tm, route):
    T = a.shape[0]
    const = lambda i: (0, 0)
    rows = lambda i: (i, 0)
    in_specs = [
        pl.BlockSpec((tm, D_MODEL), rows),
        pl.BlockSpec((D_MODEL, D_MODEL), const),
        pl.BlockSpec((tm, D_MODEL), rows),
        pl.BlockSpec((1, D_MODEL), const),
        pl.BlockSpec((1, D_MODEL), const),
    ] + [pl.BlockSpec(t.shape, const) for t in tail]
    out_specs = [pl.BlockSpec((tm, D_MODEL), rows)]
    out_shape = [jax.ShapeDtypeStruct((T, D_MODEL), F32)]
    if route:
        cols = lambda i: (0, i)
        out_specs += [pl.BlockSpec((SUBLANES, tm), cols), pl.BlockSpec((SUBLANES, tm), cols)]
        out_shape += [jax.ShapeDtypeStruct((SUBLANES, T), jnp.int32), jax.ShapeDtypeStruct((SUBLANES, T), F32)]
    else:
        out_specs += [pl.BlockSpec((tm, D_MODEL), rows)]
        out_shape += [jax.ShapeDtypeStruct((T, D_MODEL), BF16)]
    return pl.pallas_call(
        _proj_ln_route_body if route else _proj_ln_q_body,
        grid=(T // tm,),
        in_specs=in_specs,
        out_specs=out_specs,
        out_shape=out_shape,
        compiler_params=_cparams(1),
        name="proj_ln_route" if route else "proj_ln_q",
    )(a, w, resid, g, b, *tail)


def _attn_body(q_ref, k_ref, v_ref, o_ref):
    scale = CA_HD ** -0.5
    for hh in range(CA_HEADS):
        cs = slice(hh * CA_HD, (hh + 1) * CA_HD)
        s = _dot_nt(q_ref[:, cs].astype(BF16), k_ref[0, :, cs].astype(BF16)) * scale
        m = jnp.max(s, -1, keepdims=True)
        p = jnp.exp(s - m)
        denom = jnp.sum(p, -1, keepdims=True)
        o = _dot(p.astype(BF16), v_ref[0, :, cs].astype(BF16)) / denom
        o_ref[:, cs] = o.astype(o_ref.dtype)


def _attention(q, mem_k, mem_v, B, L):
    tq = min(512, L)
    nq = L // tq
    M = mem_k.shape[1]
    kv_spec = pl.BlockSpec((1, M, D_MODEL), lambda b, i: (b, 0, 0))
    return pl.pallas_call(
        _attn_body,
        grid=(B, nq),
        in_specs=[pl.BlockSpec((tq, D_MODEL), lambda b, i: (b * nq + i, 0)), kv_spec, kv_spec],
        out_specs=pl.BlockSpec((tq, D_MODEL), lambda b, i: (b * nq + i, 0)),
        out_shape=jax.ShapeDtypeStruct((B * L, D_MODEL), BF16),
        compiler_params=_cparams(2),
        name="attention",
    )(q, mem_k, mem_v)


HALVES = CA_HD // LANES
KV_ROWS = CA_HEADS * HALVES


def _attn_rows_body(q_ref, xk_ref, xv_ref, o_ref, *, bb, L, M):
    scale = CA_HD ** -0.5
    R = M * KV_ROWS
    lane = lax.broadcasted_iota(jnp.int32, (L, R), 1)
    q_all = q_ref[...].astype(F32)
    for i in range(bb):
        q = q_all[i * L:(i + 1) * L, :]
        qm = jnp.concatenate([q[:, j * LANES:(j + 1) * LANES] for j in range(KV_ROWS)], axis=0).astype(BF16)
        xk = xk_ref[i * R:(i + 1) * R, :].astype(BF16)
        xv = xv_ref[i * R:(i + 1) * R, :].astype(BF16)
        g = _dot_nt(qm, xk)
        ps, invs = [], []
        for hh in range(CA_HEADS):
            s = None
            for half in range(HALVES):
                j = hh * HALVES + half
                part = g[j * L:(j + 1) * L]
                if half:
                    part = pltpu.roll(part, R - half * CA_HEADS, 1)
                s = part if s is None else s + part
            s = jnp.where((lane % KV_ROWS) == hh, s * scale, -jnp.inf)
            p = jnp.exp(s - jnp.max(s, -1, keepdims=True))
            inv = 1.0 / jnp.sum(p, -1, keepdims=True)
            for half in range(HALVES):
                ps.append(pltpu.roll(p, half * CA_HEADS, 1) if half else p)
                invs.append(inv)
        o = _dot(jnp.concatenate(ps, axis=0).astype(BF16), xv)
        for j in range(KV_ROWS):
            o_ref[i * L:(i + 1) * L, j * LANES:(j + 1) * LANES] = (o[j * L:(j + 1) * L] * invs[j]).astype(o_ref.dtype)


def _attention_rows(q, xk, xv, B, L, M):
    bb = 4 if B % 4 == 0 else 1
    R = M * KV_ROWS
    kv_spec = pl.BlockSpec((bb * R, LANES), lambda b: (b, 0))
    return pl.pallas_call(
        functools.partial(_attn_rows_body, bb=bb, L=L, M=M),
        grid=(B // bb,),
        in_specs=[pl.BlockSpec((bb * L, D_MODEL), lambda b: (b, 0)), kv_spec, kv_spec],
        out_specs=pl.BlockSpec((bb * L, D_MODEL), lambda b: (b, 0)),
        out_shape=jax.ShapeDtypeStruct((B * L, D_MODEL), BF16),
        compiler_params=_cparams(1),
        name="attention_rows",
    )(q, xk, xv)


def _route(x, wt_hi, wt_lo):
    assert MOE_GROUPS == SUBLANES and MOE_EPG == SUBLANES
    x_hi = x.astype(BF16)
    x_lo = (x - x_hi.astype(F32)).astype(BF16)
    logits = _dot_nt(wt_hi, x_hi) + (_dot_nt(wt_lo, x_hi) + _dot_nt(wt_hi, x_lo))
    tm = x.shape[0]
    sub = lax.broadcasted_iota(jnp.int32, (SUBLANES, tm), 0)
    neg = -jnp.inf
    g_log = logits[0:MOE_GROUPS]
    g_max = jnp.max(g_log, 0, keepdims=True)
    g_sel = jnp.min(jnp.where(g_log == g_max, sub, MOE_GROUPS), 0, keepdims=True)
    g_w = 1.0 / jnp.sum(jnp.exp(g_log - g_max), 0, keepdims=True)
    e_log = logits[MOE_GROUPS:MOE_GROUPS + MOE_EPG]
    for g in range(1, MOE_GROUPS):
        e_log = jnp.where(g_sel == g, logits[MOE_GROUPS + g * MOE_EPG:MOE_GROUPS + (g + 1) * MOE_EPG], e_log)
    v1 = jnp.max(e_log, 0, keepdims=True)
    i1 = jnp.min(jnp.where(e_log == v1, sub, MOE_EPG), 0, keepdims=True)
    e_log2 = jnp.where(sub == i1, neg, e_log)
    v2 = jnp.max(e_log2, 0, keepdims=True)
    i2 = jnp.min(jnp.where(e_log2 == v2, sub, MOE_EPG), 0, keepdims=True)
    t = jnp.exp(v2 - v1)
    w1 = g_w / (1.0 + t)
    w2 = g_w * t / (1.0 + t)
    first = sub == 0
    return g_sel * MOE_EPG + jnp.where(first, i1, i2), jnp.where(first, w1, w2)


def _plan_body(idx_ref, dest_ref, pend_ref, carry, pstart):
    ph = pl.program_id(0)
    i = pl.program_id(1)
    tm = idx_ref.shape[0]
    lane = lax.broadcasted_iota(jnp.int32, (tm, LANES), 1)
    oh0 = (lane == idx_ref[:, 0:1]).astype(F32)
    oh1 = (lane == idx_ref[:, 1:2]).astype(F32)
    both = oh0 + oh1

    @pl.when((ph == 0) & (i == 0))
    def _():
        carry[...] = jnp.zeros(carry.shape, F32)

    @pl.when(ph == 0)
    def _():
        carry[...] += jnp.sum(both, 0, keepdims=True)

    @pl.when((ph == 1) & (i == 0))
    def _():
        padded = jnp.floor((carry[...] + (MOE_BLOCK - 1)) * (1.0 / MOE_BLOCK)) * MOE_BLOCK
        a = lax.broadcasted_iota(jnp.int32, (LANES, LANES), 0)
        b = lax.broadcasted_iota(jnp.int32, (LANES, LANES), 1)
        upper = (a < b).astype(F32)
        ps = jnp.dot(jnp.broadcast_to(padded, (SUBLANES, LANES)), upper, preferred_element_type=F32,
                     precision=lax.Precision.HIGHEST)
        pstart[...] = ps[0:1]
        carry[...] = jnp.zeros(carry.shape, F32)

    @pl.when(ph == 1)
    def _():
        r = lax.broadcasted_iota(jnp.int32, (tm, tm), 0)
        c = lax.broadcasted_iota(jnp.int32, (tm, tm), 1)
        before = (r > c).astype(BF16)
        base = _dot(before, both.astype(BF16)) + carry[...] + pstart[...]
        d0 = jnp.sum(oh0 * base, -1, keepdims=True)
        d1 = jnp.sum(oh1 * base, -1, keepdims=True)
        first = lax.broadcasted_iota(jnp.int32, (tm, 2), 1) == 0
        dest_ref[...] = jnp.where(first, d0, d1).astype(jnp.int32)
        carry[...] += jnp.sum(both, 0, keepdims=True)
        padded_tot = jnp.floor((carry[...] + (MOE_BLOCK - 1)) * (1.0 / MOE_BLOCK)) * MOE_BLOCK
        pend_ref[...] = jnp.broadcast_to(pstart[...] + padded_tot, pend_ref.shape)


def _plan(idx, *, tm):
    T = idx.shape[0]
    return pl.pallas_call(
        _plan_body,
        grid=(2, T // tm),
        in_specs=[pl.BlockSpec((tm, 2), lambda p, i: (i, 0))],
        out_specs=[pl.BlockSpec((tm, 2), lambda p, i: (i * p, 0)), pl.BlockSpec((SUBLANES, LANES), lambda p, i: (0, 0))],
        out_shape=[jax.ShapeDtypeStruct((T, 2), jnp.int32), jax.ShapeDtypeStruct((SUBLANES, LANES), F32)],
        scratch_shapes=[pltpu.VMEM((1, LANES), F32), pltpu.VMEM((1, LANES), F32)],
        compiler_params=_cparams(2),
        name="plan",
    )(idx)


def _row_copy(src_ref, s, dst_ref, d, sem):
    return pltpu.make_async_copy(src_ref.at[pl.ds(s, 1), :], dst_ref.at[pl.ds(d, 1), :], sem)


def _dispatch_body(dest_ref, x_ref, xs_in_ref, xs_ref, sem):
    del xs_in_ref
    tm = x_ref.shape[0]

    def start(t, c):
        _row_copy(x_ref, t, xs_ref, dest_ref[0, 0, 2 * t], sem).start()
        _row_copy(x_ref, t, xs_ref, dest_ref[0, 0, 2 * t + 1], sem).start()
        return c

    lax.fori_loop(0, tm, start, 0, unroll=ROW_DMA_UNROLL)
    for _ in range(2):
        pltpu.make_async_copy(x_ref, xs_ref.at[pl.ds(0, tm), :], sem).wait()


def _dispatch(dest, x, xs, *, tm):
    T = x.shape[0]
    nt = T // tm
    return pl.pallas_call(
        _dispatch_body,
        grid=(nt,),
        in_specs=[
            pl.BlockSpec((1, 1, 2 * tm), lambda i: (i, 0, 0), memory_space=pltpu.SMEM),
            pl.BlockSpec((tm, D_MODEL), lambda i: (i, 0)),
            pl.BlockSpec(memory_space=pl.ANY),
        ],
        out_specs=pl.BlockSpec(memory_space=pl.ANY),
        out_shape=jax.ShapeDtypeStruct(xs.shape, xs.dtype),
        scratch_shapes=[pltpu.SemaphoreType.DMA(())],
        input_output_aliases={2: 0},
        compiler_params=_cparams(1),
        name="dispatch",
    )(dest.reshape(nt, 1, 2 * tm), x, xs)


def _combine_body(dest_ref, x_ref, ew_ref, g_ref, b_ref, rows_ref, o_ref, gbuf, sem):
    tm = x_ref.shape[0]

    def start(t, c):
        _row_copy(rows_ref, dest_ref[0, 0, 2 * t], gbuf.at[0], t, sem).start()
        _row_copy(rows_ref, dest_ref[0, 0, 2 * t + 1], gbuf.at[1], t, sem).start()
        return c

    lax.fori_loop(0, tm, start, 0, unroll=ROW_DMA_UNROLL)
    for half in range(2):
        pltpu.make_async_copy(rows_ref.at[pl.ds(0, tm), :], gbuf.at[half], sem).wait()
    y = ew_ref[:, 0:1] * gbuf[0] + ew_ref[:, 1:2] * gbuf[1]
    o_ref[...] = _layer_norm(ALPHA * x_ref[...] + y, g_ref[...], b_ref[...])


def _combine(dest, x, ew, g, b, rows, *, tm):
    T = x.shape[0]
    nt = T // tm
    const = lambda i: (0, 0)
    return pl.pallas_call(
        _combine_body,
        grid=(nt,),
        in_specs=[
            pl.BlockSpec((1, 1, 2 * tm), lambda i: (i, 0, 0), memory_space=pltpu.SMEM),
            pl.BlockSpec((tm, D_MODEL), lambda i: (i, 0)),
            pl.BlockSpec((tm, 2), lambda i: (i, 0)),
            pl.BlockSpec((1, D_MODEL), const),
            pl.BlockSpec((1, D_MODEL), const),
            pl.BlockSpec(memory_space=pl.ANY),
        ],
        out_specs=pl.BlockSpec((tm, D_MODEL), lambda i: (i, 0)),
        out_shape=jax.ShapeDtypeStruct((T, D_MODEL), F32),
        scratch_shapes=[pltpu.VMEM((2, tm, D_MODEL), F32), pltpu.SemaphoreType.DMA(())],
        compiler_params=_cparams(1),
        name="combine",
    )(dest.reshape(nt, 1, 2 * tm), x, ew, g, b, rows)


def _experts_body(be_ref, nu_ref, xs_ref, wg_ref, wu_ref, wd_ref, o_ref):
    del be_ref
    i = pl.program_id(0)

    @pl.when(i < nu_ref[0])
    def _():
        xb = xs_ref[...].astype(BF16)
        gate = _dot(xb, wg_ref[0].astype(BF16))
        up = _dot(xb, wu_ref[0].astype(BF16))
        hid = gate * jax.nn.sigmoid(gate) * up
        o_ref[...] = _dot(hid.astype(BF16), wd_ref[0].astype(BF16))

    @pl.when(i >= nu_ref[0])
    def _():
        o_ref[...] = jnp.zeros(o_ref.shape, F32)


def _experts(block_expert, n_used, xs, w_gate, w_up, w_down):
    R = xs.shape[0]
    nb = R // MOE_BLOCK
    grid_spec = pltpu.PrefetchScalarGridSpec(
        num_scalar_prefetch=2,
        grid=(nb,),
        in_specs=[
            pl.BlockSpec((MOE_BLOCK, D_MODEL), lambda i, be, nu: (i, 0)),
            pl.BlockSpec((1, D_MODEL, MOE_FF), lambda i, be, nu: (be[i], 0, 0)),
            pl.BlockSpec((1, D_MODEL, MOE_FF), lambda i, be, nu: (be[i], 0, 0)),
            pl.BlockSpec((1, MOE_FF, D_MODEL), lambda i, be, nu: (be[i], 0, 0)),
        ],
        out_specs=pl.BlockSpec((MOE_BLOCK, D_MODEL), lambda i, be, nu: (i, 0)),
    )
    return pl.pallas_call(
        _experts_body,
        grid_spec=grid_spec,
        out_shape=jax.ShapeDtypeStruct((R, D_MODEL), F32),
        compiler_params=_cparams(1),
        name="experts",
    )(block_expert, n_used, xs, w_gate, w_up, w_down)


def _row_tile(T, pref):
    t = pref
    while T % t:
        t //= 2
    return t


def _group_to_x2(x, mem_k, mem_v, conv_prev, pool_prev, s0, pos0, W):
    B, L, _ = x.shape
    T = B * L
    xf = x.reshape(T, D_MODEL)
    tm = _row_tile(T, 512)
    mixer_w = (W["w_conv"], W["a_log"], W["dt_bias"], W["w_onorm"], W["w_pool"], W["pool_scale"])
    if L >= FUSE_MIN_SEQ:
        merged, s_new, conv_tail, pool_tail = _mixer(xf, B, L, conv_prev, pool_prev, s0, *mixer_w, pos0=pos0,
                                                     w_in=W["w_in"])
    else:
        proj = _mm(xf, W["w_in"], tm=tm, tn=N_PROJ // 3)
        merged, s_new, conv_tail, pool_tail = _mixer(proj, B, L, conv_prev, pool_prev, s0, *mixer_w, pos0=pos0)
    x1, q = _proj_ln(merged, W["w_out"], xf, W["ln1_g"], W["ln1_b"], (W["w_cq"],), tm=tm, route=False)
    if mem_k.ndim == 3:
        att = _attention(q, mem_k, mem_v, B, L)
    else:
        M = mem_k.shape[1]

        def rows_view(m):
            m = m.reshape(B, M, CA_HEADS, HALVES, LANES)
            return jnp.swapaxes(m, 2, 3).reshape(B * M * KV_ROWS, LANES)

        att = _attention_rows(q, rows_view(mem_k), rows_view(mem_v), B, L, M)
    x2, idx, ew = _proj_ln(att, W["w_co"], x1, W["ln2_g"], W["ln2_b"], (W["w_r_hi"], W["w_r_lo"]), tm=tm, route=True)
    idx, ew = idx[0:2].T, ew[0:2].T
    return x2, idx, ew, s_new, conv_tail[:, -(CONV_W - 1):], pool_tail[:, -(POOL_MAX - 1):]


def kernel(x_prompt, x_sample, cache_mem_k, cache_mem_v, state_delta, state_conv, state_pool, mem_prompt, w_in, w_conv, a_log, dt_bias, w_onorm, w_pool, pool_scale, w_out, ln1_g, ln1_b, w_cq, w_ck, w_cv, w_co, ln2_g, ln2_b, w_router_group, w_router_expert, w_gate, w_up, w_down, ln3_g, ln3_b):
    Bp, Lp, _ = x_prompt.shape
    Bs, Ls, _ = x_sample.shape
    Tp, Ts = Bp * Lp, Bs * Ls
    lyr = 0

    def pad_lanes(v):
        return jnp.pad(v.astype(F32), (0, LANES - v.shape[0])).reshape(1, LANES)

    o1, o2, o3, o4 = QKV_DIM, QKV_DIM + D_MODEL, QKV_DIM + D_MODEL + 2 * DN_HEADS, QKV_DIM + 2 * D_MODEL + 2 * DN_HEADS
    wi = w_in[lyr]
    w_in_r = jnp.concatenate(
        [wi[:, :o2], wi[:, o3:], wi[:, o2:o3], jnp.zeros((D_MODEL, LANES - 2 * DN_HEADS), F32)], axis=1).astype(BF16)
    del o1, o4
    w_r = jnp.concatenate([w_router_group[lyr], w_router_expert[lyr],
                           jnp.zeros((D_MODEL, LANES - MOE_GROUPS - MOE_EXPERTS), F32)], axis=1).T
    w_r_hi = w_r.astype(BF16)
    W = {
        "w_r_hi": w_r_hi,
        "w_r_lo": (w_r - w_r_hi.astype(F32)).astype(BF16),
        "w_in": w_in_r,
        "w_conv": w_conv[lyr],
        "a_log": pad_lanes(a_log[lyr]),
        "dt_bias": pad_lanes(dt_bias[lyr]),
        "w_onorm": w_onorm[lyr].reshape(1, DN_HD),
        "w_pool": w_pool[lyr].astype(BF16),
        "pool_scale": pool_scale[lyr].reshape(1, D_MODEL),
        "w_out": w_out[lyr].astype(BF16),
        "ln1_g": ln1_g[lyr].reshape(1, D_MODEL), "ln1_b": ln1_b[lyr].reshape(1, D_MODEL),
        "w_cq": w_cq[lyr].astype(BF16),
        "w_co": w_co[lyr].astype(BF16),
        "ln2_g": ln2_g[lyr].reshape(1, D_MODEL), "ln2_b": ln2_b[lyr].reshape(1, D_MODEL),
    }

    M = mem_prompt.shape[1]
    memf = mem_prompt.reshape(Bp * M, D_MODEL)
    tmm = _row_tile(Bp * M, 512)
    mk = _mm(memf, w_ck[lyr].astype(BF16), tm=tmm, tn=D_MODEL).reshape(Bp, M, D_MODEL)
    mv = _mm(memf, w_cv[lyr].astype(BF16), tm=tmm, tn=D_MODEL).reshape(Bp, M, D_MODEL)
    x2_p, idx_p, ew_p, d_p, c_p, p_p = _group_to_x2(x_prompt, mk, mv, None, None, None, 0, W)
    mk = mk.reshape(Bp, M, CA_HEADS, CA_HD)
    mv = mv.reshape(Bp, M, CA_HEADS, CA_HD)
    x2_s, idx_s, ew_s, d_s, c_s, p_s = _group_to_x2(x_sample, cache_mem_k[lyr], cache_mem_v[lyr], state_conv[lyr],
                                       state_pool[lyr], state_delta[lyr], PAST_LEN, W)

    tp, ts = _row_tile(Tp, 256), _row_tile(Ts, 256)
    T = Tp + Ts
    dest, pend = _plan(jnp.concatenate([idx_p, idx_s], axis=0), tm=_row_tile(T, 1024))
    n_blocks = (2 * T + MOE_EXPERTS * (MOE_BLOCK - 1) + MOE_BLOCK - 1) // MOE_BLOCK
    pend_e = pend[0, :MOE_EXPERTS].astype(jnp.int32)
    block_start = jnp.arange(n_blocks, dtype=jnp.int32) * MOE_BLOCK
    block_expert = jnp.minimum(jnp.sum(block_start[:, None] >= pend_e[None, :], axis=1), MOE_EXPERTS - 1).astype(jnp.int32)
    n_used = (pend_e[MOE_EXPERTS - 1:] // MOE_BLOCK).astype(jnp.int32)
    xs = jnp.zeros((n_blocks * MOE_BLOCK, D_MODEL), F32)
    xs = _dispatch(dest[:Tp], x2_p, xs, tm=tp)
    xs = _dispatch(dest[Tp:], x2_s, xs, tm=ts)
    rows = _experts(block_expert, n_used, xs, w_gate[lyr], w_up[lyr], w_down[lyr])
    g3, b3 = ln3_g[lyr].reshape(1, D_MODEL), ln3_b[lyr].reshape(1, D_MODEL)
    y_p = _combine(dest[:Tp], x2_p, ew_p, g3, b3, rows, tm=tp).reshape(Bp, Lp, D_MODEL)
    y_s = _combine(dest[Tp:], x2_s, ew_s, g3, b3, rows, tm=ts).reshape(Bs, Ls, D_MODEL)

    return (y_p, y_s, d_p[None], c_p[None], p_p[None], mk[None], mv[None], d_s[None], c_s[None], p_s[None])
```

```python
import functools
import math

import jax
import jax.numpy as jnp
from jax import lax
from jax.experimental import pallas as pl
from jax.experimental.pallas import tpu as pltpu

F32 = jnp.float32
BF16 = jnp.bfloat16

D_MODEL = 1024
DN_HEADS = 8
DN_HD = 128
QKV_DIM = 3 * DN_HEADS * DN_HD
CONV_W = 4
DN_CHUNK = 64
POOL_WINDOWS = (2, 4, 8, 16)
POOL_GD = D_MODEL // len(POOL_WINDOWS)
POOL_MAX = 16
CA_HEADS = 4
CA_HD = D_MODEL // CA_HEADS
MOE_GROUPS = 8
MOE_EPG = 8
MOE_EXPERTS = MOE_GROUPS * MOE_EPG
MOE_FF = D_MODEL // 4
MOE_BLOCK = 256
PAST_LEN = 16384
LN_EPS = 1e-5
NORM_EPS = 1e-6
ALPHA = 2.0 ** 0.25

LANES = 128
SUBLANES = 8
CONV_HALO = SUBLANES
POOL_HALO = POOL_MAX
MIN_CHUNK = 16
N_PROJ = 7 * D_MODEL + LANES
VMEM_LIMIT = 48 * 1024 * 1024
MIXER_VMEM_LIMIT = 56 * 1024 * 1024
ROW_DMA_UNROLL = 8
FUSE_MIN_SEQ = 128
SEQS_PER_STEP = 8


def _cparams(n_axes, vmem=VMEM_LIMIT):
    return pltpu.CompilerParams(dimension_semantics=("arbitrary",) * n_axes, vmem_limit_bytes=vmem)


def _layer_norm(x, g, b):
    mu = jnp.mean(x, -1, keepdims=True)
    xc = x - mu
    var = jnp.mean(xc * xc, -1, keepdims=True)
    return xc * lax.rsqrt(var + LN_EPS) * g + b


def _softplus(x):
    return jnp.maximum(x, 0.0) + jnp.log1p(jnp.exp(-jnp.abs(x)))


def _dot(a, b):
    return jnp.dot(a, b, preferred_element_type=F32)


def _dot_nt(a, b):
    return lax.dot_general(a, b, (((1,), (1,)), ((), ())), preferred_element_type=F32)


def _dot_tn(a, b):
    return lax.dot_general(a, b, (((0,), (0,)), ((), ())), preferred_element_type=F32)


def _mm_body(x_ref, w_ref, o_ref):
    o_ref[...] = _dot(x_ref[...].astype(BF16), w_ref[...]).astype(o_ref.dtype)


def _mm(x, w, *, tm, tn, out_dtype=F32):
    T, K = x.shape
    N = w.shape[1]
    return pl.pallas_call(
        _mm_body,
        grid=(N // tn, T // tm),
        in_specs=[pl.BlockSpec((tm, K), lambda j, i: (i, 0)), pl.BlockSpec((K, tn), lambda j, i: (0, j))],
        out_specs=pl.BlockSpec((tm, tn), lambda j, i: (i, j)),
        out_shape=jax.ShapeDtypeStruct((T, N), out_dtype),
        compiler_params=_cparams(2),
        name="mm",
    )(x, w)


def _mixer_body(*refs, tl, chunk, pos0, has_state, fused, nb):
    refs = list(refs)
    take = lambda n: [refs.pop(0) for _ in range(n)]
    if fused:
        x_ref, win_hbm = take(2)
    else:
        qkv_ref, z_ref, p_ref, ga_ref, gb_ref, ab_ref = take(6)
    if has_state:
        convp_ref, poolp_ref, s0_ref = take(3)
    wconv_ref, alog_ref, dtb_ref, wonorm_ref, wpool_ref, pscale_ref = take(6)
    merged_ref, sout_ref, convo_ref, poolo_ref = take(4)
    qkvbuf, pbuf, obuf, bbuf, gcb, betab, gt_ref = take(7)
    if fused:
        z_ref, ga_ref, gb_ref, win_vmem, win_sem = take(5)
    b = pl.program_id(0)
    l = pl.program_id(1)
    tlp = max(tl, MIN_CHUNK)
    C = max(chunk, MIN_CHUNK)
    cps = tlp // C
    seqs = range(nb)

    @pl.when(l == 0)
    def _init():
        for s in seqs:
            qkvbuf[s, 0:CONV_HALO, :] = jnp.zeros((CONV_HALO, QKV_DIM), F32)
            pbuf[s, 0:POOL_HALO, :] = jnp.zeros((POOL_HALO, D_MODEL), F32)
            if has_state:
                qkvbuf[s, CONV_HALO - (CONV_W - 1):CONV_HALO, :] = convp_ref[s]
                pbuf[s, POOL_HALO - (POOL_MAX - 1):POOL_HALO, :] = poolp_ref[s]
        if has_state:
            sout_ref[...] = s0_ref[...]
        else:
            sout_ref[...] = jnp.zeros(sout_ref.shape, F32)

    if fused:
        @pl.when((b == 0) & (l == 0))
        def _load_weight():
            cp = pltpu.make_async_copy(win_hbm, win_vmem, win_sem)
            cp.start()
            cp.wait()

        xb = x_ref[...].astype(BF16)

        def proj_cols(c0, width):
            return _dot(xb, win_vmem[:, c0:c0 + width])

        ab = proj_cols(7 * D_MODEL, LANES)
        for c in range(QKV_DIM // D_MODEL):
            qkvbuf[0, CONV_HALO:CONV_HALO + tl, c * D_MODEL:(c + 1) * D_MODEL] = proj_cols(c * D_MODEL, D_MODEL)

        def _p():
            pbuf[0, POOL_HALO:POOL_HALO + tl, :] = proj_cols(4 * D_MODEL, D_MODEL)

        def _z():
            zz = proj_cols(3 * D_MODEL, D_MODEL)
            z_ref[...] = zz * jax.nn.sigmoid(zz)

        def _ga():
            ga_ref[...] = jax.nn.sigmoid(proj_cols(5 * D_MODEL, D_MODEL))

        def _gb():
            gb_ref[...] = jax.nn.sigmoid(proj_cols(6 * D_MODEL, D_MODEL))

        late_proj = [_p, _z, _ga, _gb]
        silu_z = lambda cols: z_ref[:, cols]
        gate_a = lambda cols: ga_ref[:, cols]
        gate_b = lambda cols: gb_ref[:, cols]
    else:
        late_proj = []
        silu_z = lambda cols: z_ref[:, cols] * jax.nn.sigmoid(z_ref[:, cols])
        gate_a = lambda cols: jax.nn.sigmoid(ga_ref[:, cols])
        gate_b = lambda cols: jax.nn.sigmoid(gb_ref[:, cols])
        ab = ab_ref[...]
        for s in seqs:
            qkvbuf[s, CONV_HALO:CONV_HALO + tl, :] = qkv_ref[s * tl:(s + 1) * tl, :]
            pbuf[s, POOL_HALO:POOL_HALO + tl, :] = p_ref[s * tl:(s + 1) * tl, :]

    def pad_seq(pieces):
        out = []
        for x in pieces:
            out.append(x)
            if tlp != tl:
                out.append(jnp.zeros((tlp - tl, x.shape[1]), F32))
        return out[0] if len(out) == 1 else jnp.concatenate(out, axis=0)

    def unpad_seq(x):
        if tlp == tl:
            return x
        return jnp.concatenate([x[s * tlp:s * tlp + tl] for s in seqs], axis=0)

    g_raw = -jnp.exp(alog_ref[...]) * _softplus(ab + dtb_ref[...])
    beta_raw = jax.nn.sigmoid(ab)
    g_all = pad_seq([g_raw[s * tl:(s + 1) * tl] for s in seqs])
    beta_all = pad_seq([beta_raw[s * tl:(s + 1) * tl] for s in seqs])
    R = nb * tlp
    ri = lax.broadcasted_iota(jnp.int32, (R, R), 0)
    ci = lax.broadcasted_iota(jnp.int32, (R, R), 1)
    cum_mat = ((ri >= ci) & ((ri // C) == (ci // C))).astype(F32)
    gcum = jnp.dot(cum_mat, g_all, preferred_element_type=F32, precision=lax.Precision.HIGHEST)
    gcum_t = gcum.T
    for h in range(DN_HEADS):
        gt_ref[h] = jnp.broadcast_to(gcum_t[h:h + 1, :], (SUBLANES, R))
        gcb[h] = jnp.broadcast_to(gcum[:, h:h + 1], (R, LANES))
        betab[h] = jnp.broadcast_to(beta_all[:, DN_HEADS + h:DN_HEADS + h + 1], (R, LANES))

    ii = lax.broadcasted_iota(jnp.int32, (C, C), 0)
    jj = lax.broadcasted_iota(jnp.int32, (C, C), 1)
    causal = ii >= jj
    strict = ii > jj
    eye = (ii == jj).astype(F32)
    n_sq = max(C.bit_length() - 2, 0)

    def conv_slab(col0):
        cols = slice(col0, col0 + DN_HD)
        pieces = []
        for s in seqs:
            acc = qkvbuf[s, CONV_HALO:CONV_HALO + tl, cols] * wconv_ref[CONV_W - 1:CONV_W, cols]
            for i in range(CONV_W - 1):
                r0 = CONV_HALO - (CONV_W - 1) + i
                acc = acc + qkvbuf[s, r0:r0 + tl, cols] * wconv_ref[i:i + 1, cols]
            pieces.append(acc * jax.nn.sigmoid(acc))
        return pad_seq(pieces)

    heads = range(DN_HEADS)
    chunks = range(nb * cps)
    probs = [(h, c) for h in heads for c in chunks]
    rows = [slice(c * C, (c + 1) * C) for c in chunks]
    q, k, v, gcs, bs, egs, kbs, grows = [], [], [], [], [], [], [], []
    for h in heads:
        qh = conv_slab(h * DN_HD)
        kh = conv_slab((DN_HEADS + h) * DN_HD)
        v.append(conv_slab((2 * DN_HEADS + h) * DN_HD))
        q.append(qh * lax.rsqrt(jnp.sum(qh * qh, -1, keepdims=True) + NORM_EPS) * (DN_HD ** -0.5))
        kh = kh * lax.rsqrt(jnp.sum(kh * kh, -1, keepdims=True) + NORM_EPS)
        k.append(kh)
        gcs.append(gcb[h])
        bs.append(betab[h])
        egs.append(jnp.exp(gcs[h]))
        kbs.append(kh * bs[h])
        grows.append(gt_ref[h][0:1, :])

    def emit_late_proj():
        if late_proj:
            late_proj.pop(0)()

    aq = {(h, c): _dot_nt(jnp.concatenate([kbs[h][rows[c]], q[h][rows[c]]], axis=0).astype(BF16),
                          k[h][rows[c]].astype(BF16)) for h, c in probs}
    emit_late_proj()

    if pos0 == 0:
        pos = l * tl + lax.broadcasted_iota(jnp.int32, (tl, 1), 0)
    for gi, win in enumerate(POOL_WINDOWS):
        cs = slice(gi * POOL_GD, (gi + 1) * POOL_GD)
        pooled = []
        for s in seqs:
            acc = pbuf[s, POOL_HALO:POOL_HALO + tl, cs]
            for j in range(1, win):
                acc = acc + pbuf[s, POOL_HALO - j:POOL_HALO - j + tl, cs]
            if pos0 == 0:
                cnt = jnp.minimum(win, pos + 1).astype(F32)
            else:
                cnt = float(min(win, pos0 + 1))
            pooled.append(acc / cnt - pbuf[s, POOL_HALO:POOL_HALO + tl, cs])
        pooled = pooled[0] if nb == 1 else jnp.concatenate(pooled, axis=0)
        bbuf[:, cs] = _dot(pooled.astype(BF16), wpool_ref[gi]) * pscale_ref[:, cs]
    emit_late_proj()

    decay = {(h, c): jnp.exp(jnp.where(causal, gcs[h][rows[c], 0:C] - grows[h][:, c * C:(c + 1) * C], -jnp.inf))
             for h, c in probs}
    A = {p: jnp.where(strict, aq[p][0:C] * decay[p], 0.0) for p in probs}
    qk = {p: (aq[p][C:2 * C] * decay[p]).astype(BF16) for p in probs}
    P = {p: eye - A[p] for p in probs}
    Q = {p: _dot(A[p].astype(BF16), A[p].astype(BF16)) for p in probs}
    emit_late_proj()
    for _ in range(n_sq - 1):
        pq = {p: _dot(jnp.concatenate([P[p], Q[p]], axis=0).astype(BF16), Q[p].astype(BF16)) for p in probs}
        emit_late_proj()
        P = {p: P[p] + pq[p][0:C] for p in probs}
        Q = {p: pq[p][C:2 * C] for p in probs}
    P = {p: P[p] + _dot(P[p].astype(BF16), Q[p].astype(BF16)) for p in probs}
    while late_proj:
        emit_late_proj()
    sol = {(h, c): _dot(P[(h, c)].astype(BF16),
                        jnp.concatenate([v[h][rows[c]] * bs[h][rows[c]], kbs[h][rows[c]] * egs[h][rows[c]]],
                                        axis=1).astype(BF16)) for h, c in probs}

    S = [[sout_ref[s, h] for h in heads] for s in seqs]
    sh = [(s, h) for s in seqs for h in heads]
    outs = {}
    for j in range(cps):
        cof = {s: s * cps + j for s in seqs}
        ws_qs = {(s, h): _dot(jnp.concatenate([sol[(h, cof[s])][:, DN_HD:2 * DN_HD],
                                               q[h][rows[cof[s]]] * egs[h][rows[cof[s]]]], axis=0).astype(BF16),
                              S[s][h].astype(BF16)) for s, h in sh}
        v_new = {(s, h): (sol[(h, cof[s])][:, 0:DN_HD] - ws_qs[(s, h)][0:C]).astype(BF16) for s, h in sh}
        glast = {(s, h): gcs[h][rows[cof[s]]][C - 1:C, :] for s, h in sh}
        S_new = {(s, h): S[s][h] * jnp.exp(glast[(s, h)])
                 + _dot_tn((k[h][rows[cof[s]]] * jnp.exp(glast[(s, h)] - gcs[h][rows[cof[s]]])).astype(BF16),
                           v_new[(s, h)]) for s, h in sh}
        S = [[S_new[(s, h)] for h in heads] for s in seqs]
        for s, h in sh:
            outs[(h, cof[s])] = ws_qs[(s, h)][C:2 * C] + _dot(qk[(h, cof[s])], v_new[(s, h)])
    for s, h in sh:
        sout_ref[s, h] = S[s][h]
    for h in heads:
        o = outs[(h, 0)] if len(chunks) == 1 else jnp.concatenate([outs[(h, c)] for c in chunks], axis=0)
        o = unpad_seq(o)
        o = o * lax.rsqrt(jnp.mean(o * o, -1, keepdims=True) + NORM_EPS) * wonorm_ref[...]
        hc = slice(h * DN_HD, (h + 1) * DN_HD)
        obuf[:, hc] = o * silu_z(hc)

    for gi in range(len(POOL_WINDOWS)):
        cs = slice(gi * POOL_GD, (gi + 1) * POOL_GD)
        merged_ref[:, cs] = (gate_a(cs) * obuf[:, cs] + gate_b(cs) * bbuf[:, cs]).astype(merged_ref.dtype)

    for s in seqs:
        conv_tail = qkvbuf[s, tl:tl + CONV_HALO, :]
        pool_tail = pbuf[s, tl:tl + POOL_HALO, :]
        convo_ref[s] = conv_tail
        poolo_ref[s] = pool_tail
        if tl >= POOL_HALO:
            qkvbuf[s, 0:CONV_HALO, :] = conv_tail
            pbuf[s, 0:POOL_HALO, :] = pool_tail


def _mixer(src, B, L, conv_prev, pool_prev, s0, wconv, alog, dtb, wonorm, wpool, pscale, *, pos0, w_in=None):
    tl = min(256, L)
    nL = L // tl
    assert nL == 1 or tl >= POOL_HALO
    chunk = min(DN_CHUNK, tl)
    tlp = max(tl, MIN_CHUNK)
    has_state = s0 is not None
    fused = w_in is not None
    nb = next(n for n in (SEQS_PER_STEP, 2, 1) if B % n == 0) if (nL == 1 and not fused) else 1
    rt = nb * tl
    row = lambda b, l: b * nL + l
    if fused:
        in_specs = [pl.BlockSpec((rt, D_MODEL), lambda b, l: (row(b, l), 0)), pl.BlockSpec(memory_space=pl.ANY)]
        args = [src, w_in]
    else:
        in_specs = [
            pl.BlockSpec((rt, QKV_DIM), lambda b, l: (row(b, l), 0)),
            pl.BlockSpec((rt, D_MODEL), lambda b, l: (row(b, l), 3)),
            pl.BlockSpec((rt, D_MODEL), lambda b, l: (row(b, l), 4)),
            pl.BlockSpec((rt, D_MODEL), lambda b, l: (row(b, l), 5)),
            pl.BlockSpec((rt, D_MODEL), lambda b, l: (row(b, l), 6)),
            pl.BlockSpec((rt, LANES), lambda b, l: (row(b, l), 7 * D_MODEL // LANES)),
        ]
        args = [src] * 6
    if has_state:
        in_specs += [
            pl.BlockSpec((nb, CONV_W - 1, QKV_DIM), lambda b, l: (b, 0, 0)),
            pl.BlockSpec((nb, POOL_MAX - 1, D_MODEL), lambda b, l: (b, 0, 0)),
            pl.BlockSpec((nb, DN_HEADS, DN_HD, DN_HD), lambda b, l: (b, 0, 0, 0)),
        ]
        args += [conv_prev, pool_prev, s0]
    const2 = lambda b, l: (0, 0)
    in_specs += [
        pl.BlockSpec((CONV_W, QKV_DIM), const2),
        pl.BlockSpec((1, LANES), const2),
        pl.BlockSpec((1, LANES), const2),
        pl.BlockSpec((1, DN_HD), const2),
        pl.BlockSpec((len(POOL_WINDOWS), POOL_GD, POOL_GD), lambda b, l: (0, 0, 0)),
        pl.BlockSpec((1, D_MODEL), const2),
    ]
    args += [wconv, alog, dtb, wonorm, wpool, pscale]
    scratch = [
        pltpu.VMEM((nb, CONV_HALO + tl, QKV_DIM), F32),
        pltpu.VMEM((nb, POOL_HALO + tl, D_MODEL), F32),
        pltpu.VMEM((rt, D_MODEL), F32),
        pltpu.VMEM((rt, D_MODEL), F32),
        pltpu.VMEM((DN_HEADS, nb * tlp, LANES), F32),
        pltpu.VMEM((DN_HEADS, nb * tlp, LANES), F32),
        pltpu.VMEM((DN_HEADS, SUBLANES, nb * tlp), F32),
    ]
    if fused:
        scratch += [pltpu.VMEM((tl, D_MODEL), F32)] * 3
        scratch += [pltpu.VMEM(w_in.shape, w_in.dtype), pltpu.SemaphoreType.DMA(())]
    return pl.pallas_call(
        functools.partial(_mixer_body, tl=tl, chunk=chunk, pos0=pos0, has_state=has_state, fused=fused, nb=nb),
        grid=(B // nb, nL),
        in_specs=in_specs,
        out_specs=[
            pl.BlockSpec((rt, D_MODEL), lambda b, l: (row(b, l), 0)),
            pl.BlockSpec((nb, DN_HEADS, DN_HD, DN_HD), lambda b, l: (b, 0, 0, 0)),
            pl.BlockSpec((nb, CONV_HALO, QKV_DIM), lambda b, l: (b, 0, 0)),
            pl.BlockSpec((nb, POOL_HALO, D_MODEL), lambda b, l: (b, 0, 0)),
        ],
        out_shape=[
            jax.ShapeDtypeStruct((B * L, D_MODEL), BF16),
            jax.ShapeDtypeStruct((B, DN_HEADS, DN_HD, DN_HD), F32),
            jax.ShapeDtypeStruct((B, CONV_HALO, QKV_DIM), F32),
            jax.ShapeDtypeStruct((B, POOL_HALO, D_MODEL), F32),
        ],
        scratch_shapes=scratch,
        compiler_params=_cparams(2, vmem=MIXER_VMEM_LIMIT),
        name="mixer",
    )(*args)


def _proj_ln_q_body(a_ref, w_ref, r_ref, g_ref, b_ref, wq_ref, o_ref, q_ref):
    y = ALPHA * r_ref[...] + _dot(a_ref[...], w_ref[...])
    x1 = _layer_norm(y, g_ref[...], b_ref[...])
    o_ref[...] = x1
    q_ref[...] = _dot(x1.astype(BF16), wq_ref[...]).astype(q_ref.dtype)


def _proj_ln_route_body(a_ref, w_ref, r_ref, g_ref, b_ref, whi_ref, wlo_ref, o_ref, idx_ref, ew_ref):
    y = ALPHA * r_ref[...] + _dot(a_ref[...], w_ref[...])
    x2 = _layer_norm(y, g_ref[...], b_ref[...])
    o_ref[...] = x2
    idx_ref[...], ew_ref[...] = _route(x2, whi_ref[...], wlo_ref[...])


def _proj_ln(a, w, resid, g, b, tail, *, tm, route):
    T = a.shape[0]
    const = lambda i: (0, 0)
    rows = lambda i: (i, 0)
    in_specs = [
        pl.BlockSpec((tm, D_MODEL), rows),
        pl.BlockSpec((D_MODEL, D_MODEL), const),
        pl.BlockSpec((tm, D_MODEL), rows),
        pl.BlockSpec((1, D_MODEL), const),
        pl.BlockSpec((1, D_MODEL), const),
    ] + [pl.BlockSpec(t.shape, const) for t in tail]
    out_specs = [pl.BlockSpec((tm, D_MODEL), rows)]
    out_shape = [jax.ShapeDtypeStruct((T, D_MODEL), F32)]
    if route:
        cols = lambda i: (0, i)
        out_specs += [pl.BlockSpec((SUBLANES, tm), cols), pl.BlockSpec((SUBLANES, tm), cols)]
        out_shape += [jax.ShapeDtypeStruct((SUBLANES, T), jnp.int32), jax.ShapeDtypeStruct((SUBLANES, T), F32)]
    else:
        out_specs += [pl.BlockSpec((tm, D_MODEL), rows)]
        out_shape += [jax.ShapeDtypeStruct((T, D_MODEL), BF16)]
    return pl.pallas_call(
        _proj_ln_route_body if route else _proj_ln_q_body,
        grid=(T // tm,),
        in_specs=in_specs,
        out_specs=out_specs,
        out_shape=out_shape,
        compiler_params=_cparams(1),
        name="proj_ln_route" if route else "proj_ln_q",
    )(a, w, resid, g, b, *tail)


def _attn_body(q_ref, k_ref, v_ref, o_ref):
    scale = CA_HD ** -0.5
    for hh in range(CA_HEADS):
        cs = slice(hh * CA_HD, (hh + 1) * CA_HD)
        s = _dot_nt(q_ref[:, cs].astype(BF16), k_ref[0, :, cs].astype(BF16)) * scale
        m = jnp.max(s, -1, keepdims=True)
        p = jnp.exp(s - m)
        denom = jnp.sum(p, -1, keepdims=True)
        o = _dot(p.astype(BF16), v_ref[0, :, cs].astype(BF16)) / denom
        o_ref[:, cs] = o.astype(o_ref.dtype)


def _attention(q, mem_k, mem_v, B, L):
    tq = min(512, L)
    nq = L // tq
    M = mem_k.shape[1]
    kv_spec = pl.BlockSpec((1, M, D_MODEL), lambda b, i: (b, 0, 0))
    return pl.pallas_call(
        _attn_body,
        grid=(B, nq),
        in_specs=[pl.BlockSpec((tq, D_MODEL), lambda b, i: (b * nq + i, 0)), kv_spec, kv_spec],
        out_specs=pl.BlockSpec((tq, D_MODEL), lambda b, i: (b * nq + i, 0)),
        out_shape=jax.ShapeDtypeStruct((B * L, D_MODEL), BF16),
        compiler_params=_cparams(2),
        name="attention",
    )(q, mem_k, mem_v)


HALVES = CA_HD // LANES
KV_ROWS = CA_HEADS * HALVES


def _attn_rows_body(q_ref, xk_ref, xv_ref, o_ref, *, bb, L, M):
    scale = CA_HD ** -0.5
    R = M * KV_ROWS
    lane = lax.broadcasted_iota(jnp.int32, (L, R), 1)
    q_all = q_ref[...].astype(F32)
    for i in range(bb):
        q = q_all[i * L:(i + 1) * L, :]
        qm = jnp.concatenate([q[:, j * LANES:(j + 1) * LANES] for j in range(KV_ROWS)], axis=0).astype(BF16)
        xk = xk_ref[i * R:(i + 1) * R, :].astype(BF16)
        xv = xv_ref[i * R:(i + 1) * R, :].astype(BF16)
        g = _dot_nt(qm, xk)
        ps, invs = [], []
        for hh in range(CA_HEADS):
            s = None
            for half in range(HALVES):
                j = hh * HALVES + half
                part = g[j * L:(j + 1) * L]
                if half:
                    part = pltpu.roll(part, R - half * CA_HEADS, 1)
                s = part if s is None else s + part
            s = jnp.where((lane % KV_ROWS) == hh, s * scale, -jnp.inf)
            p = jnp.exp(s - jnp.max(s, -1, keepdims=True))
            inv = 1.0 / jnp.sum(p, -1, keepdims=True)
            for half in range(HALVES):
                ps.append(pltpu.roll(p, half * CA_HEADS, 1) if half else p)
                invs.append(inv)
        o = _dot(jnp.concatenate(ps, axis=0).astype(BF16), xv)
        for j in range(KV_ROWS):
            o_ref[i * L:(i + 1) * L, j * LANES:(j + 1) * LANES] = (o[j * L:(j + 1) * L] * invs[j]).astype(o_ref.dtype)


def _attention_rows(q, xk, xv, B, L, M):
    bb = 4 if B % 4 == 0 else 1
    R = M * KV_ROWS
    kv_spec = pl.BlockSpec((bb * R, LANES), lambda b: (b, 0))
    return pl.pallas_call(
        functools.partial(_attn_rows_body, bb=bb, L=L, M=M),
        grid=(B // bb,),
        in_specs=[pl.BlockSpec((bb * L, D_MODEL), lambda b: (b, 0)), kv_spec, kv_spec],
        out_specs=pl.BlockSpec((bb * L, D_MODEL), lambda b: (b, 0)),
        out_shape=jax.ShapeDtypeStruct((B * L, D_MODEL), BF16),
        compiler_params=_cparams(1),
        name="attention_rows",
    )(q, xk, xv)


def _route(x, wt_hi, wt_lo):
    assert MOE_GROUPS == SUBLANES and MOE_EPG == SUBLANES
    x_hi = x.astype(BF16)
    x_lo = (x - x_hi.astype(F32)).astype(BF16)
    logits = _dot_nt(wt_hi, x_hi) + (_dot_nt(wt_lo, x_hi) + _dot_nt(wt_hi, x_lo))
    tm = x.shape[0]
    sub = lax.broadcasted_iota(jnp.int32, (SUBLANES, tm), 0)
    neg = -jnp.inf
    g_log = logits[0:MOE_GROUPS]
    g_max = jnp.max(g_log, 0, keepdims=True)
    g_sel = jnp.min(jnp.where(g_log == g_max, sub, MOE_GROUPS), 0, keepdims=True)
    g_w = 1.0 / jnp.sum(jnp.exp(g_log - g_max), 0, keepdims=True)
    e_log = logits[MOE_GROUPS:MOE_GROUPS + MOE_EPG]
    for g in range(1, MOE_GROUPS):
        e_log = jnp.where(g_sel == g, logits[MOE_GROUPS + g * MOE_EPG:MOE_GROUPS + (g + 1) * MOE_EPG], e_log)
    v1 = jnp.max(e_log, 0, keepdims=True)
    i1 = jnp.min(jnp.where(e_log == v1, sub, MOE_EPG), 0, keepdims=True)
    e_log2 = jnp.where(sub == i1, neg, e_log)
    v2 = jnp.max(e_log2, 0, keepdims=True)
    i2 = jnp.min(jnp.where(e_log2 == v2, sub, MOE_EPG), 0, keepdims=True)
    t = jnp.exp(v2 - v1)
    w1 = g_w / (1.0 + t)
    w2 = g_w * t / (1.0 + t)
    first = sub == 0
    return g_sel * MOE_EPG + jnp.where(first, i1, i2), jnp.where(first, w1, w2)


def _plan_body(idx_ref, dest_ref, pend_ref, carry, pstart):
    ph = pl.program_id(0)
    i = pl.program_id(1)
    tm = idx_ref.shape[0]
    lane = lax.broadcasted_iota(jnp.int32, (tm, LANES), 1)
    oh0 = (lane == idx_ref[:, 0:1]).astype(F32)
    oh1 = (lane == idx_ref[:, 1:2]).astype(F32)
    both = oh0 + oh1

    @pl.when((ph == 0) & (i == 0))
    def _():
        carry[...] = jnp.zeros(carry.shape, F32)

    @pl.when(ph == 0)
    def _():
        carry[...] += jnp.sum(both, 0, keepdims=True)

    @pl.when((ph == 1) & (i == 0))
    def _():
        padded = jnp.floor((carry[...] + (MOE_BLOCK - 1)) * (1.0 / MOE_BLOCK)) * MOE_BLOCK
        a = lax.broadcasted_iota(jnp.int32, (LANES, LANES), 0)
        b = lax.broadcasted_iota(jnp.int32, (LANES, LANES), 1)
        upper = (a < b).astype(F32)
        ps = jnp.dot(jnp.broadcast_to(padded, (SUBLANES, LANES)), upper, preferred_element_type=F32,
                     precision=lax.Precision.HIGHEST)
        pstart[...] = ps[0:1]
        carry[...] = jnp.zeros(carry.shape, F32)

    @pl.when(ph == 1)
    def _():
        r = lax.broadcasted_iota(jnp.int32, (tm, tm), 0)
        c = lax.broadcasted_iota(jnp.int32, (tm, tm), 1)
        before = (r > c).astype(BF16)
        base = _dot(before, both.astype(BF16)) + carry[...] + pstart[...]
        d0 = jnp.sum(oh0 * base, -1, keepdims=True)
        d1 = jnp.sum(oh1 * base, -1, keepdims=True)
        first = lax.broadcasted_iota(jnp.int32, (tm, 2), 1) == 0
        dest_ref[...] = jnp.where(first, d0, d1).astype(jnp.int32)
        carry[...] += jnp.sum(both, 0, keepdims=True)
        padded_tot = jnp.floor((carry[...] + (MOE_BLOCK - 1)) * (1.0 / MOE_BLOCK)) * MOE_BLOCK
        pend_ref[...] = jnp.broadcast_to(pstart[...] + padded_tot, pend_ref.shape)


def _plan(idx, *, tm):
    T = idx.shape[0]
    return pl.pallas_call(
        _plan_body,
        grid=(2, T // tm),
        in_specs=[pl.BlockSpec((tm, 2), lambda p, i: (i, 0))],
        out_specs=[pl.BlockSpec((tm, 2), lambda p, i: (i * p, 0)), pl.BlockSpec((SUBLANES, LANES), lambda p, i: (0, 0))],
        out_shape=[jax.ShapeDtypeStruct((T, 2), jnp.int32), jax.ShapeDtypeStruct((SUBLANES, LANES), F32)],
        scratch_shapes=[pltpu.VMEM((1, LANES), F32), pltpu.VMEM((1, LANES), F32)],
        compiler_params=_cparams(2),
        name="plan",
    )(idx)


def _row_copy(src_ref, s, dst_ref, d, sem):
    return pltpu.make_async_copy(src_ref.at[pl.ds(s, 1), :], dst_ref.at[pl.ds(d, 1), :], sem)


def _dispatch_body(pend_ref, dest_ref, xa_ref, xb_ref, xs_ref, sem, zbuf, zsem, *, nta):
    i = pl.program_id(0)
    tm = xa_ref.shape[0]
    n_blocks = xs_ref.shape[0] // MOE_BLOCK

    @pl.when(i == 0)
    def _zero_partial_blocks():
        zbuf[...] = jnp.zeros(zbuf.shape, zbuf.dtype)

        def block_copy(blk):
            row0 = pl.multiple_of(blk * MOE_BLOCK, MOE_BLOCK)
            return pltpu.make_async_copy(zbuf, xs_ref.at[pl.ds(row0, MOE_BLOCK), :], zsem)

        def tail(e):
            end = pend_ref[e]
            begin = jnp.where(e == 0, 0, pend_ref[jnp.maximum(e - 1, 0)])
            return end > begin, block_copy(jnp.maximum(end // MOE_BLOCK - 1, 0))

        n_used = pend_ref[MOE_EXPERTS - 1] // MOE_BLOCK
        for op in ("start", "wait"):
            def expert_tail(e, c):
                used, cp = tail(e)
                pl.when(used)(getattr(cp, op))
                return c

            def unused_block(blk, c):
                pl.when(blk >= n_used)(getattr(block_copy(blk), op))
                return c

            lax.fori_loop(0, MOE_EXPERTS, expert_tail, 0)
            lax.fori_loop(0, n_blocks, unused_block, 0)

    def scatter_tile(x_ref):
        def start(t, c):
            _row_copy(x_ref, t, xs_ref, dest_ref[0, 0, 2 * t], sem).start()
            _row_copy(x_ref, t, xs_ref, dest_ref[0, 0, 2 * t + 1], sem).start()
            return c

        lax.fori_loop(0, tm, start, 0, unroll=ROW_DMA_UNROLL)
        for _ in range(2):
            pltpu.make_async_copy(x_ref, xs_ref.at[pl.ds(0, tm), :], sem).wait()

    pl.when(i < nta)(lambda: scatter_tile(xa_ref))
    pl.when(i >= nta)(lambda: scatter_tile(xb_ref))


def _dispatch(dest, xa, xb, pend, rows, *, tm):
    nta, ntb = xa.shape[0] // tm, xb.shape[0] // tm
    nt = nta + ntb
    grid_spec = pltpu.PrefetchScalarGridSpec(
        num_scalar_prefetch=1,
        grid=(nt,),
        in_specs=[
            pl.BlockSpec((1, 1, 2 * tm), lambda i, pe: (i, 0, 0), memory_space=pltpu.SMEM),
            pl.BlockSpec((tm, D_MODEL), lambda i, pe: (jnp.minimum(i, nta - 1), 0)),
            pl.BlockSpec((tm, D_MODEL), lambda i, pe: (jnp.maximum(i - nta, 0), 0)),
        ],
        out_specs=pl.BlockSpec(memory_space=pl.ANY),
        scratch_shapes=[pltpu.SemaphoreType.DMA(()), pltpu.VMEM((MOE_BLOCK, D_MODEL), F32),
                        pltpu.SemaphoreType.DMA(())],
    )
    return pl.pallas_call(
        functools.partial(_dispatch_body, nta=nta),
        grid_spec=grid_spec,
        out_shape=jax.ShapeDtypeStruct((rows, D_MODEL), F32),
        compiler_params=_cparams(1),
        name="dispatch",
    )(pend, dest.reshape(nt, 1, 2 * tm), xa, xb)


def _combine_body(dest_ref, x_ref, ew_ref, g_ref, b_ref, rows_ref, o_ref, gbuf, sem):
    tm = x_ref.shape[0]

    def start(t, c):
        _row_copy(rows_ref, dest_ref[0, 0, 2 * t], gbuf.at[0], t, sem).start()
        _row_copy(rows_ref, dest_ref[0, 0, 2 * t + 1], gbuf.at[1], t, sem).start()
        return c

    lax.fori_loop(0, tm, start, 0, unroll=ROW_DMA_UNROLL)
    for half in range(2):
        pltpu.make_async_copy(rows_ref.at[pl.ds(0, tm), :], gbuf.at[half], sem).wait()
    y = ew_ref[:, 0:1] * gbuf[0] + ew_ref[:, 1:2] * gbuf[1]
    o_ref[...] = _layer_norm(ALPHA * x_ref[...] + y, g_ref[...], b_ref[...])


def _combine(dest, x, ew, g, b, rows, *, tm):
    T = x.shape[0]
    nt = T // tm
    const = lambda i: (0, 0)
    return pl.pallas_call(
        _combine_body,
        grid=(nt,),
        in_specs=[
            pl.BlockSpec((1, 1, 2 * tm), lambda i: (i, 0, 0), memory_space=pltpu.SMEM),
            pl.BlockSpec((tm, D_MODEL), lambda i: (i, 0)),
            pl.BlockSpec((tm, 2), lambda i: (i, 0)),
            pl.BlockSpec((1, D_MODEL), const),
            pl.BlockSpec((1, D_MODEL), const),
            pl.BlockSpec(memory_space=pl.ANY),
        ],
        out_specs=pl.BlockSpec((tm, D_MODEL), lambda i: (i, 0)),
        out_shape=jax.ShapeDtypeStruct((T, D_MODEL), F32),
        scratch_shapes=[pltpu.VMEM((2, tm, D_MODEL), F32), pltpu.SemaphoreType.DMA(())],
        compiler_params=_cparams(1),
        name="combine",
    )(dest.reshape(nt, 1, 2 * tm), x, ew, g, b, rows)


def _experts_body(be_ref, nu_ref, xs_ref, wg_ref, wu_ref, wd_ref, o_ref):
    del be_ref
    i = pl.program_id(0)

    @pl.when(i < nu_ref[0])
    def _():
        xb = xs_ref[...].astype(BF16)
        gate = _dot(xb, wg_ref[0].astype(BF16))
        up = _dot(xb, wu_ref[0].astype(BF16))
        hid = gate * jax.nn.sigmoid(gate) * up
        o_ref[...] = _dot(hid.astype(BF16), wd_ref[0].astype(BF16))

    @pl.when(i >= nu_ref[0])
    def _():
        o_ref[...] = jnp.zeros(o_ref.shape, F32)


def _experts(block_expert, n_used, xs, w_gate, w_up, w_down):
    R = xs.shape[0]
    nb = R // MOE_BLOCK
    grid_spec = pltpu.PrefetchScalarGridSpec(
        num_scalar_prefetch=2,
        grid=(nb,),
        in_specs=[
            pl.BlockSpec((MOE_BLOCK, D_MODEL), lambda i, be, nu: (jnp.minimum(i, jnp.maximum(nu[0] - 1, 0)), 0)),
            pl.BlockSpec((1, D_MODEL, MOE_FF), lambda i, be, nu: (be[i], 0, 0)),
            pl.BlockSpec((1, D_MODEL, MOE_FF), lambda i, be, nu: (be[i], 0, 0)),
            pl.BlockSpec((1, MOE_FF, D_MODEL), lambda i, be, nu: (be[i], 0, 0)),
        ],
        out_specs=pl.BlockSpec((MOE_BLOCK, D_MODEL), lambda i, be, nu: (i, 0)),
    )
    return pl.pallas_call(
        _experts_body,
        grid_spec=grid_spec,
        out_shape=jax.ShapeDtypeStruct((R, D_MODEL), F32),
        compiler_params=_cparams(1),
        name="experts",
    )(block_expert, n_used, xs, w_gate, w_up, w_down)


def _row_tile(T, pref):
    t = pref
    while T % t:
        t //= 2
    return t


def _group_to_x2(x, mem_k, mem_v, conv_prev, pool_prev, s0, pos0, W):
    B, L, _ = x.shape
    T = B * L
    xf = x.reshape(T, D_MODEL)
    tm = _row_tile(T, 512)
    mixer_w = (W["w_conv"], W["a_log"], W["dt_bias"], W["w_onorm"], W["w_pool"], W["pool_scale"])
    if L >= FUSE_MIN_SEQ:
        merged, s_new, conv_tail, pool_tail = _mixer(xf, B, L, conv_prev, pool_prev, s0, *mixer_w, pos0=pos0,
                                                     w_in=W["w_in"])
    else:
        proj = _mm(xf, W["w_in"], tm=tm, tn=N_PROJ // 3)
        merged, s_new, conv_tail, pool_tail = _mixer(proj, B, L, conv_prev, pool_prev, s0, *mixer_w, pos0=pos0)
    x1, q = _proj_ln(merged, W["w_out"], xf, W["ln1_g"], W["ln1_b"], (W["w_cq"],), tm=tm, route=False)
    if mem_k.ndim == 3:
        att = _attention(q, mem_k, mem_v, B, L)
    else:
        M = mem_k.shape[1]

        def rows_view(m):
            m = m.reshape(B, M, CA_HEADS, HALVES, LANES)
            return jnp.swapaxes(m, 2, 3).reshape(B * M * KV_ROWS, LANES)

        att = _attention_rows(q, rows_view(mem_k), rows_view(mem_v), B, L, M)
    x2, idx, ew = _proj_ln(att, W["w_co"], x1, W["ln2_g"], W["ln2_b"], (W["w_r_hi"], W["w_r_lo"]), tm=tm, route=True)
    idx, ew = idx[0:2].T, ew[0:2].T
    return x2, idx, ew, s_new, conv_tail[:, -(CONV_W - 1):], pool_tail[:, -(POOL_MAX - 1):]


def kernel(x_prompt, x_sample, cache_mem_k, cache_mem_v, state_delta, state_conv, state_pool, mem_prompt, w_in, w_conv, a_log, dt_bias, w_onorm, w_pool, pool_scale, w_out, ln1_g, ln1_b, w_cq, w_ck, w_cv, w_co, ln2_g, ln2_b, w_router_group, w_router_expert, w_gate, w_up, w_down, ln3_g, ln3_b):
    Bp, Lp, _ = x_prompt.shape
    Bs, Ls, _ = x_sample.shape
    Tp, Ts = Bp * Lp, Bs * Ls
    lyr = 0

    def pad_lanes(v):
        return jnp.pad(v.astype(F32), (0, LANES - v.shape[0])).reshape(1, LANES)

    o1, o2, o3, o4 = QKV_DIM, QKV_DIM + D_MODEL, QKV_DIM + D_MODEL + 2 * DN_HEADS, QKV_DIM + 2 * D_MODEL + 2 * DN_HEADS
    wi = w_in[lyr]
    w_in_r = jnp.concatenate(
        [wi[:, :o2], wi[:, o3:], wi[:, o2:o3], jnp.zeros((D_MODEL, LANES - 2 * DN_HEADS), F32)], axis=1).astype(BF16)
    del o1, o4
    w_r = jnp.concatenate([w_router_group[lyr], w_router_expert[lyr],
                           jnp.zeros((D_MODEL, LANES - MOE_GROUPS - MOE_EXPERTS), F32)], axis=1).T
    w_r_hi = w_r.astype(BF16)
    W = {
        "w_r_hi": w_r_hi,
        "w_r_lo": (w_r - w_r_hi.astype(F32)).astype(BF16),
        "w_in": w_in_r,
        "w_conv": w_conv[lyr],
        "a_log": pad_lanes(a_log[lyr]),
        "dt_bias": pad_lanes(dt_bias[lyr]),
        "w_onorm": w_onorm[lyr].reshape(1, DN_HD),
        "w_pool": w_pool[lyr].astype(BF16),
        "pool_scale": pool_scale[lyr].reshape(1, D_MODEL),
        "w_out": w_out[lyr].astype(BF16),
        "ln1_g": ln1_g[lyr].reshape(1, D_MODEL), "ln1_b": ln1_b[lyr].reshape(1, D_MODEL),
        "w_cq": w_cq[lyr].astype(BF16),
        "w_co": w_co[lyr].astype(BF16),
        "ln2_g": ln2_g[lyr].reshape(1, D_MODEL), "ln2_b": ln2_b[lyr].reshape(1, D_MODEL),
    }

    M = mem_prompt.shape[1]
    memf = mem_prompt.reshape(Bp * M, D_MODEL)
    tmm = _row_tile(Bp * M, 512)
    mk = _mm(memf, w_ck[lyr].astype(BF16), tm=tmm, tn=D_MODEL).reshape(Bp, M, D_MODEL)
    mv = _mm(memf, w_cv[lyr].astype(BF16), tm=tmm, tn=D_MODEL).reshape(Bp, M, D_MODEL)
    x2_p, idx_p, ew_p, d_p, c_p, p_p = _group_to_x2(x_prompt, mk, mv, None, None, None, 0, W)
    mk = mk.reshape(Bp, M, CA_HEADS, CA_HD)
    mv = mv.reshape(Bp, M, CA_HEADS, CA_HD)
    x2_s, idx_s, ew_s, d_s, c_s, p_s = _group_to_x2(x_sample, cache_mem_k[lyr], cache_mem_v[lyr], state_conv[lyr],
                                       state_pool[lyr], state_delta[lyr], PAST_LEN, W)

    tp, ts = _row_tile(Tp, 256), _row_tile(Ts, 256)
    T = Tp + Ts
    dest, pend = _plan(jnp.concatenate([idx_p, idx_s], axis=0), tm=_row_tile(T, 1024))
    n_blocks = (2 * T + MOE_EXPERTS * (MOE_BLOCK - 1) + MOE_BLOCK - 1) // MOE_BLOCK
    pend_e = pend[0, :MOE_EXPERTS].astype(jnp.int32)
    block_start = jnp.arange(n_blocks, dtype=jnp.int32) * MOE_BLOCK
    block_expert = jnp.minimum(jnp.sum(block_start[:, None] >= pend_e[None, :], axis=1), MOE_EXPERTS - 1).astype(jnp.int32)
    n_used = (pend_e[MOE_EXPERTS - 1:] // MOE_BLOCK).astype(jnp.int32)
    xs = _dispatch(dest, x2_p, x2_s, pend_e, n_blocks * MOE_BLOCK, tm=_row_tile(math.gcd(Tp, Ts), 256))
    rows = _experts(block_expert, n_used, xs, w_gate[lyr], w_up[lyr], w_down[lyr])
    g3, b3 = ln3_g[lyr].reshape(1, D_MODEL), ln3_b[lyr].reshape(1, D_MODEL)
    y_p = _combine(dest[:Tp], x2_p, ew_p, g3, b3, rows, tm=tp).reshape(Bp, Lp, D_MODEL)
    y_s = _combine(dest[Tp:], x2_s, ew_s, g3, b3, rows, tm=ts).reshape(Bs, Ls, D_MODEL)

    return (y_p, y_s, d_p[None], c_p[None], p_p[None], mk[None], mv[None], d_s[None], c_s[None], p_s[None])
```

```python
import functools
import math

import jax
import jax.numpy as jnp
from jax import lax
from jax.experimental import pallas as pl
from jax.experimental.pallas import tpu as pltpu

F32 = jnp.float32
BF16 = jnp.bfloat16

D_MODEL = 1024
DN_HEADS = 8
DN_HD = 128
QKV_DIM = 3 * DN_HEADS * DN_HD
CONV_W = 4
DN_CHUNK = 64
POOL_WINDOWS = (2, 4, 8, 16)
POOL_GD = D_MODEL // len(POOL_WINDOWS)
POOL_MAX = 16
CA_HEADS = 4
CA_HD = D_MODEL // CA_HEADS
MOE_GROUPS = 8
MOE_EPG = 8
MOE_EXPERTS = MOE_GROUPS * MOE_EPG
MOE_FF = D_MODEL // 4
MOE_BLOCK = 256
PAST_LEN = 16384
LN_EPS = 1e-5
NORM_EPS = 1e-6
ALPHA = 2.0 ** 0.25

LANES = 128
SUBLANES = 8
CONV_HALO = SUBLANES
POOL_HALO = POOL_MAX
MIN_CHUNK = 16
N_PROJ = 7 * D_MODEL + LANES
VMEM_LIMIT = 48 * 1024 * 1024
MIXER_VMEM_LIMIT = 56 * 1024 * 1024
ROW_DMA_UNROLL = 256
FUSE_MIN_SEQ = 128
SEQS_PER_STEP = 8


def _cparams(n_axes, vmem=VMEM_LIMIT):
    return pltpu.CompilerParams(dimension_semantics=("arbitrary",) * n_axes, vmem_limit_bytes=vmem)


def _layer_norm(x, g, b):
    mu = jnp.mean(x, -1, keepdims=True)
    xc = x - mu
    var = jnp.mean(xc * xc, -1, keepdims=True)
    return xc * lax.rsqrt(var + LN_EPS) * g + b


def _softplus(x):
    return jnp.maximum(x, 0.0) + jnp.log1p(jnp.exp(-jnp.abs(x)))


def _dot(a, b):
    return jnp.dot(a, b, preferred_element_type=F32)


def _dot_nt(a, b):
    return lax.dot_general(a, b, (((1,), (1,)), ((), ())), preferred_element_type=F32)


def _dot_tn(a, b):
    return lax.dot_general(a, b, (((0,), (0,)), ((), ())), preferred_element_type=F32)


def _mm_body(x_ref, w_ref, o_ref):
    o_ref[...] = _dot(x_ref[...].astype(BF16), w_ref[...]).astype(o_ref.dtype)


def _mm(x, w, *, tm, tn, out_dtype=F32):
    T, K = x.shape
    N = w.shape[1]
    return pl.pallas_call(
        _mm_body,
        grid=(N // tn, T // tm),
        in_specs=[pl.BlockSpec((tm, K), lambda j, i: (i, 0)), pl.BlockSpec((K, tn), lambda j, i: (0, j))],
        out_specs=pl.BlockSpec((tm, tn), lambda j, i: (i, j)),
        out_shape=jax.ShapeDtypeStruct((T, N), out_dtype),
        compiler_params=_cparams(2),
        name="mm",
    )(x, w)


def _mixer_body(*refs, tl, chunk, pos0, has_state, fused, nb):
    refs = list(refs)
    take = lambda n: [refs.pop(0) for _ in range(n)]
    if fused:
        x_ref, win_hbm = take(2)
    else:
        qkv_ref, z_ref, p_ref, ga_ref, gb_ref, ab_ref = take(6)
    if has_state:
        convp_ref, poolp_ref, s0_ref = take(3)
    wconv_ref, alog_ref, dtb_ref, wonorm_ref, wpool_ref, pscale_ref = take(6)
    merged_ref, sout_ref, convo_ref, poolo_ref = take(4)
    qkvbuf, pbuf, obuf, gcb, betab, gt_ref = take(6)
    if fused:
        z_ref, ga_ref, gb_ref, win_vmem, win_sem = take(5)
    b = pl.program_id(0)
    l = pl.program_id(1)
    tlp = max(tl, MIN_CHUNK)
    C = max(chunk, MIN_CHUNK)
    cps = tlp // C
    seqs = range(nb)

    @pl.when(l == 0)
    def _init():
        for s in seqs:
            qkvbuf[s, 0:CONV_HALO, :] = jnp.zeros((CONV_HALO, QKV_DIM), F32)
            pbuf[s, 0:POOL_HALO, :] = jnp.zeros((POOL_HALO, D_MODEL), F32)
            if has_state:
                qkvbuf[s, CONV_HALO - (CONV_W - 1):CONV_HALO, :] = convp_ref[s]
                pbuf[s, POOL_HALO - (POOL_MAX - 1):POOL_HALO, :] = poolp_ref[s]
        if has_state:
            sout_ref[...] = s0_ref[...]
        else:
            sout_ref[...] = jnp.zeros(sout_ref.shape, F32)

    if fused:
        @pl.when((b == 0) & (l == 0))
        def _load_weight():
            cp = pltpu.make_async_copy(win_hbm, win_vmem, win_sem)
            cp.start()
            cp.wait()

        xb = x_ref[...].astype(BF16)

        def proj_cols(c0, width):
            return _dot(xb, win_vmem[:, c0:c0 + width])

        ab = proj_cols(7 * D_MODEL, LANES)
        for c in range(QKV_DIM // D_MODEL):
            qkvbuf[0, CONV_HALO:CONV_HALO + tl, c * D_MODEL:(c + 1) * D_MODEL] = proj_cols(c * D_MODEL, D_MODEL)
        z_ref[...] = proj_cols(3 * D_MODEL, D_MODEL)
        pbuf[0, POOL_HALO:POOL_HALO + tl, :] = proj_cols(4 * D_MODEL, D_MODEL)
        ga_ref[...] = proj_cols(5 * D_MODEL, D_MODEL)
        gb_ref[...] = proj_cols(6 * D_MODEL, D_MODEL)
    else:
        ab = ab_ref[...]
        for s in seqs:
            qkvbuf[s, CONV_HALO:CONV_HALO + tl, :] = qkv_ref[s * tl:(s + 1) * tl, :]
            pbuf[s, POOL_HALO:POOL_HALO + tl, :] = p_ref[s * tl:(s + 1) * tl, :]

    def pad_seq(pieces):
        out = []
        for x in pieces:
            out.append(x)
            if tlp != tl:
                out.append(jnp.zeros((tlp - tl, x.shape[1]), F32))
        return out[0] if len(out) == 1 else jnp.concatenate(out, axis=0)

    def unpad_seq(x):
        if tlp == tl:
            return x
        return jnp.concatenate([x[s * tlp:s * tlp + tl] for s in seqs], axis=0)

    g_raw = -jnp.exp(alog_ref[...]) * _softplus(ab + dtb_ref[...])
    beta_raw = jax.nn.sigmoid(ab)
    g_all = pad_seq([g_raw[s * tl:(s + 1) * tl] for s in seqs])
    beta_all = pad_seq([beta_raw[s * tl:(s + 1) * tl] for s in seqs])
    R = nb * tlp
    ri = lax.broadcasted_iota(jnp.int32, (R, R), 0)
    ci = lax.broadcasted_iota(jnp.int32, (R, R), 1)
    cum_mat = ((ri >= ci) & ((ri // C) == (ci // C))).astype(F32)
    gcum = jnp.dot(cum_mat, g_all, preferred_element_type=F32, precision=lax.Precision.HIGHEST)
    gcum_t = gcum.T
    for h in range(DN_HEADS):
        gt_ref[h] = jnp.broadcast_to(gcum_t[h:h + 1, :], (SUBLANES, R))
        gcb[h] = jnp.broadcast_to(gcum[:, h:h + 1], (R, LANES))
        betab[h] = jnp.broadcast_to(beta_all[:, DN_HEADS + h:DN_HEADS + h + 1], (R, LANES))

    ii = lax.broadcasted_iota(jnp.int32, (C, C), 0)
    jj = lax.broadcasted_iota(jnp.int32, (C, C), 1)
    causal = ii >= jj
    strict = ii > jj
    eye = (ii == jj).astype(F32)
    n_sq = max(C.bit_length() - 2, 0)

    def conv_slab(col0):
        cols = slice(col0, col0 + DN_HD)
        pieces = []
        for s in seqs:
            acc = qkvbuf[s, CONV_HALO:CONV_HALO + tl, cols] * wconv_ref[CONV_W - 1:CONV_W, cols]
            for i in range(CONV_W - 1):
                r0 = CONV_HALO - (CONV_W - 1) + i
                acc = acc + qkvbuf[s, r0:r0 + tl, cols] * wconv_ref[i:i + 1, cols]
            pieces.append(acc * jax.nn.sigmoid(acc))
        return pad_seq(pieces)

    heads = range(DN_HEADS)
    chunks = range(nb * cps)
    probs = [(h, c) for h in heads for c in chunks]
    rows = [slice(c * C, (c + 1) * C) for c in chunks]
    q, k, v, gcs, bs, egs, kbs, grows = [], [], [], [], [], [], [], []
    for h in heads:
        qh = conv_slab(h * DN_HD)
        kh = conv_slab((DN_HEADS + h) * DN_HD)
        v.append(conv_slab((2 * DN_HEADS + h) * DN_HD))
        q.append(qh * lax.rsqrt(jnp.sum(qh * qh, -1, keepdims=True) + NORM_EPS) * (DN_HD ** -0.5))
        kh = kh * lax.rsqrt(jnp.sum(kh * kh, -1, keepdims=True) + NORM_EPS)
        k.append(kh)
        gcs.append(gcb[h])
        bs.append(betab[h])
        egs.append(jnp.exp(gcs[h]))
        kbs.append(kh * bs[h])
        grows.append(gt_ref[h][0:1, :])

    aq = {(h, c): _dot_nt(jnp.concatenate([kbs[h][rows[c]], q[h][rows[c]]], axis=0).astype(BF16),
                          k[h][rows[c]].astype(BF16)) for h, c in probs}
    decay = {(h, c): jnp.exp(jnp.where(causal, gcs[h][rows[c], 0:C] - grows[h][:, c * C:(c + 1) * C], -jnp.inf))
             for h, c in probs}
    A = {p: jnp.where(strict, aq[p][0:C] * decay[p], 0.0) for p in probs}
    qk = {p: (aq[p][C:2 * C] * decay[p]).astype(BF16) for p in probs}
    P = {p: eye - A[p] for p in probs}
    Q = {p: _dot(A[p].astype(BF16), A[p].astype(BF16)) for p in probs}
    for _ in range(n_sq - 1):
        pq = {p: _dot(jnp.concatenate([P[p], Q[p]], axis=0).astype(BF16), Q[p].astype(BF16)) for p in probs}
        P = {p: P[p] + pq[p][0:C] for p in probs}
        Q = {p: pq[p][C:2 * C] for p in probs}
    P = {p: P[p] + _dot(P[p].astype(BF16), Q[p].astype(BF16)) for p in probs}
    sol = {(h, c): _dot(P[(h, c)].astype(BF16),
                        jnp.concatenate([v[h][rows[c]] * bs[h][rows[c]], kbs[h][rows[c]] * egs[h][rows[c]]],
                                        axis=1).astype(BF16)) for h, c in probs}

    S = [[sout_ref[s, h] for h in heads] for s in seqs]
    sh = [(s, h) for s in seqs for h in heads]
    outs = {}
    for j in range(cps):
        cof = {s: s * cps + j for s in seqs}
        ws_qs = {(s, h): _dot(jnp.concatenate([sol[(h, cof[s])][:, DN_HD:2 * DN_HD],
                                               q[h][rows[cof[s]]] * egs[h][rows[cof[s]]]], axis=0).astype(BF16),
                              S[s][h].astype(BF16)) for s, h in sh}
        v_new = {(s, h): (sol[(h, cof[s])][:, 0:DN_HD] - ws_qs[(s, h)][0:C]).astype(BF16) for s, h in sh}
        glast = {(s, h): gcs[h][rows[cof[s]]][C - 1:C, :] for s, h in sh}
        S_new = {(s, h): S[s][h] * jnp.exp(glast[(s, h)])
                 + _dot_tn((k[h][rows[cof[s]]] * jnp.exp(glast[(s, h)] - gcs[h][rows[cof[s]]])).astype(BF16),
                           v_new[(s, h)]) for s, h in sh}
        S = [[S_new[(s, h)] for h in heads] for s in seqs]
        for s, h in sh:
            outs[(h, cof[s])] = ws_qs[(s, h)][C:2 * C] + _dot(qk[(h, cof[s])], v_new[(s, h)])
    for s, h in sh:
        sout_ref[s, h] = S[s][h]
    for h in heads:
        o = outs[(h, 0)] if len(chunks) == 1 else jnp.concatenate([outs[(h, c)] for c in chunks], axis=0)
        o = unpad_seq(o)
        o = o * lax.rsqrt(jnp.mean(o * o, -1, keepdims=True) + NORM_EPS) * wonorm_ref[...]
        zh = z_ref[:, h * DN_HD:(h + 1) * DN_HD]
        obuf[:, h * DN_HD:(h + 1) * DN_HD] = o * (zh * jax.nn.sigmoid(zh))

    if pos0 == 0:
        pos = l * tl + lax.broadcasted_iota(jnp.int32, (tl, 1), 0)
    for gi, win in enumerate(POOL_WINDOWS):
        cs = slice(gi * POOL_GD, (gi + 1) * POOL_GD)
        pooled = []
        for s in seqs:
            acc = pbuf[s, POOL_HALO:POOL_HALO + tl, cs]
            for j in range(1, win):
                acc = acc + pbuf[s, POOL_HALO - j:POOL_HALO - j + tl, cs]
            if pos0 == 0:
                cnt = jnp.minimum(win, pos + 1).astype(F32)
            else:
                cnt = float(min(win, pos0 + 1))
            pooled.append(acc / cnt - pbuf[s, POOL_HALO:POOL_HALO + tl, cs])
        pooled = pooled[0] if nb == 1 else jnp.concatenate(pooled, axis=0)
        bb = _dot(pooled.astype(BF16), wpool_ref[gi]) * pscale_ref[:, cs]
        merged_ref[:, cs] = (jax.nn.sigmoid(ga_ref[:, cs]) * obuf[:, cs]
                             + jax.nn.sigmoid(gb_ref[:, cs]) * bb).astype(merged_ref.dtype)

    for s in seqs:
        conv_tail = qkvbuf[s, tl:tl + CONV_HALO, :]
        pool_tail = pbuf[s, tl:tl + POOL_HALO, :]
        convo_ref[s] = conv_tail
        poolo_ref[s] = pool_tail
        if tl >= POOL_HALO:
            qkvbuf[s, 0:CONV_HALO, :] = conv_tail
            pbuf[s, 0:POOL_HALO, :] = pool_tail


def _mixer(src, B, L, conv_prev, pool_prev, s0, wconv, alog, dtb, wonorm, wpool, pscale, *, pos0, w_in=None):
    tl = min(256, L)
    nL = L // tl
    assert nL == 1 or tl >= POOL_HALO
    chunk = min(DN_CHUNK, tl)
    tlp = max(tl, MIN_CHUNK)
    has_state = s0 is not None
    fused = w_in is not None
    nb = next(n for n in (SEQS_PER_STEP, 2, 1) if B % n == 0) if (nL == 1 and not fused) else 1
    rt = nb * tl
    row = lambda b, l: b * nL + l
    if fused:
        in_specs = [pl.BlockSpec((rt, D_MODEL), lambda b, l: (row(b, l), 0)), pl.BlockSpec(memory_space=pl.ANY)]
        args = [src, w_in]
    else:
        in_specs = [
            pl.BlockSpec((rt, QKV_DIM), lambda b, l: (row(b, l), 0)),
            pl.BlockSpec((rt, D_MODEL), lambda b, l: (row(b, l), 3)),
            pl.BlockSpec((rt, D_MODEL), lambda b, l: (row(b, l), 4)),
            pl.BlockSpec((rt, D_MODEL), lambda b, l: (row(b, l), 5)),
            pl.BlockSpec((rt, D_MODEL), lambda b, l: (row(b, l), 6)),
            pl.BlockSpec((rt, LANES), lambda b, l: (row(b, l), 7 * D_MODEL // LANES)),
        ]
        args = [src] * 6
    if has_state:
        in_specs += [
            pl.BlockSpec((nb, CONV_W - 1, QKV_DIM), lambda b, l: (b, 0, 0)),
            pl.BlockSpec((nb, POOL_MAX - 1, D_MODEL), lambda b, l: (b, 0, 0)),
            pl.BlockSpec((nb, DN_HEADS, DN_HD, DN_HD), lambda b, l: (b, 0, 0, 0)),
        ]
        args += [conv_prev, pool_prev, s0]
    const2 = lambda b, l: (0, 0)
    in_specs += [
        pl.BlockSpec((CONV_W, QKV_DIM), const2),
        pl.BlockSpec((1, LANES), const2),
        pl.BlockSpec((1, LANES), const2),
        pl.BlockSpec((1, DN_HD), const2),
        pl.BlockSpec((len(POOL_WINDOWS), POOL_GD, POOL_GD), lambda b, l: (0, 0, 0)),
        pl.BlockSpec((1, D_MODEL), const2),
    ]
    args += [wconv, alog, dtb, wonorm, wpool, pscale]
    scratch = [
        pltpu.VMEM((nb, CONV_HALO + tl, QKV_DIM), F32),
        pltpu.VMEM((nb, POOL_HALO + tl, D_MODEL), F32),
        pltpu.VMEM((rt, D_MODEL), F32),
        pltpu.VMEM((DN_HEADS, nb * tlp, LANES), F32),
        pltpu.VMEM((DN_HEADS, nb * tlp, LANES), F32),
        pltpu.VMEM((DN_HEADS, SUBLANES, nb * tlp), F32),
    ]
    if fused:
        scratch += [pltpu.VMEM((tl, D_MODEL), F32)] * 3
        scratch += [pltpu.VMEM(w_in.shape, w_in.dtype), pltpu.SemaphoreType.DMA(())]
    return pl.pallas_call(
        functools.partial(_mixer_body, tl=tl, chunk=chunk, pos0=pos0, has_state=has_state, fused=fused, nb=nb),
        grid=(B // nb, nL),
        in_specs=in_specs,
        out_specs=[
            pl.BlockSpec((rt, D_MODEL), lambda b, l: (row(b, l), 0)),
            pl.BlockSpec((nb, DN_HEADS, DN_HD, DN_HD), lambda b, l: (b, 0, 0, 0)),
            pl.BlockSpec((nb, CONV_HALO, QKV_DIM), lambda b, l: (b, 0, 0)),
            pl.BlockSpec((nb, POOL_HALO, D_MODEL), lambda b, l: (b, 0, 0)),
        ],
        out_shape=[
            jax.ShapeDtypeStruct((B * L, D_MODEL), BF16),
            jax.ShapeDtypeStruct((B, DN_HEADS, DN_HD, DN_HD), F32),
            jax.ShapeDtypeStruct((B, CONV_HALO, QKV_DIM), F32),
            jax.ShapeDtypeStruct((B, POOL_HALO, D_MODEL), F32),
        ],
        scratch_shapes=scratch,
        compiler_params=_cparams(2, vmem=MIXER_VMEM_LIMIT),
        name="mixer",
    )(*args)


def _proj_ln_q_body(a_ref, w_ref, r_ref, g_ref, b_ref, wq_ref, o_ref, q_ref):
    y = ALPHA * r_ref[...] + _dot(a_ref[...], w_ref[...])
    x1 = _layer_norm(y, g_ref[...], b_ref[...])
    o_ref[...] = x1
    q_ref[...] = _dot(x1.astype(BF16), wq_ref[...]).astype(q_ref.dtype)


def _proj_ln_route_body(a_ref, w_ref, r_ref, g_ref, b_ref, whi_ref, wlo_ref, o_ref, idx_ref, ew_ref):
    y = ALPHA * r_ref[...] + _dot(a_ref[...], w_ref[...])
    x2 = _layer_norm(y, g_ref[...], b_ref[...])
    o_ref[...] = x2
    idx_ref[...], ew_ref[...] = _route(x2, whi_ref[...], wlo_ref[...])


def _proj_ln(a, w, resid, g, b, tail, *, tm, route):
    T = a.shape[0]
    const = lambda i: (0, 0)
    rows = lambda i: (i, 0)
    in_specs = [
        pl.BlockSpec((tm, D_MODEL), rows),
        pl.BlockSpec((D_MODEL, D_MODEL), const),
        pl.BlockSpec((tm, D_MODEL), rows),
        pl.BlockSpec((1, D_MODEL), const),
        pl.BlockSpec((1, D_MODEL), const),
    ] + [pl.BlockSpec(t.shape, const) for t in tail]
    out_specs = [pl.BlockSpec((tm, D_MODEL), rows)]
    out_shape = [jax.ShapeDtypeStruct((T, D_MODEL), F32)]
    if route:
        cols = lambda i: (0, i)
        out_specs += [pl.BlockSpec((SUBLANES, tm), cols), pl.BlockSpec((SUBLANES, tm), cols)]
        out_shape += [jax.ShapeDtypeStruct((SUBLANES, T), jnp.int32), jax.ShapeDtypeStruct((SUBLANES, T), F32)]
    else:
        out_specs += [pl.BlockSpec((tm, D_MODEL), rows)]
        out_shape += [jax.ShapeDtypeStruct((T, D_MODEL), BF16)]
    return pl.pallas_call(
        _proj_ln_route_body if route else _proj_ln_q_body,
        grid=(T // tm,),
        in_specs=in_specs,
        out_specs=out_specs,
        out_shape=out_shape,
        compiler_params=_cparams(1),
        name="proj_ln_route" if route else "proj_ln_q",
    )(a, w, resid, g, b, *tail)


def _attn_body(q_ref, k_ref, v_ref, o_ref):
    scale = CA_HD ** -0.5
    for hh in range(CA_HEADS):
        cs = slice(hh * CA_HD, (hh + 1) * CA_HD)
        s = _dot_nt(q_ref[:, cs].astype(BF16), k_ref[0, :, cs].astype(BF16)) * scale
        m = jnp.max(s, -1, keepdims=True)
        p = jnp.exp(s - m)
        denom = jnp.sum(p, -1, keepdims=True)
        o = _dot(p.astype(BF16), v_ref[0, :, cs].astype(BF16)) / denom
        o_ref[:, cs] = o.astype(o_ref.dtype)


def _attention(q, mem_k, mem_v, B, L):
    tq = min(512, L)
    nq = L // tq
    M = mem_k.shape[1]
    kv_spec = pl.BlockSpec((1, M, D_MODEL), lambda b, i: (b, 0, 0))
    return pl.pallas_call(
        _attn_body,
        grid=(B, nq),
        in_specs=[pl.BlockSpec((tq, D_MODEL), lambda b, i: (b * nq + i, 0)), kv_spec, kv_spec],
        out_specs=pl.BlockSpec((tq, D_MODEL), lambda b, i: (b * nq + i, 0)),
        out_shape=jax.ShapeDtypeStruct((B * L, D_MODEL), BF16),
        compiler_params=_cparams(2),
        name="attention",
    )(q, mem_k, mem_v)


HALVES = CA_HD // LANES
KV_ROWS = CA_HEADS * HALVES


def _attn_rows_body(q_ref, xk_ref, xv_ref, o_ref, *, bb, L, M):
    scale = CA_HD ** -0.5
    R = M * KV_ROWS
    lane = lax.broadcasted_iota(jnp.int32, (L, R), 1)
    q_all = q_ref[...].astype(F32)
    for i in range(bb):
        q = q_all[i * L:(i + 1) * L, :]
        qm = jnp.concatenate([q[:, j * LANES:(j + 1) * LANES] for j in range(KV_ROWS)], axis=0).astype(BF16)
        xk = xk_ref[i * R:(i + 1) * R, :].astype(BF16)
        xv = xv_ref[i * R:(i + 1) * R, :].astype(BF16)
        g = _dot_nt(qm, xk)
        ps, invs = [], []
        for hh in range(CA_HEADS):
            s = None
            for half in range(HALVES):
                j = hh * HALVES + half
                part = g[j * L:(j + 1) * L]
                if half:
                    part = pltpu.roll(part, R - half * CA_HEADS, 1)
                s = part if s is None else s + part
            s = jnp.where((lane % KV_ROWS) == hh, s * scale, -jnp.inf)
            p = jnp.exp(s - jnp.max(s, -1, keepdims=True))
            inv = 1.0 / jnp.sum(p, -1, keepdims=True)
            for half in range(HALVES):
                ps.append(pltpu.roll(p, half * CA_HEADS, 1) if half else p)
                invs.append(inv)
        o = _dot(jnp.concatenate(ps, axis=0).astype(BF16), xv)
        for j in range(KV_ROWS):
            o_ref[i * L:(i + 1) * L, j * LANES:(j + 1) * LANES] = (o[j * L:(j + 1) * L] * invs[j]).astype(o_ref.dtype)


def _attention_rows(q, xk, xv, B, L, M):
    bb = 4 if B % 4 == 0 else 1
    R = M * KV_ROWS
    kv_spec = pl.BlockSpec((bb * R, LANES), lambda b: (b, 0))
    return pl.pallas_call(
        functools.partial(_attn_rows_body, bb=bb, L=L, M=M),
        grid=(B // bb,),
        in_specs=[pl.BlockSpec((bb * L, D_MODEL), lambda b: (b, 0)), kv_spec, kv_spec],
        out_specs=pl.BlockSpec((bb * L, D_MODEL), lambda b: (b, 0)),
        out_shape=jax.ShapeDtypeStruct((B * L, D_MODEL), BF16),
        compiler_params=_cparams(1),
        name="attention_rows",
    )(q, xk, xv)


def _route(x, wt_hi, wt_lo):
    assert MOE_GROUPS == SUBLANES and MOE_EPG == SUBLANES
    x_hi = x.astype(BF16)
    x_lo = (x - x_hi.astype(F32)).astype(BF16)
    logits = _dot_nt(wt_hi, x_hi) + (_dot_nt(wt_lo, x_hi) + _dot_nt(wt_hi, x_lo))
    tm = x.shape[0]
    sub = lax.broadcasted_iota(jnp.int32, (SUBLANES, tm), 0)
    neg = -jnp.inf
    g_log = logits[0:MOE_GROUPS]
    g_max = jnp.max(g_log, 0, keepdims=True)
    g_sel = jnp.min(jnp.where(g_log == g_max, sub, MOE_GROUPS), 0, keepdims=True)
    g_w = 1.0 / jnp.sum(jnp.exp(g_log - g_max), 0, keepdims=True)
    e_log = logits[MOE_GROUPS:MOE_GROUPS + MOE_EPG]
    for g in range(1, MOE_GROUPS):
        e_log = jnp.where(g_sel == g, logits[MOE_GROUPS + g * MOE_EPG:MOE_GROUPS + (g + 1) * MOE_EPG], e_log)
    v1 = jnp.max(e_log, 0, keepdims=True)
    i1 = jnp.min(jnp.where(e_log == v1, sub, MOE_EPG), 0, keepdims=True)
    e_log2 = jnp.where(sub == i1, neg, e_log)
    v2 = jnp.max(e_log2, 0, keepdims=True)
    i2 = jnp.min(jnp.where(e_log2 == v2, sub, MOE_EPG), 0, keepdims=True)
    t = jnp.exp(v2 - v1)
    w1 = g_w / (1.0 + t)
    w2 = g_w * t / (1.0 + t)
    first = sub == 0
    return g_sel * MOE_EPG + jnp.where(first, i1, i2), jnp.where(first, w1, w2)


def _plan_body(idx_ref, dest_ref, pend_ref, carry, pstart):
    ph = pl.program_id(0)
    i = pl.program_id(1)
    tm = idx_ref.shape[0]
    lane = lax.broadcasted_iota(jnp.int32, (tm, LANES), 1)
    oh0 = (lane == idx_ref[:, 0:1]).astype(F32)
    oh1 = (lane == idx_ref[:, 1:2]).astype(F32)
    both = oh0 + oh1

    @pl.when((ph == 0) & (i == 0))
    def _():
        carry[...] = jnp.zeros(carry.shape, F32)

    @pl.when(ph == 0)
    def _():
        carry[...] += jnp.sum(both, 0, keepdims=True)

    @pl.when((ph == 1) & (i == 0))
    def _():
        padded = jnp.floor((carry[...] + (MOE_BLOCK - 1)) * (1.0 / MOE_BLOCK)) * MOE_BLOCK
        a = lax.broadcasted_iota(jnp.int32, (LANES, LANES), 0)
        b = lax.broadcasted_iota(jnp.int32, (LANES, LANES), 1)
        upper = (a < b).astype(F32)
        ps = jnp.dot(jnp.broadcast_to(padded, (SUBLANES, LANES)), upper, preferred_element_type=F32,
                     precision=lax.Precision.HIGHEST)
        pstart[...] = ps[0:1]
        carry[...] = jnp.zeros(carry.shape, F32)

    @pl.when(ph == 1)
    def _():
        r = lax.broadcasted_iota(jnp.int32, (tm, tm), 0)
        c = lax.broadcasted_iota(jnp.int32, (tm, tm), 1)
        before = (r > c).astype(BF16)
        base = _dot(before, both.astype(BF16)) + carry[...] + pstart[...]
        d0 = jnp.sum(oh0 * base, -1, keepdims=True)
        d1 = jnp.sum(oh1 * base, -1, keepdims=True)
        first = lax.broadcasted_iota(jnp.int32, (tm, 2), 1) == 0
        dest_ref[...] = jnp.where(first, d0, d1).astype(jnp.int32)
        carry[...] += jnp.sum(both, 0, keepdims=True)
        padded_tot = jnp.floor((carry[...] + (MOE_BLOCK - 1)) * (1.0 / MOE_BLOCK)) * MOE_BLOCK
        pend_ref[...] = jnp.broadcast_to(pstart[...] + padded_tot, pend_ref.shape)


def _plan(idx, *, tm):
    T = idx.shape[0]
    return pl.pallas_call(
        _plan_body,
        grid=(2, T // tm),
        in_specs=[pl.BlockSpec((tm, 2), lambda p, i: (i, 0))],
        out_specs=[pl.BlockSpec((tm, 2), lambda p, i: (i * p, 0)), pl.BlockSpec((SUBLANES, LANES), lambda p, i: (0, 0))],
        out_shape=[jax.ShapeDtypeStruct((T, 2), jnp.int32), jax.ShapeDtypeStruct((SUBLANES, LANES), F32)],
        scratch_shapes=[pltpu.VMEM((1, LANES), F32), pltpu.VMEM((1, LANES), F32)],
        compiler_params=_cparams(2),
        name="plan",
    )(idx)


def _row_copy(src_ref, s, dst_ref, d, sem):
    return pltpu.make_async_copy(src_ref.at[pl.ds(s, 1), :], dst_ref.at[pl.ds(d, 1), :], sem)


def _dispatch_body(pend_ref, dest_ref, xa_ref, xb_ref, xs_ref, sem, zbuf, zsem, *, nta):
    i = pl.program_id(0)
    tm = xa_ref.shape[0]
    n_blocks = xs_ref.shape[0] // MOE_BLOCK

    @pl.when(i == 0)
    def _zero_partial_blocks():
        zbuf[...] = jnp.zeros(zbuf.shape, zbuf.dtype)

        def block_copy(blk):
            row0 = pl.multiple_of(blk * MOE_BLOCK, MOE_BLOCK)
            return pltpu.make_async_copy(zbuf, xs_ref.at[pl.ds(row0, MOE_BLOCK), :], zsem)

        def tail(e):
            end = pend_ref[e]
            begin = jnp.where(e == 0, 0, pend_ref[jnp.maximum(e - 1, 0)])
            return end > begin, block_copy(jnp.maximum(end // MOE_BLOCK - 1, 0))

        n_used = pend_ref[MOE_EXPERTS - 1] // MOE_BLOCK
        for op in ("start", "wait"):
            def expert_tail(e, c):
                used, cp = tail(e)
                pl.when(used)(getattr(cp, op))
                return c

            def unused_block(blk, c):
                pl.when(blk >= n_used)(getattr(block_copy(blk), op))
                return c

            lax.fori_loop(0, MOE_EXPERTS, expert_tail, 0)
            lax.fori_loop(0, n_blocks, unused_block, 0)

    def scatter_tile(x_ref):
        def start(t, c):
            _row_copy(x_ref, t, xs_ref, dest_ref[0, 0, 2 * t], sem).start()
            _row_copy(x_ref, t, xs_ref, dest_ref[0, 0, 2 * t + 1], sem).start(priority=1)
            return c

        lax.fori_loop(0, tm, start, 0, unroll=ROW_DMA_UNROLL)
        for _ in range(2):
            pltpu.make_async_copy(x_ref, xs_ref.at[pl.ds(0, tm), :], sem).wait()

    pl.when(i < nta)(lambda: scatter_tile(xa_ref))
    pl.when(i >= nta)(lambda: scatter_tile(xb_ref))


def _dispatch(dest, xa, xb, pend, rows, *, tm):
    nta, ntb = xa.shape[0] // tm, xb.shape[0] // tm
    nt = nta + ntb
    grid_spec = pltpu.PrefetchScalarGridSpec(
        num_scalar_prefetch=1,
        grid=(nt,),
        in_specs=[
            pl.BlockSpec((1, 1, 2 * tm), lambda i, pe: (i, 0, 0), memory_space=pltpu.SMEM),
            pl.BlockSpec((tm, D_MODEL), lambda i, pe: (jnp.minimum(i, nta - 1), 0)),
            pl.BlockSpec((tm, D_MODEL), lambda i, pe: (jnp.maximum(i - nta, 0), 0)),
        ],
        out_specs=pl.BlockSpec(memory_space=pl.ANY),
        scratch_shapes=[pltpu.SemaphoreType.DMA(()), pltpu.VMEM((MOE_BLOCK, D_MODEL), F32),
                        pltpu.SemaphoreType.DMA(())],
    )
    return pl.pallas_call(
        functools.partial(_dispatch_body, nta=nta),
        grid_spec=grid_spec,
        out_shape=jax.ShapeDtypeStruct((rows, D_MODEL), F32),
        compiler_params=_cparams(1),
        name="dispatch",
    )(pend, dest.reshape(nt, 1, 2 * tm), xa, xb)


def _combine_body(dest_ref, x_ref, ew_ref, g_ref, b_ref, rows_ref, o_ref, gbuf, sem):
    tm = x_ref.shape[0]

    def start(t, c):
        _row_copy(rows_ref, dest_ref[0, 0, 2 * t], gbuf.at[0], t, sem).start()
        _row_copy(rows_ref, dest_ref[0, 0, 2 * t + 1], gbuf.at[1], t, sem).start(priority=1)
        return c

    lax.fori_loop(0, tm, start, 0, unroll=ROW_DMA_UNROLL)
    for half in range(2):
        pltpu.make_async_copy(rows_ref.at[pl.ds(0, tm), :], gbuf.at[half], sem).wait()
    y = ew_ref[:, 0:1] * gbuf[0] + ew_ref[:, 1:2] * gbuf[1]
    o_ref[...] = _layer_norm(ALPHA * x_ref[...] + y, g_ref[...], b_ref[...])


def _combine(dest, x, ew, g, b, rows, *, tm):
    T = x.shape[0]
    nt = T // tm
    const = lambda i: (0, 0)
    return pl.pallas_call(
        _combine_body,
        grid=(nt,),
        in_specs=[
            pl.BlockSpec((1, 1, 2 * tm), lambda i: (i, 0, 0), memory_space=pltpu.SMEM),
            pl.BlockSpec((tm, D_MODEL), lambda i: (i, 0)),
            pl.BlockSpec((tm, 2), lambda i: (i, 0)),
            pl.BlockSpec((1, D_MODEL), const),
            pl.BlockSpec((1, D_MODEL), const),
            pl.BlockSpec(memory_space=pl.ANY),
        ],
        out_specs=pl.BlockSpec((tm, D_MODEL), lambda i: (i, 0)),
        out_shape=jax.ShapeDtypeStruct((T, D_MODEL), F32),
        scratch_shapes=[pltpu.VMEM((2, tm, D_MODEL), F32), pltpu.SemaphoreType.DMA(())],
        compiler_params=_cparams(1),
        name="combine",
    )(dest.reshape(nt, 1, 2 * tm), x, ew, g, b, rows)


def _experts_body(be_ref, nu_ref, xs_ref, wg_ref, wu_ref, wd_ref, o_ref):
    del be_ref
    i = pl.program_id(0)

    @pl.when(i < nu_ref[0])
    def _():
        xb = xs_ref[...].astype(BF16)
        gate = _dot(xb, wg_ref[0].astype(BF16))
        up = _dot(xb, wu_ref[0].astype(BF16))
        hid = gate * jax.nn.sigmoid(gate) * up
        o_ref[...] = _dot(hid.astype(BF16), wd_ref[0].astype(BF16))

    @pl.when(i >= nu_ref[0])
    def _():
        o_ref[...] = jnp.zeros(o_ref.shape, F32)


def _experts(block_expert, n_used, xs, w_gate, w_up, w_down):
    R = xs.shape[0]
    nb = R // MOE_BLOCK
    grid_spec = pltpu.PrefetchScalarGridSpec(
        num_scalar_prefetch=2,
        grid=(nb,),
        in_specs=[
            pl.BlockSpec((MOE_BLOCK, D_MODEL), lambda i, be, nu: (jnp.minimum(i, jnp.maximum(nu[0] - 1, 0)), 0)),
            pl.BlockSpec((1, D_MODEL, MOE_FF), lambda i, be, nu: (be[i], 0, 0)),
            pl.BlockSpec((1, D_MODEL, MOE_FF), lambda i, be, nu: (be[i], 0, 0)),
            pl.BlockSpec((1, MOE_FF, D_MODEL), lambda i, be, nu: (be[i], 0, 0)),
        ],
        out_specs=pl.BlockSpec((MOE_BLOCK, D_MODEL), lambda i, be, nu: (i, 0)),
    )
    return pl.pallas_call(
        _experts_body,
        grid_spec=grid_spec,
        out_shape=jax.ShapeDtypeStruct((R, D_MODEL), F32),
        compiler_params=_cparams(1),
        name="experts",
    )(block_expert, n_used, xs, w_gate, w_up, w_down)


def _row_tile(T, pref):
    t = pref
    while T % t:
        t //= 2
    return t


def _group_to_x2(x, mem_k, mem_v, conv_prev, pool_prev, s0, pos0, W):
    B, L, _ = x.shape
    T = B * L
    xf = x.reshape(T, D_MODEL)
    tm = _row_tile(T, 512)
    mixer_w = (W["w_conv"], W["a_log"], W["dt_bias"], W["w_onorm"], W["w_pool"], W["pool_scale"])
    if L >= FUSE_MIN_SEQ:
        merged, s_new, conv_tail, pool_tail = _mixer(xf, B, L, conv_prev, pool_prev, s0, *mixer_w, pos0=pos0,
                                                     w_in=W["w_in"])
    else:
        proj = _mm(xf, W["w_in"], tm=tm, tn=N_PROJ // 3)
        merged, s_new, conv_tail, pool_tail = _mixer(proj, B, L, conv_prev, pool_prev, s0, *mixer_w, pos0=pos0)
    x1, q = _proj_ln(merged, W["w_out"], xf, W["ln1_g"], W["ln1_b"], (W["w_cq"],), tm=tm, route=False)
    if mem_k.ndim == 3:
        att = _attention(q, mem_k, mem_v, B, L)
    else:
        M = mem_k.shape[1]

        def rows_view(m):
            m = m.reshape(B, M, CA_HEADS, HALVES, LANES)
            return jnp.swapaxes(m, 2, 3).reshape(B * M * KV_ROWS, LANES)

        att = _attention_rows(q, rows_view(mem_k), rows_view(mem_v), B, L, M)
    x2, idx, ew = _proj_ln(att, W["w_co"], x1, W["ln2_g"], W["ln2_b"], (W["w_r_hi"], W["w_r_lo"]), tm=tm, route=True)
    idx, ew = idx[0:2].T, ew[0:2].T
    return x2, idx, ew, s_new, conv_tail[:, -(CONV_W - 1):], pool_tail[:, -(POOL_MAX - 1):]


def kernel(x_prompt, x_sample, cache_mem_k, cache_mem_v, state_delta, state_conv, state_pool, mem_prompt, w_in, w_conv, a_log, dt_bias, w_onorm, w_pool, pool_scale, w_out, ln1_g, ln1_b, w_cq, w_ck, w_cv, w_co, ln2_g, ln2_b, w_router_group, w_router_expert, w_gate, w_up, w_down, ln3_g, ln3_b):
    Bp, Lp, _ = x_prompt.shape
    Bs, Ls, _ = x_sample.shape
    Tp, Ts = Bp * Lp, Bs * Ls
    lyr = 0

    def pad_lanes(v):
        return jnp.pad(v.astype(F32), (0, LANES - v.shape[0])).reshape(1, LANES)

    o1, o2, o3, o4 = QKV_DIM, QKV_DIM + D_MODEL, QKV_DIM + D_MODEL + 2 * DN_HEADS, QKV_DIM + 2 * D_MODEL + 2 * DN_HEADS
    wi = w_in[lyr]
    w_in_r = jnp.concatenate(
        [wi[:, :o2], wi[:, o3:], wi[:, o2:o3], jnp.zeros((D_MODEL, LANES - 2 * DN_HEADS), F32)], axis=1).astype(BF16)
    del o1, o4
    w_r = jnp.concatenate([w_router_group[lyr], w_router_expert[lyr],
                           jnp.zeros((D_MODEL, LANES - MOE_GROUPS - MOE_EXPERTS), F32)], axis=1).T
    w_r_hi = w_r.astype(BF16)
    W = {
        "w_r_hi": w_r_hi,
        "w_r_lo": (w_r - w_r_hi.astype(F32)).astype(BF16),
        "w_in": w_in_r,
        "w_conv": w_conv[lyr],
        "a_log": pad_lanes(a_log[lyr]),
        "dt_bias": pad_lanes(dt_bias[lyr]),
        "w_onorm": w_onorm[lyr].reshape(1, DN_HD),
        "w_pool": w_pool[lyr].astype(BF16),
        "pool_scale": pool_scale[lyr].reshape(1, D_MODEL),
        "w_out": w_out[lyr].astype(BF16),
        "ln1_g": ln1_g[lyr].reshape(1, D_MODEL), "ln1_b": ln1_b[lyr].reshape(1, D_MODEL),
        "w_cq": w_cq[lyr].astype(BF16),
        "w_co": w_co[lyr].astype(BF16),
        "ln2_g": ln2_g[lyr].reshape(1, D_MODEL), "ln2_b": ln2_b[lyr].reshape(1, D_MODEL),
    }

    M = mem_prompt.shape[1]
    memf = mem_prompt.reshape(Bp * M, D_MODEL)
    tmm = _row_tile(Bp * M, 512)
    mk = _mm(memf, w_ck[lyr].astype(BF16), tm=tmm, tn=D_MODEL).reshape(Bp, M, D_MODEL)
    mv = _mm(memf, w_cv[lyr].astype(BF16), tm=tmm, tn=D_MODEL).reshape(Bp, M, D_MODEL)
    x2_p, idx_p, ew_p, d_p, c_p, p_p = _group_to_x2(x_prompt, mk, mv, None, None, None, 0, W)
    mk = mk.reshape(Bp, M, CA_HEADS, CA_HD)
    mv = mv.reshape(Bp, M, CA_HEADS, CA_HD)
    x2_s, idx_s, ew_s, d_s, c_s, p_s = _group_to_x2(x_sample, cache_mem_k[lyr], cache_mem_v[lyr], state_conv[lyr],
                                       state_pool[lyr], state_delta[lyr], PAST_LEN, W)

    tp, ts = _row_tile(Tp, 256), _row_tile(Ts, 256)
    T = Tp + Ts
    dest, pend = _plan(jnp.concatenate([idx_p, idx_s], axis=0), tm=_row_tile(T, 1024))
    n_blocks = (2 * T + MOE_EXPERTS * (MOE_BLOCK - 1) + MOE_BLOCK - 1) // MOE_BLOCK
    pend_e = pend[0, :MOE_EXPERTS].astype(jnp.int32)
    block_start = jnp.arange(n_blocks, dtype=jnp.int32) * MOE_BLOCK
    block_expert = jnp.minimum(jnp.sum(block_start[:, None] >= pend_e[None, :], axis=1), MOE_EXPERTS - 1).astype(jnp.int32)
    n_used = (pend_e[MOE_EXPERTS - 1:] // MOE_BLOCK).astype(jnp.int32)
    xs = _dispatch(dest, x2_p, x2_s, pend_e, n_blocks * MOE_BLOCK, tm=_row_tile(math.gcd(Tp, Ts), 256))
    rows = _experts(block_expert, n_used, xs, w_gate[lyr], w_up[lyr], w_down[lyr])
    g3, b3 = ln3_g[lyr].reshape(1, D_MODEL), ln3_b[lyr].reshape(1, D_MODEL)
    y_p = _combine(dest[:Tp], x2_p, ew_p, g3, b3, rows, tm=tp).reshape(Bp, Lp, D_MODEL)
    y_s = _combine(dest[Tp:], x2_s, ew_s, g3, b3, rows, tm=ts).reshape(Bs, Ls, D_MODEL)

    return (y_p, y_s, d_p[None], c_p[None], p_p[None], mk[None], mv[None], d_s[None], c_s[None], p_s[None])
```

```python
import functools
import math

import jax
import jax.numpy as jnp
from jax import lax
from jax.experimental import pallas as pl
from jax.experimental.pallas import tpu as pltpu

F32 = jnp.float32
BF16 = jnp.bfloat16

D_MODEL = 1024
DN_HEADS = 8
DN_HD = 128
QKV_DIM = 3 * DN_HEADS * DN_HD
CONV_W = 4
DN_CHUNK = 64
POOL_WINDOWS = (2, 4, 8, 16)
POOL_GD = D_MODEL // len(POOL_WINDOWS)
POOL_MAX = 16
CA_HEADS = 4
CA_HD = D_MODEL // CA_HEADS
MOE_GROUPS = 8
MOE_EPG = 8
MOE_EXPERTS = MOE_GROUPS * MOE_EPG
MOE_FF = D_MODEL // 4
MOE_BLOCK = 256
PAST_LEN = 16384
LN_EPS = 1e-5
NORM_EPS = 1e-6
ALPHA = 2.0 ** 0.25

LANES = 128
SUBLANES = 8
CONV_HALO = SUBLANES
POOL_HALO = POOL_MAX
MIN_CHUNK = 16
N_PROJ = 7 * D_MODEL + LANES
VMEM_LIMIT = 48 * 1024 * 1024
MIXER_VMEM_LIMIT = 56 * 1024 * 1024
ROW_DMA_UNROLL = 256
FUSE_MIN_SEQ = 128
SEQS_PER_STEP = 8


def _cparams(n_axes, vmem=VMEM_LIMIT):
    return pltpu.CompilerParams(dimension_semantics=("arbitrary",) * n_axes, vmem_limit_bytes=vmem)


def _layer_norm(x, g, b):
    mu = jnp.mean(x, -1, keepdims=True)
    xc = x - mu
    var = jnp.mean(xc * xc, -1, keepdims=True)
    return xc * lax.rsqrt(var + LN_EPS) * g + b


def _softplus(x):
    return jnp.maximum(x, 0.0) + jnp.log1p(jnp.exp(-jnp.abs(x)))


def _dot(a, b):
    return jnp.dot(a, b, preferred_element_type=F32)


def _dot_nt(a, b):
    return lax.dot_general(a, b, (((1,), (1,)), ((), ())), preferred_element_type=F32)


def _dot_tn(a, b):
    return lax.dot_general(a, b, (((0,), (0,)), ((), ())), preferred_element_type=F32)


def _mm_body(x_ref, w_ref, o_ref):
    o_ref[...] = _dot(x_ref[...].astype(BF16), w_ref[...]).astype(o_ref.dtype)


def _mm(x, w, *, tm, tn, out_dtype=F32):
    T, K = x.shape
    N = w.shape[1]
    return pl.pallas_call(
        _mm_body,
        grid=(N // tn, T // tm),
        in_specs=[pl.BlockSpec((tm, K), lambda j, i: (i, 0)), pl.BlockSpec((K, tn), lambda j, i: (0, j))],
        out_specs=pl.BlockSpec((tm, tn), lambda j, i: (i, j)),
        out_shape=jax.ShapeDtypeStruct((T, N), out_dtype),
        compiler_params=_cparams(2),
        name="mm",
    )(x, w)


def _mixer_body(*refs, tl, chunk, pos0, has_state, fused, nb):
    refs = list(refs)
    take = lambda n: [refs.pop(0) for _ in range(n)]
    if fused:
        x_ref, win_hbm = take(2)
    else:
        qkv_ref, z_ref, p_ref, ga_ref, gb_ref, ab_ref = take(6)
    if has_state:
        convp_ref, poolp_ref, s0_ref = take(3)
    wconv_ref, alog_ref, dtb_ref, wonorm_ref, wpool_ref, pscale_ref = take(6)
    merged_ref, sout_ref, convo_ref, poolo_ref = take(4)
    qkvbuf, pbuf, obuf, gcb, betab, gt_ref = take(6)
    if fused:
        z_ref, ga_ref, gb_ref, win_vmem, win_sem = take(5)
    b = pl.program_id(0)
    l = pl.program_id(1)
    tlp = max(tl, MIN_CHUNK)
    C = max(chunk, MIN_CHUNK)
    cps = tlp // C
    seqs = range(nb)

    @pl.when(l == 0)
    def _init():
        for s in seqs:
            qkvbuf[s, 0:CONV_HALO, :] = jnp.zeros((CONV_HALO, QKV_DIM), F32)
            pbuf[s, 0:POOL_HALO, :] = jnp.zeros((POOL_HALO, D_MODEL), F32)
            if has_state:
                qkvbuf[s, CONV_HALO - (CONV_W - 1):CONV_HALO, :] = convp_ref[s]
                pbuf[s, POOL_HALO - (POOL_MAX - 1):POOL_HALO, :] = poolp_ref[s]
        if has_state:
            sout_ref[...] = s0_ref[...]
        else:
            sout_ref[...] = jnp.zeros(sout_ref.shape, F32)

    if fused:
        @pl.when((b == 0) & (l == 0))
        def _load_weight():
            cp = pltpu.make_async_copy(win_hbm, win_vmem, win_sem)
            cp.start()
            cp.wait()

        xb = x_ref[...].astype(BF16)

        def proj_cols(c0, width):
            return _dot(xb, win_vmem[:, c0:c0 + width])

        ab = proj_cols(7 * D_MODEL, LANES)
        for c in range(QKV_DIM // D_MODEL):
            qkvbuf[0, CONV_HALO:CONV_HALO + tl, c * D_MODEL:(c + 1) * D_MODEL] = proj_cols(c * D_MODEL, D_MODEL)
        z_ref[...] = proj_cols(3 * D_MODEL, D_MODEL)
        pbuf[0, POOL_HALO:POOL_HALO + tl, :] = proj_cols(4 * D_MODEL, D_MODEL)
        ga_ref[...] = proj_cols(5 * D_MODEL, D_MODEL)
        gb_ref[...] = proj_cols(6 * D_MODEL, D_MODEL)
    else:
        ab = ab_ref[...]
        for s in seqs:
            qkvbuf[s, CONV_HALO:CONV_HALO + tl, :] = qkv_ref[s * tl:(s + 1) * tl, :]
            pbuf[s, POOL_HALO:POOL_HALO + tl, :] = p_ref[s * tl:(s + 1) * tl, :]

    def pad_seq(pieces):
        out = []
        for x in pieces:
            out.append(x)
            if tlp != tl:
                out.append(jnp.zeros((tlp - tl, x.shape[1]), F32))
        return out[0] if len(out) == 1 else jnp.concatenate(out, axis=0)

    def unpad_seq(x):
        if tlp == tl:
            return x
        return jnp.concatenate([x[s * tlp:s * tlp + tl] for s in seqs], axis=0)

    g_raw = -jnp.exp(alog_ref[...]) * _softplus(ab + dtb_ref[...])
    beta_raw = jax.nn.sigmoid(ab)
    g_all = pad_seq([g_raw[s * tl:(s + 1) * tl] for s in seqs])
    beta_all = pad_seq([beta_raw[s * tl:(s + 1) * tl] for s in seqs])
    R = nb * tlp
    ri = lax.broadcasted_iota(jnp.int32, (R, R), 0)
    ci = lax.broadcasted_iota(jnp.int32, (R, R), 1)
    cum_mat = ((ri >= ci) & ((ri // C) == (ci // C))).astype(F32)
    gcum = jnp.dot(cum_mat, g_all, preferred_element_type=F32, precision=lax.Precision.HIGHEST)
    gcum_t = gcum.T
    for h in range(DN_HEADS):
        gt_ref[h] = jnp.broadcast_to(gcum_t[h:h + 1, :], (SUBLANES, R))
        gcb[h] = jnp.broadcast_to(gcum[:, h:h + 1], (R, LANES))
        betab[h] = jnp.broadcast_to(beta_all[:, DN_HEADS + h:DN_HEADS + h + 1], (R, LANES))

    ii = lax.broadcasted_iota(jnp.int32, (C, C), 0)
    jj = lax.broadcasted_iota(jnp.int32, (C, C), 1)
    causal = ii >= jj
    strict = ii > jj
    eye = (ii == jj).astype(F32)
    n_sq = max(C.bit_length() - 2, 0)

    def conv_slab(col0):
        cols = slice(col0, col0 + DN_HD)
        pieces = []
        for s in seqs:
            acc = qkvbuf[s, CONV_HALO:CONV_HALO + tl, cols] * wconv_ref[CONV_W - 1:CONV_W, cols]
            for i in range(CONV_W - 1):
                r0 = CONV_HALO - (CONV_W - 1) + i
                acc = acc + qkvbuf[s, r0:r0 + tl, cols] * wconv_ref[i:i + 1, cols]
            pieces.append(acc * jax.nn.sigmoid(acc))
        return pad_seq(pieces)

    heads = range(DN_HEADS)
    chunks = range(nb * cps)
    probs = [(h, c) for h in heads for c in chunks]
    rows = [slice(c * C, (c + 1) * C) for c in chunks]
    q, k, v, gcs, bs, egs, kbs, grows = [], [], [], [], [], [], [], []
    for h in heads:
        qh = conv_slab(h * DN_HD)
        kh = conv_slab((DN_HEADS + h) * DN_HD)
        v.append(conv_slab((2 * DN_HEADS + h) * DN_HD))
        q.append(qh * lax.rsqrt(jnp.sum(qh * qh, -1, keepdims=True) + NORM_EPS) * (DN_HD ** -0.5))
        kh = kh * lax.rsqrt(jnp.sum(kh * kh, -1, keepdims=True) + NORM_EPS)
        k.append(kh)
        gcs.append(gcb[h])
        bs.append(betab[h])
        egs.append(jnp.exp(gcs[h]))
        kbs.append(kh * bs[h])
        grows.append(gt_ref[h][0:1, :])

    aq = {(h, c): _dot_nt(jnp.concatenate([kbs[h][rows[c]], q[h][rows[c]]], axis=0).astype(BF16),
                          k[h][rows[c]].astype(BF16)) for h, c in probs}
    decay = {(h, c): jnp.exp(jnp.where(causal, gcs[h][rows[c], 0:C] - grows[h][:, c * C:(c + 1) * C], -jnp.inf))
             for h, c in probs}
    A = {p: jnp.where(strict, aq[p][0:C] * decay[p], 0.0) for p in probs}
    qk = {p: (aq[p][C:2 * C] * decay[p]).astype(BF16) for p in probs}
    P = {p: eye - A[p] for p in probs}
    Q = {p: _dot(A[p].astype(BF16), A[p].astype(BF16)) for p in probs}
    for _ in range(n_sq - 1):
        pq = {p: _dot(jnp.concatenate([P[p], Q[p]], axis=0).astype(BF16), Q[p].astype(BF16)) for p in probs}
        P = {p: P[p] + pq[p][0:C] for p in probs}
        Q = {p: pq[p][C:2 * C] for p in probs}
    P = {p: P[p] + _dot(P[p].astype(BF16), Q[p].astype(BF16)) for p in probs}
    sol = {(h, c): _dot(P[(h, c)].astype(BF16),
                        jnp.concatenate([v[h][rows[c]] * bs[h][rows[c]], kbs[h][rows[c]] * egs[h][rows[c]]],
                                        axis=1).astype(BF16)) for h, c in probs}

    S = [[sout_ref[s, h] for h in heads] for s in seqs]
    sh = [(s, h) for s in seqs for h in heads]
    outs = {}
    for j in range(cps):
        cof = {s: s * cps + j for s in seqs}
        ws_qs = {(s, h): _dot(jnp.concatenate([sol[(h, cof[s])][:, DN_HD:2 * DN_HD],
                                               q[h][rows[cof[s]]] * egs[h][rows[cof[s]]]], axis=0).astype(BF16),
                              S[s][h].astype(BF16)) for s, h in sh}
        v_new = {(s, h): (sol[(h, cof[s])][:, 0:DN_HD] - ws_qs[(s, h)][0:C]).astype(BF16) for s, h in sh}
        glast = {(s, h): gcs[h][rows[cof[s]]][C - 1:C, :] for s, h in sh}
        S_new = {(s, h): S[s][h] * jnp.exp(glast[(s, h)])
                 + _dot_tn((k[h][rows[cof[s]]] * jnp.exp(glast[(s, h)] - gcs[h][rows[cof[s]]])).astype(BF16),
                           v_new[(s, h)]) for s, h in sh}
        S = [[S_new[(s, h)] for h in heads] for s in seqs]
        for s, h in sh:
            outs[(h, cof[s])] = ws_qs[(s, h)][C:2 * C] + _dot(qk[(h, cof[s])], v_new[(s, h)])
    for s, h in sh:
        sout_ref[s, h] = S[s][h]
    for h in heads:
        o = outs[(h, 0)] if len(chunks) == 1 else jnp.concatenate([outs[(h, c)] for c in chunks], axis=0)
        o = unpad_seq(o)
        o = o * lax.rsqrt(jnp.mean(o * o, -1, keepdims=True) + NORM_EPS) * wonorm_ref[...]
        zh = z_ref[:, h * DN_HD:(h + 1) * DN_HD]
        obuf[:, h * DN_HD:(h + 1) * DN_HD] = o * (zh * jax.nn.sigmoid(zh))

    if pos0 == 0:
        pos = l * tl + lax.broadcasted_iota(jnp.int32, (tl, 1), 0)
    for gi, win in enumerate(POOL_WINDOWS):
        cs = slice(gi * POOL_GD, (gi + 1) * POOL_GD)
        pooled = []
        for s in seqs:
            acc = pbuf[s, POOL_HALO:POOL_HALO + tl, cs]
            for j in range(1, win):
                acc = acc + pbuf[s, POOL_HALO - j:POOL_HALO - j + tl, cs]
            if pos0 == 0:
                cnt = jnp.minimum(win, pos + 1).astype(F32)
            else:
                cnt = float(min(win, pos0 + 1))
            pooled.append(acc / cnt - pbuf[s, POOL_HALO:POOL_HALO + tl, cs])
        pooled = pooled[0] if nb == 1 else jnp.concatenate(pooled, axis=0)
        bb = _dot(pooled.astype(BF16), wpool_ref[gi]) * pscale_ref[:, cs]
        merged_ref[:, cs] = (jax.nn.sigmoid(ga_ref[:, cs]) * obuf[:, cs]
                             + jax.nn.sigmoid(gb_ref[:, cs]) * bb).astype(merged_ref.dtype)

    for s in seqs:
        conv_tail = qkvbuf[s, tl:tl + CONV_HALO, :]
        pool_tail = pbuf[s, tl:tl + POOL_HALO, :]
        convo_ref[s] = conv_tail
        poolo_ref[s] = pool_tail
        if tl >= POOL_HALO:
            qkvbuf[s, 0:CONV_HALO, :] = conv_tail
            pbuf[s, 0:POOL_HALO, :] = pool_tail


def _mixer(src, B, L, conv_prev, pool_prev, s0, wconv, alog, dtb, wonorm, wpool, pscale, *, pos0, w_in=None):
    tl = min(256, L)
    nL = L // tl
    assert nL == 1 or tl >= POOL_HALO
    chunk = min(DN_CHUNK, tl)
    tlp = max(tl, MIN_CHUNK)
    has_state = s0 is not None
    fused = w_in is not None
    nb = next(n for n in (SEQS_PER_STEP, 2, 1) if B % n == 0) if (nL == 1 and not fused) else 1
    rt = nb * tl
    row = lambda b, l: b * nL + l
    if fused:
        in_specs = [pl.BlockSpec((rt, D_MODEL), lambda b, l: (row(b, l), 0)), pl.BlockSpec(memory_space=pl.ANY)]
        args = [src, w_in]
    else:
        in_specs = [
            pl.BlockSpec((rt, QKV_DIM), lambda b, l: (row(b, l), 0)),
            pl.BlockSpec((rt, D_MODEL), lambda b, l: (row(b, l), 3)),
            pl.BlockSpec((rt, D_MODEL), lambda b, l: (row(b, l), 4)),
            pl.BlockSpec((rt, D_MODEL), lambda b, l: (row(b, l), 5)),
            pl.BlockSpec((rt, D_MODEL), lambda b, l: (row(b, l), 6)),
            pl.BlockSpec((rt, LANES), lambda b, l: (row(b, l), 7 * D_MODEL // LANES)),
        ]
        args = [src] * 6
    if has_state:
        in_specs += [
            pl.BlockSpec((nb, CONV_W - 1, QKV_DIM), lambda b, l: (b, 0, 0)),
            pl.BlockSpec((nb, POOL_MAX - 1, D_MODEL), lambda b, l: (b, 0, 0)),
            pl.BlockSpec((nb, DN_HEADS, DN_HD, DN_HD), lambda b, l: (b, 0, 0, 0)),
        ]
        args += [conv_prev, pool_prev, s0]
    const2 = lambda b, l: (0, 0)
    in_specs += [
        pl.BlockSpec((CONV_W, QKV_DIM), const2),
        pl.BlockSpec((1, LANES), const2),
        pl.BlockSpec((1, LANES), const2),
        pl.BlockSpec((1, DN_HD), const2),
        pl.BlockSpec((len(POOL_WINDOWS), POOL_GD, POOL_GD), lambda b, l: (0, 0, 0)),
        pl.BlockSpec((1, D_MODEL), const2),
    ]
    args += [wconv, alog, dtb, wonorm, wpool, pscale]
    scratch = [
        pltpu.VMEM((nb, CONV_HALO + tl, QKV_DIM), F32),
        pltpu.VMEM((nb, POOL_HALO + tl, D_MODEL), F32),
        pltpu.VMEM((rt, D_MODEL), F32),
        pltpu.VMEM((DN_HEADS, nb * tlp, LANES), F32),
        pltpu.VMEM((DN_HEADS, nb * tlp, LANES), F32),
        pltpu.VMEM((DN_HEADS, SUBLANES, nb * tlp), F32),
    ]
    if fused:
        scratch += [pltpu.VMEM((tl, D_MODEL), F32)] * 3
        scratch += [pltpu.VMEM(w_in.shape, w_in.dtype), pltpu.SemaphoreType.DMA(())]
    return pl.pallas_call(
        functools.partial(_mixer_body, tl=tl, chunk=chunk, pos0=pos0, has_state=has_state, fused=fused, nb=nb),
        grid=(B // nb, nL),
        in_specs=in_specs,
        out_specs=[
            pl.BlockSpec((rt, D_MODEL), lambda b, l: (row(b, l), 0)),
            pl.BlockSpec((nb, DN_HEADS, DN_HD, DN_HD), lambda b, l: (b, 0, 0, 0)),
            pl.BlockSpec((nb, CONV_HALO, QKV_DIM), lambda b, l: (b, 0, 0)),
            pl.BlockSpec((nb, POOL_HALO, D_MODEL), lambda b, l: (b, 0, 0)),
        ],
        out_shape=[
            jax.ShapeDtypeStruct((B * L, D_MODEL), BF16),
            jax.ShapeDtypeStruct((B, DN_HEADS, DN_HD, DN_HD), F32),
            jax.ShapeDtypeStruct((B, CONV_HALO, QKV_DIM), F32),
            jax.ShapeDtypeStruct((B, POOL_HALO, D_MODEL), F32),
        ],
        scratch_shapes=scratch,
        compiler_params=_cparams(2, vmem=MIXER_VMEM_LIMIT),
        name="mixer",
    )(*args)


def _proj_ln_q_body(a_ref, w_ref, r_ref, g_ref, b_ref, wq_ref, o_ref, q_ref):
    y = ALPHA * r_ref[...] + _dot(a_ref[...], w_ref[...])
    x1 = _layer_norm(y, g_ref[...], b_ref[...])
    o_ref[...] = x1
    q_ref[...] = _dot(x1.astype(BF16), wq_ref[...]).astype(q_ref.dtype)


def _proj_ln_route_body(a_ref, w_ref, r_ref, g_ref, b_ref, whi_ref, wlo_ref, o_ref, idx_ref, ew_ref):
    y = ALPHA * r_ref[...] + _dot(a_ref[...], w_ref[...])
    x2 = _layer_norm(y, g_ref[...], b_ref[...])
    o_ref[...] = x2
    idx_ref[...], ew_ref[...] = _route(x2, whi_ref[...], wlo_ref[...])


def _proj_ln(a, w, resid, g, b, tail, *, tm, route):
    T = a.shape[0]
    const = lambda i: (0, 0)
    rows = lambda i: (i, 0)
    in_specs = [
        pl.BlockSpec((tm, D_MODEL), rows),
        pl.BlockSpec((D_MODEL, D_MODEL), const),
        pl.BlockSpec((tm, D_MODEL), rows),
        pl.BlockSpec((1, D_MODEL), const),
        pl.BlockSpec((1, D_MODEL), const),
    ] + [pl.BlockSpec(t.shape, const) for t in tail]
    out_specs = [pl.BlockSpec((tm, D_MODEL), rows)]
    out_shape = [jax.ShapeDtypeStruct((T, D_MODEL), F32)]
    if route:
        cols = lambda i: (0, i)
        out_specs += [pl.BlockSpec((SUBLANES, tm), cols), pl.BlockSpec((SUBLANES, tm), cols)]
        out_shape += [jax.ShapeDtypeStruct((SUBLANES, T), jnp.int32), jax.ShapeDtypeStruct((SUBLANES, T), F32)]
    else:
        out_specs += [pl.BlockSpec((tm, D_MODEL), rows)]
        out_shape += [jax.ShapeDtypeStruct((T, D_MODEL), BF16)]
    return pl.pallas_call(
        _proj_ln_route_body if route else _proj_ln_q_body,
        grid=(T // tm,),
        in_specs=in_specs,
        out_specs=out_specs,
        out_shape=out_shape,
        compiler_params=_cparams(1),
        name="proj_ln_route" if route else "proj_ln_q",
    )(a, w, resid, g, b, *tail)


def _attn_body(q_ref, k_ref, v_ref, o_ref):
    scale = CA_HD ** -0.5
    for hh in range(CA_HEADS):
        cs = slice(hh * CA_HD, (hh + 1) * CA_HD)
        s = _dot_nt(q_ref[:, cs].astype(BF16), k_ref[0, :, cs].astype(BF16)) * scale
        m = jnp.max(s, -1, keepdims=True)
        p = jnp.exp(s - m)
        denom = jnp.sum(p, -1, keepdims=True)
        o = _dot(p.astype(BF16), v_ref[0, :, cs].astype(BF16)) / denom
        o_ref[:, cs] = o.astype(o_ref.dtype)


def _attention(q, mem_k, mem_v, B, L):
    tq = min(512, L)
    nq = L // tq
    M = mem_k.shape[1]
    kv_spec = pl.BlockSpec((1, M, D_MODEL), lambda b, i: (b, 0, 0))
    return pl.pallas_call(
        _attn_body,
        grid=(B, nq),
        in_specs=[pl.BlockSpec((tq, D_MODEL), lambda b, i: (b * nq + i, 0)), kv_spec, kv_spec],
        out_specs=pl.BlockSpec((tq, D_MODEL), lambda b, i: (b * nq + i, 0)),
        out_shape=jax.ShapeDtypeStruct((B * L, D_MODEL), BF16),
        compiler_params=_cparams(2),
        name="attention",
    )(q, mem_k, mem_v)


HALVES = CA_HD // LANES
KV_ROWS = CA_HEADS * HALVES


def _attn_rows_body(q_ref, xk_ref, xv_ref, o_ref, *, bb, L, M):
    scale = CA_HD ** -0.5
    R = M * KV_ROWS
    lane = lax.broadcasted_iota(jnp.int32, (L, R), 1)
    q_all = q_ref[...].astype(F32)
    for i in range(bb):
        q = q_all[i * L:(i + 1) * L, :]
        qm = jnp.concatenate([q[:, j * LANES:(j + 1) * LANES] for j in range(KV_ROWS)], axis=0).astype(BF16)
        xk = xk_ref[i * R:(i + 1) * R, :].astype(BF16)
        xv = xv_ref[i * R:(i + 1) * R, :].astype(BF16)
        g = _dot_nt(qm, xk)
        ps, invs = [], []
        for hh in range(CA_HEADS):
            s = None
            for half in range(HALVES):
                j = hh * HALVES + half
                part = g[j * L:(j + 1) * L]
                if half:
                    part = pltpu.roll(part, R - half * CA_HEADS, 1)
                s = part if s is None else s + part
            s = jnp.where((lane % KV_ROWS) == hh, s * scale, -jnp.inf)
            p = jnp.exp(s - jnp.max(s, -1, keepdims=True))
            inv = 1.0 / jnp.sum(p, -1, keepdims=True)
            for half in range(HALVES):
                ps.append(pltpu.roll(p, half * CA_HEADS, 1) if half else p)
                invs.append(inv)
        o = _dot(jnp.concatenate(ps, axis=0).astype(BF16), xv)
        for j in range(KV_ROWS):
            o_ref[i * L:(i + 1) * L, j * LANES:(j + 1) * LANES] = (o[j * L:(j + 1) * L] * invs[j]).astype(o_ref.dtype)


def _attention_rows(q, xk, xv, B, L, M):
    bb = 4 if B % 4 == 0 else 1
    R = M * KV_ROWS
    kv_spec = pl.BlockSpec((bb * R, LANES), lambda b: (b, 0))
    return pl.pallas_call(
        functools.partial(_attn_rows_body, bb=bb, L=L, M=M),
        grid=(B // bb,),
        in_specs=[pl.BlockSpec((bb * L, D_MODEL), lambda b: (b, 0)), kv_spec, kv_spec],
        out_specs=pl.BlockSpec((bb * L, D_MODEL), lambda b: (b, 0)),
        out_shape=jax.ShapeDtypeStruct((B * L, D_MODEL), BF16),
        compiler_params=_cparams(1),
        name="attention_rows",
    )(q, xk, xv)


def _route(x, wt_hi, wt_lo):
    assert MOE_GROUPS == SUBLANES and MOE_EPG == SUBLANES
    x_hi = x.astype(BF16)
    x_lo = (x - x_hi.astype(F32)).astype(BF16)
    logits = _dot_nt(wt_hi, x_hi) + (_dot_nt(wt_lo, x_hi) + _dot_nt(wt_hi, x_lo))
    tm = x.shape[0]
    sub = lax.broadcasted_iota(jnp.int32, (SUBLANES, tm), 0)
    neg = -jnp.inf
    g_log = logits[0:MOE_GROUPS]
    g_max = jnp.max(g_log, 0, keepdims=True)
    g_sel = jnp.min(jnp.where(g_log == g_max, sub, MOE_GROUPS), 0, keepdims=True)
    g_w = 1.0 / jnp.sum(jnp.exp(g_log - g_max), 0, keepdims=True)
    e_log = logits[MOE_GROUPS:MOE_GROUPS + MOE_EPG]
    for g in range(1, MOE_GROUPS):
        e_log = jnp.where(g_sel == g, logits[MOE_GROUPS + g * MOE_EPG:MOE_GROUPS + (g + 1) * MOE_EPG], e_log)
    v1 = jnp.max(e_log, 0, keepdims=True)
    i1 = jnp.min(jnp.where(e_log == v1, sub, MOE_EPG), 0, keepdims=True)
    e_log2 = jnp.where(sub == i1, neg, e_log)
    v2 = jnp.max(e_log2, 0, keepdims=True)
    i2 = jnp.min(jnp.where(e_log2 == v2, sub, MOE_EPG), 0, keepdims=True)
    t = jnp.exp(v2 - v1)
    w1 = g_w / (1.0 + t)
    w2 = g_w * t / (1.0 + t)
    first = sub == 0
    return g_sel * MOE_EPG + jnp.where(first, i1, i2), jnp.where(first, w1, w2)


def _plan_body(idx_ref, dest_ref, pend_ref, carry, pstart):
    ph = pl.program_id(0)
    i = pl.program_id(1)
    tm = idx_ref.shape[0]
    lane = lax.broadcasted_iota(jnp.int32, (tm, LANES), 1)
    oh0 = (lane == idx_ref[:, 0:1]).astype(F32)
    oh1 = (lane == idx_ref[:, 1:2]).astype(F32)
    both = oh0 + oh1

    @pl.when((ph == 0) & (i == 0))
    def _():
        carry[...] = jnp.zeros(carry.shape, F32)

    @pl.when(ph == 0)
    def _():
        carry[...] += jnp.sum(both, 0, keepdims=True)

    @pl.when((ph == 1) & (i == 0))
    def _():
        padded = jnp.floor((carry[...] + (MOE_BLOCK - 1)) * (1.0 / MOE_BLOCK)) * MOE_BLOCK
        a = lax.broadcasted_iota(jnp.int32, (LANES, LANES), 0)
        b = lax.broadcasted_iota(jnp.int32, (LANES, LANES), 1)
        upper = (a < b).astype(F32)
        ps = jnp.dot(jnp.broadcast_to(padded, (SUBLANES, LANES)), upper, preferred_element_type=F32,
                     precision=lax.Precision.HIGHEST)
        pstart[...] = ps[0:1]
        carry[...] = jnp.zeros(carry.shape, F32)

    @pl.when(ph == 1)
    def _():
        r = lax.broadcasted_iota(jnp.int32, (tm, tm), 0)
        c = lax.broadcasted_iota(jnp.int32, (tm, tm), 1)
        before = (r > c).astype(BF16)
        base = _dot(before, both.astype(BF16)) + carry[...] + pstart[...]
        d0 = jnp.sum(oh0 * base, -1, keepdims=True)
        d1 = jnp.sum(oh1 * base, -1, keepdims=True)
        first = lax.broadcasted_iota(jnp.int32, (tm, 2), 1) == 0
        dest_ref[...] = jnp.where(first, d0, d1).astype(jnp.int32)
        carry[...] += jnp.sum(both, 0, keepdims=True)
        padded_tot = jnp.floor((carry[...] + (MOE_BLOCK - 1)) * (1.0 / MOE_BLOCK)) * MOE_BLOCK
        pend_ref[...] = jnp.broadcast_to(pstart[...] + padded_tot, pend_ref.shape)


def _plan(idx, *, tm):
    T = idx.shape[0]
    return pl.pallas_call(
        _plan_body,
        grid=(2, T // tm),
        in_specs=[pl.BlockSpec((tm, 2), lambda p, i: (i, 0))],
        out_specs=[pl.BlockSpec((tm, 2), lambda p, i: (i * p, 0)), pl.BlockSpec((SUBLANES, LANES), lambda p, i: (0, 0))],
        out_shape=[jax.ShapeDtypeStruct((T, 2), jnp.int32), jax.ShapeDtypeStruct((SUBLANES, LANES), F32)],
        scratch_shapes=[pltpu.VMEM((1, LANES), F32), pltpu.VMEM((1, LANES), F32)],
        compiler_params=_cparams(2),
        name="plan",
    )(idx)


def _row_copy(src_ref, s, dst_ref, d, sem):
    return pltpu.make_async_copy(src_ref.at[pl.ds(s, 1), :], dst_ref.at[pl.ds(d, 1), :], sem)


def _dispatch_body(pend_ref, dest_ref, xa_ref, xb_ref, xs_ref, xbuf, sems, zbuf, zsem, *, nta, nt):
    i = pl.program_id(0)
    tm = xa_ref.shape[0]
    n_blocks = xs_ref.shape[0] // MOE_BLOCK

    @pl.when(i == 0)
    def _zero_partial_blocks():
        zbuf[...] = jnp.zeros(zbuf.shape, zbuf.dtype)

        def block_copy(blk):
            row0 = pl.multiple_of(blk * MOE_BLOCK, MOE_BLOCK)
            return pltpu.make_async_copy(zbuf, xs_ref.at[pl.ds(row0, MOE_BLOCK), :], zsem)

        def tail(e):
            end = pend_ref[e]
            begin = jnp.where(e == 0, 0, pend_ref[jnp.maximum(e - 1, 0)])
            return end > begin, block_copy(jnp.maximum(end // MOE_BLOCK - 1, 0))

        n_used = pend_ref[MOE_EXPERTS - 1] // MOE_BLOCK
        for op in ("start", "wait"):
            def expert_tail(e, c):
                used, cp = tail(e)
                pl.when(used)(getattr(cp, op))
                return c

            def unused_block(blk, c):
                pl.when(blk >= n_used)(getattr(block_copy(blk), op))
                return c

            lax.fori_loop(0, MOE_EXPERTS, expert_tail, 0)
            lax.fori_loop(0, n_blocks, unused_block, 0)

    slot = i % 2

    def wait_slot(s):
        for _ in range(2):
            pltpu.make_async_copy(xbuf.at[s], xs_ref.at[pl.ds(0, tm), :], sems.at[s]).wait()

    pl.when(i >= 2)(lambda: wait_slot(slot))

    def scatter_tile(x_ref):
        xbuf[slot] = x_ref[...]

        def start(t, c):
            _row_copy(xbuf.at[slot], t, xs_ref, dest_ref[0, 0, 2 * t], sems.at[slot]).start()
            _row_copy(xbuf.at[slot], t, xs_ref, dest_ref[0, 0, 2 * t + 1], sems.at[slot]).start(priority=1)
            return c

        lax.fori_loop(0, tm, start, 0, unroll=ROW_DMA_UNROLL)

    pl.when(i < nta)(lambda: scatter_tile(xa_ref))
    pl.when(i >= nta)(lambda: scatter_tile(xb_ref))

    @pl.when(i == nt - 1)
    def _drain():
        wait_slot(slot)
        if nt >= 2:
            wait_slot(1 - slot)


def _dispatch(dest, xa, xb, pend, rows, *, tm):
    nta, ntb = xa.shape[0] // tm, xb.shape[0] // tm
    nt = nta + ntb
    grid_spec = pltpu.PrefetchScalarGridSpec(
        num_scalar_prefetch=1,
        grid=(nt,),
        in_specs=[
            pl.BlockSpec((1, 1, 2 * tm), lambda i, pe: (i, 0, 0), memory_space=pltpu.SMEM),
            pl.BlockSpec((tm, D_MODEL), lambda i, pe: (jnp.minimum(i, nta - 1), 0)),
            pl.BlockSpec((tm, D_MODEL), lambda i, pe: (jnp.maximum(i - nta, 0), 0)),
        ],
        out_specs=pl.BlockSpec(memory_space=pl.ANY),
        scratch_shapes=[pltpu.VMEM((2, tm, D_MODEL), F32), pltpu.SemaphoreType.DMA((2,)),
                        pltpu.VMEM((MOE_BLOCK, D_MODEL), F32), pltpu.SemaphoreType.DMA(())],
    )
    return pl.pallas_call(
        functools.partial(_dispatch_body, nta=nta, nt=nt),
        grid_spec=grid_spec,
        out_shape=jax.ShapeDtypeStruct((rows, D_MODEL), F32),
        compiler_params=_cparams(1),
        name="dispatch",
    )(pend, dest.reshape(nt, 1, 2 * tm), xa, xb)


def _combine_body(dest_ref, dest_next_ref, x_ref, ew_ref, g_ref, b_ref, rows_ref, o_ref, gbuf, sems, *, nt):
    i = pl.program_id(0)
    tm = x_ref.shape[0]

    def gather(d_ref, slot):
        def start(t, c):
            _row_copy(rows_ref, d_ref[0, 0, 2 * t], gbuf.at[slot, 0], t, sems.at[slot]).start()
            _row_copy(rows_ref, d_ref[0, 0, 2 * t + 1], gbuf.at[slot, 1], t, sems.at[slot]).start(priority=1)
            return c

        lax.fori_loop(0, tm, start, 0, unroll=ROW_DMA_UNROLL)

    slot = i % 2
    pl.when(i == 0)(lambda: gather(dest_ref, 0))
    pl.when(i + 1 < nt)(lambda: gather(dest_next_ref, 1 - slot))
    for half in range(2):
        pltpu.make_async_copy(rows_ref.at[pl.ds(0, tm), :], gbuf.at[slot, half], sems.at[slot]).wait()
    y = ew_ref[:, 0:1] * gbuf[slot, 0] + ew_ref[:, 1:2] * gbuf[slot, 1]
    o_ref[...] = _layer_norm(ALPHA * x_ref[...] + y, g_ref[...], b_ref[...])


def _combine(dest, x, ew, g, b, rows, *, tm):
    T = x.shape[0]
    nt = T // tm
    const = lambda i: (0, 0)
    dest3 = dest.reshape(nt, 1, 2 * tm)
    return pl.pallas_call(
        functools.partial(_combine_body, nt=nt),
        grid=(nt,),
        in_specs=[
            pl.BlockSpec((1, 1, 2 * tm), lambda i: (i, 0, 0), memory_space=pltpu.SMEM),
            pl.BlockSpec((1, 1, 2 * tm), lambda i: (jnp.minimum(i + 1, nt - 1), 0, 0), memory_space=pltpu.SMEM),
            pl.BlockSpec((tm, D_MODEL), lambda i: (i, 0)),
            pl.BlockSpec((tm, 2), lambda i: (i, 0)),
            pl.BlockSpec((1, D_MODEL), const),
            pl.BlockSpec((1, D_MODEL), const),
            pl.BlockSpec(memory_space=pl.ANY),
        ],
        out_specs=pl.BlockSpec((tm, D_MODEL), lambda i: (i, 0)),
        out_shape=jax.ShapeDtypeStruct((T, D_MODEL), F32),
        scratch_shapes=[pltpu.VMEM((2, 2, tm, D_MODEL), F32), pltpu.SemaphoreType.DMA((2,))],
        compiler_params=_cparams(1),
        name="combine",
    )(dest3, dest3, x, ew, g, b, rows)


def _experts_body(be_ref, nu_ref, xs_ref, wg_ref, wu_ref, wd_ref, o_ref):
    del be_ref
    i = pl.program_id(0)

    @pl.when(i < nu_ref[0])
    def _():
        xb = xs_ref[...].astype(BF16)
        gate = _dot(xb, wg_ref[0].astype(BF16))
        up = _dot(xb, wu_ref[0].astype(BF16))
        hid = gate * jax.nn.sigmoid(gate) * up
        o_ref[...] = _dot(hid.astype(BF16), wd_ref[0].astype(BF16))

    @pl.when(i >= nu_ref[0])
    def _():
        o_ref[...] = jnp.zeros(o_ref.shape, F32)


def _experts(block_expert, n_used, xs, w_gate, w_up, w_down):
    R = xs.shape[0]
    nb = R // MOE_BLOCK
    grid_spec = pltpu.PrefetchScalarGridSpec(
        num_scalar_prefetch=2,
        grid=(nb,),
        in_specs=[
            pl.BlockSpec((MOE_BLOCK, D_MODEL), lambda i, be, nu: (jnp.minimum(i, jnp.maximum(nu[0] - 1, 0)), 0)),
            pl.BlockSpec((1, D_MODEL, MOE_FF), lambda i, be, nu: (be[i], 0, 0)),
            pl.BlockSpec((1, D_MODEL, MOE_FF), lambda i, be, nu: (be[i], 0, 0)),
            pl.BlockSpec((1, MOE_FF, D_MODEL), lambda i, be, nu: (be[i], 0, 0)),
        ],
        out_specs=pl.BlockSpec((MOE_BLOCK, D_MODEL), lambda i, be, nu: (i, 0)),
    )
    return pl.pallas_call(
        _experts_body,
        grid_spec=grid_spec,
        out_shape=jax.ShapeDtypeStruct((R, D_MODEL), F32),
        compiler_params=_cparams(1),
        name="experts",
    )(block_expert, n_used, xs, w_gate, w_up, w_down)


def _row_tile(T, pref):
    t = pref
    while T % t:
        t //= 2
    return t


def _group_to_x2(x, mem_k, mem_v, conv_prev, pool_prev, s0, pos0, W):
    B, L, _ = x.shape
    T = B * L
    xf = x.reshape(T, D_MODEL)
    tm = _row_tile(T, 512)
    mixer_w = (W["w_conv"], W["a_log"], W["dt_bias"], W["w_onorm"], W["w_pool"], W["pool_scale"])
    if L >= FUSE_MIN_SEQ:
        merged, s_new, conv_tail, pool_tail = _mixer(xf, B, L, conv_prev, pool_prev, s0, *mixer_w, pos0=pos0,
                                                     w_in=W["w_in"])
    else:
        proj = _mm(xf, W["w_in"], tm=tm, tn=N_PROJ // 3)
        merged, s_new, conv_tail, pool_tail = _mixer(proj, B, L, conv_prev, pool_prev, s0, *mixer_w, pos0=pos0)
    x1, q = _proj_ln(merged, W["w_out"], xf, W["ln1_g"], W["ln1_b"], (W["w_cq"],), tm=tm, route=False)
    if mem_k.ndim == 3:
        att = _attention(q, mem_k, mem_v, B, L)
    else:
        M = mem_k.shape[1]

        def rows_view(m):
            m = m.reshape(B, M, CA_HEADS, HALVES, LANES)
            return jnp.swapaxes(m, 2, 3).reshape(B * M * KV_ROWS, LANES)

        att = _attention_rows(q, rows_view(mem_k), rows_view(mem_v), B, L, M)
    x2, idx, ew = _proj_ln(att, W["w_co"], x1, W["ln2_g"], W["ln2_b"], (W["w_r_hi"], W["w_r_lo"]), tm=tm, route=True)
    idx, ew = idx[0:2].T, ew[0:2].T
    return x2, idx, ew, s_new, conv_tail[:, -(CONV_W - 1):], pool_tail[:, -(POOL_MAX - 1):]


def kernel(x_prompt, x_sample, cache_mem_k, cache_mem_v, state_delta, state_conv, state_pool, mem_prompt, w_in, w_conv, a_log, dt_bias, w_onorm, w_pool, pool_scale, w_out, ln1_g, ln1_b, w_cq, w_ck, w_cv, w_co, ln2_g, ln2_b, w_router_group, w_router_expert, w_gate, w_up, w_down, ln3_g, ln3_b):
    Bp, Lp, _ = x_prompt.shape
    Bs, Ls, _ = x_sample.shape
    Tp, Ts = Bp * Lp, Bs * Ls
    lyr = 0

    def pad_lanes(v):
        return jnp.pad(v.astype(F32), (0, LANES - v.shape[0])).reshape(1, LANES)

    o1, o2, o3, o4 = QKV_DIM, QKV_DIM + D_MODEL, QKV_DIM + D_MODEL + 2 * DN_HEADS, QKV_DIM + 2 * D_MODEL + 2 * DN_HEADS
    wi = w_in[lyr]
    w_in_r = jnp.concatenate(
        [wi[:, :o2], wi[:, o3:], wi[:, o2:o3], jnp.zeros((D_MODEL, LANES - 2 * DN_HEADS), F32)], axis=1).astype(BF16)
    del o1, o4
    w_r = jnp.concatenate([w_router_group[lyr], w_router_expert[lyr],
                           jnp.zeros((D_MODEL, LANES - MOE_GROUPS - MOE_EXPERTS), F32)], axis=1).T
    w_r_hi = w_r.astype(BF16)
    W = {
        "w_r_hi": w_r_hi,
        "w_r_lo": (w_r - w_r_hi.astype(F32)).astype(BF16),
        "w_in": w_in_r,
        "w_conv": w_conv[lyr],
        "a_log": pad_lanes(a_log[lyr]),
        "dt_bias": pad_lanes(dt_bias[lyr]),
        "w_onorm": w_onorm[lyr].reshape(1, DN_HD),
        "w_pool": w_pool[lyr].astype(BF16),
        "pool_scale": pool_scale[lyr].reshape(1, D_MODEL),
        "w_out": w_out[lyr].astype(BF16),
        "ln1_g": ln1_g[lyr].reshape(1, D_MODEL), "ln1_b": ln1_b[lyr].reshape(1, D_MODEL),
        "w_cq": w_cq[lyr].astype(BF16),
        "w_co": w_co[lyr].astype(BF16),
        "ln2_g": ln2_g[lyr].reshape(1, D_MODEL), "ln2_b": ln2_b[lyr].reshape(1, D_MODEL),
    }

    M = mem_prompt.shape[1]
    memf = mem_prompt.reshape(Bp * M, D_MODEL)
    tmm = _row_tile(Bp * M, 512)
    mk = _mm(memf, w_ck[lyr].astype(BF16), tm=tmm, tn=D_MODEL).reshape(Bp, M, D_MODEL)
    mv = _mm(memf, w_cv[lyr].astype(BF16), tm=tmm, tn=D_MODEL).reshape(Bp, M, D_MODEL)
    x2_p, idx_p, ew_p, d_p, c_p, p_p = _group_to_x2(x_prompt, mk, mv, None, None, None, 0, W)
    mk = mk.reshape(Bp, M, CA_HEADS, CA_HD)
    mv = mv.reshape(Bp, M, CA_HEADS, CA_HD)
    x2_s, idx_s, ew_s, d_s, c_s, p_s = _group_to_x2(x_sample, cache_mem_k[lyr], cache_mem_v[lyr], state_conv[lyr],
                                       state_pool[lyr], state_delta[lyr], PAST_LEN, W)

    tp, ts = _row_tile(Tp, 256), _row_tile(Ts, 256)
    T = Tp + Ts
    dest, pend = _plan(jnp.concatenate([idx_p, idx_s], axis=0), tm=_row_tile(T, 1024))
    n_blocks = (2 * T + MOE_EXPERTS * (MOE_BLOCK - 1) + MOE_BLOCK - 1) // MOE_BLOCK
    pend_e = pend[0, :MOE_EXPERTS].astype(jnp.int32)
    block_start = jnp.arange(n_blocks, dtype=jnp.int32) * MOE_BLOCK
    block_expert = jnp.minimum(jnp.sum(block_start[:, None] >= pend_e[None, :], axis=1), MOE_EXPERTS - 1).astype(jnp.int32)
    n_used = (pend_e[MOE_EXPERTS - 1:] // MOE_BLOCK).astype(jnp.int32)
    xs = _dispatch(dest, x2_p, x2_s, pend_e, n_blocks * MOE_BLOCK, tm=_row_tile(math.gcd(Tp, Ts), 256))
    rows = _experts(block_expert, n_used, xs, w_gate[lyr], w_up[lyr], w_down[lyr])
    g3, b3 = ln3_g[lyr].reshape(1, D_MODEL), ln3_b[lyr].reshape(1, D_MODEL)
    y_p = _combine(dest[:Tp], x2_p, ew_p, g3, b3, rows, tm=tp).reshape(Bp, Lp, D_MODEL)
    y_s = _combine(dest[Tp:], x2_s, ew_s, g3, b3, rows, tm=ts).reshape(Bs, Ls, D_MODEL)

    return (y_p, y_s, d_p[None], c_p[None], p_p[None], mk[None], mv[None], d_s[None], c_s[None], p_s[None])
```
